```python
import math
import jax, jax.numpy as jnp
from jax import lax
import numpy as np

D_MODEL = 4096
BATCH = 1
SEQ = 16384
DEPTH = 2
DEC_BATCH = 32
DEC_SEQ = 64
PAST_LEN = 4096

CHUNK = 64
QBLOCK = 128
HG_BLOCK = 16
CONV_W = 4
N_EVEN = (DEPTH + 1) // 2
N_ODD = DEPTH // 2
NORM_EPS = 1e-6

SSD_HEADS = 64
SSD_HEAD_DIM = 64
SSD_INNER = SSD_HEADS * SSD_HEAD_DIM
SSD_GROUPS = 8
SSD_STATE = 128
SSD_CONV_DIM = SSD_INNER + 2 * SSD_GROUPS * SSD_STATE
GDN_HEADS = 32
GDN_DK = 128
GDN_DV = 128
GDN_QK = GDN_HEADS * GDN_DK
GDN_V = GDN_HEADS * GDN_DV
GDN_CONV_DIM = 2 * GDN_QK + GDN_V
EVEN_IN = SSD_INNER + SSD_CONV_DIM + SSD_HEADS + GDN_CONV_DIM + 2 * GDN_HEADS + GDN_V
EVEN_MIX = SSD_INNER + GDN_V
MLA_HEADS = 32
MLA_Q_RANK = 1024
MLA_KV_RANK = 512
MLA_NOPE = 128
MLA_ROPE = 64
MLA_V = 128
ROPE_THETA = 10000.0
HG_HEADS = 32
HG_DK = 128
HG_DV = 128
HG_K = HG_HEADS * HG_DK
HG_VW = HG_HEADS * HG_DV
ODD_IN = MLA_Q_RANK + MLA_KV_RANK + MLA_ROPE + 2 * HG_K + 2 * HG_VW
ODD_MIX = MLA_HEADS * MLA_V + HG_VW
MEM_LEN = 256
XA_HEADS = 4
XA_HEAD_DIM = 128
XA_DIM = XA_HEADS * XA_HEAD_DIM
D_FF = 4 * D_MODEL

kernel_name = "hybrid_streaming_encoder_step"


def _rms(x, g):
    xf = x.astype(jnp.float32)
    y = xf * lax.rsqrt(jnp.mean(xf * xf, axis=-1, keepdims=True) + NORM_EPS)
    return (y * g.astype(jnp.float32)).astype(x.dtype)


def _l2norm(x):
    return x * lax.rsqrt(jnp.sum(x * x, axis=-1, keepdims=True) + 1e-6)


def _split(a, sizes):
    return jnp.split(a, np.cumsum(sizes)[:-1].tolist(), axis=-1)


def _block_len(n, cap):
    for d in range(min(cap, n), 0, -1):
        if n % d == 0:
            return d
    return 1


def _to_chunks(a, n):
    b, l = a.shape[:2]
    return jnp.moveaxis(a.reshape((b, n, l // n) + a.shape[2:]), 1, 0)


def _from_chunks(a):
    a = jnp.moveaxis(a, 0, 1)
    return a.reshape((a.shape[0], a.shape[1] * a.shape[2]) + a.shape[3:])


def _causal_conv(u, buf, w, b=None):
    l = u.shape[1]
    full = jnp.concatenate([buf.astype(u.dtype), u], axis=1)
    out = full[:, :l] * w[0]
    for j in range(1, CONV_W):
        out = out + full[:, j:j + l] * w[j]
    if b is not None:
        out = out + b
    return out, full[:, l:]


def _rope(x, pos):
    half = x.shape[-1] // 2
    inv = ROPE_THETA ** (-jnp.arange(half, dtype=jnp.float32) / half)
    ang = pos[:, None] * inv[None, :]
    ang = ang.reshape((ang.shape[0],) + (1,) * (x.ndim - 3) + (half,))
    cos, sin = jnp.cos(ang), jnp.sin(ang)
    xf = x.astype(jnp.float32)
    x1, x2 = xf[..., :half], xf[..., half:]
    return jnp.concatenate([x1 * cos - x2 * sin, x2 * cos + x1 * sin], axis=-1).astype(x.dtype)


def _ssd_scan(x, dt, a, bm, cm, s0):
    lc = _block_len(x.shape[1], CHUNK)
    n = x.shape[1] // lc
    incl = jnp.tril(jnp.ones((lc, lc), bool))[None, :, :, None, None]

    def step(s, inp):
        xc, dtc, bc, cc = inp
        cs = jnp.cumsum(dtc * a, axis=1)
        decay = jnp.exp(jnp.where(incl, cs[:, :, None] - cs[:, None, :], -jnp.inf))
        cb = jnp.einsum('btgn,bsgn->btsg', cc, bc)
        y = (jnp.einsum('btsg,btsgh,bsgh,bsghp->btghp', cb, decay, dtc, xc)
             + jnp.einsum('btgn,bghpn->btghp', cc, s) * jnp.exp(cs)[..., None])
        w_end = jnp.exp(cs[:, -1:] - cs) * dtc
        s = s * jnp.exp(cs[:, -1])[..., None, None] + jnp.einsum('bsgh,bsghp,bsgn->bghpn', w_end, xc, bc)
        return s, y

    s, y = lax.scan(step, s0, (_to_chunks(x, n), _to_chunks(dt, n), _to_chunks(bm, n), _to_chunks(cm, n)))
    return _from_chunks(y), s


def _gdn_scan(q, k, v, g, beta, s0):
    lc = _block_len(q.shape[1], CHUNK)
    n = q.shape[1] // lc
    dv = v.shape[-1]
    incl = jnp.tril(jnp.ones((lc, lc), bool))
    strict = jnp.tril(jnp.ones((lc, lc), bool), -1)
    eye = jnp.eye(lc, dtype=jnp.float32)

    def step(s, inp):
        qc, kc, vc, gc, bc = inp
        cs = jnp.cumsum(gc, axis=1)
        cs_h = jnp.swapaxes(cs, 1, 2)
        decay = jnp.exp(jnp.where(incl, cs_h[:, :, :, None] - cs_h[:, :, None, :], -jnp.inf))
        kb = kc * bc[..., None]
        lower = jnp.where(strict, jnp.einsum('bthk,bshk->bhts', kb, kc) * decay, 0.0)
        rhs = jnp.concatenate([jnp.swapaxes(vc * bc[..., None], 1, 2),
                               jnp.swapaxes(kb * jnp.exp(cs)[..., None], 1, 2)], axis=-1)
        sol = lax.linalg.triangular_solve(eye + lower, rhs, left_side=True, lower=True, unit_diagonal=True)
        u, w = sol[..., :dv], sol[..., dv:]
        v_new = u - jnp.einsum('bhtk,bhkv->bhtv', w, s)
        qk = jnp.einsum('bthk,bshk->bhts', qc, kc) * decay
        o = (jnp.einsum('bthk,bhkv->bthv', qc * jnp.exp(cs)[..., None], s)
             + jnp.einsum('bhts,bhsv->bthv', qk, v_new))
        k_end = kc * jnp.exp(cs[:, -1:] - cs)[..., None]
        s = s * jnp.exp(cs[:, -1])[..., None, None] + jnp.einsum('bshk,bhsv->bhkv', k_end, v_new)
        return s, o

    s, o = lax.scan(step, s0, (_to_chunks(q, n), _to_chunks(k, n), _to_chunks(v, n), _to_chunks(g, n), _to_chunks(beta, n)))
    return _from_chunks(o), s


def _hgrn_scan(q, logf, k, v, s0):
    lc = _block_len(q.shape[1], HG_BLOCK)
    n = q.shape[1] // lc
    incl = jnp.tril(jnp.ones((lc, lc), bool))[None, :, :, None, None]

    def step(s, inp):
        qc, lf, kc, vc = inp
        b = jnp.cumsum(lf, axis=1)
        decay = jnp.exp(jnp.where(incl, b[:, :, None] - b[:, None, :], -jnp.inf))
        att = jnp.einsum('bthk,bshk,btshk->bhts', qc, kc, decay)
        o = (jnp.einsum('bthk,bhkv->bthv', qc * jnp.exp(b), s)
             + jnp.einsum('bhts,bshv->bthv', att, vc))
        k_end = kc * jnp.exp(b[:, -1:] - b)
        s = s * jnp.exp(b[:, -1])[..., None] + jnp.einsum('bshk,bshv->bhkv', k_end, vc)
        return s, o

    s, o = lax.scan(step, s0, (_to_chunks(q, n), _to_chunks(logf, n), _to_chunks(k, n), _to_chunks(v, n)))
    return _from_chunks(o), s


def _mla_attention(q_lat, q_rope, c_kv, k_rope, q_pos, k_pos):
    l = q_lat.shape[1]
    qb = _block_len(l, QBLOCK)
    n = l // qb
    scale = (MLA_NOPE + MLA_ROPE) ** -0.5
    k_chunk = k_pos // CHUNK

    def one_stream(args):
        ql, qr, ck, kr = args

        def one_block(bargs):
            qlb, qrb, qpb = bargs
            s = (jnp.einsum('qhr,tr->hqt', qlb, ck) + jnp.einsum('qhp,tp->hqt', qrb, kr)).astype(jnp.float32) * scale
            allowed = k_chunk[None, :] <= (qpb // CHUNK)[:, None]
            pr = jax.nn.softmax(jnp.where(allowed[None], s, -jnp.inf), axis=-1)
            return jnp.einsum('hqt,tr->qhr', pr.astype(ck.dtype), ck)

        out = lax.map(one_block, (ql.reshape((n, qb) + ql.shape[1:]), qr.reshape((n, qb) + qr.shape[1:]), q_pos.reshape(n, qb)))
        return out.reshape(ql.shape)

    return lax.map(one_stream, (q_lat, q_rope, c_kv, k_rope))


def _even_mixer(h, p, e, ssd_s, ssd_buf, gdn_s, gdn_buf):
    bsz, l, _ = h.shape
    hg = SSD_HEADS // SSD_GROUPS
    proj = h @ p['w_in_even'][e]
    z, xbc, dt_raw, qkv, a_raw, b_raw, gate = _split(
        proj, [SSD_INNER, SSD_CONV_DIM, SSD_HEADS, GDN_CONV_DIM, GDN_HEADS, GDN_HEADS, GDN_V])
    xbc, ssd_buf = _causal_conv(xbc, ssd_buf, p['ssd_conv_w'][e], p['ssd_conv_b'][e])
    xbc = jax.nn.silu(xbc.astype(jnp.float32))
    xs, bm, cm = _split(xbc, [SSD_INNER, SSD_GROUPS * SSD_STATE, SSD_GROUPS * SSD_STATE])
    xs = xs.reshape(bsz, l, SSD_GROUPS, hg, SSD_HEAD_DIM)
    dt = jax.nn.softplus(dt_raw.astype(jnp.float32) + p['ssd_dt_bias'][e].astype(jnp.float32)).reshape(bsz, l, SSD_GROUPS, hg)
    a = -jnp.exp(p['ssd_a_log'][e].astype(jnp.float32)).reshape(SSD_GROUPS, hg)
    y, ssd_s = _ssd_scan(xs, dt, a, bm.reshape(bsz, l, SSD_GROUPS, SSD_STATE), cm.reshape(bsz, l, SSD_GROUPS, SSD_STATE),
                         ssd_s.astype(jnp.float32).reshape(bsz, SSD_GROUPS, hg, SSD_HEAD_DIM, SSD_STATE))
    y = y + p['ssd_d'][e].astype(jnp.float32).reshape(SSD_GROUPS, hg)[..., None] * xs
    y = y.reshape(bsz, l, SSD_INNER) * jax.nn.silu(z.astype(jnp.float32))
    y = _rms(y.reshape(bsz, l, SSD_GROUPS, SSD_INNER // SSD_GROUPS),
             p['ssd_norm'][e].reshape(SSD_GROUPS, SSD_INNER // SSD_GROUPS)).reshape(bsz, l, SSD_INNER)
    qkv, gdn_buf = _causal_conv(qkv, gdn_buf, p['gdn_conv_w'][e])
    qkv = jax.nn.silu(qkv.astype(jnp.float32))
    q, k, v = _split(qkv, [GDN_QK, GDN_QK, GDN_V])
    q = _l2norm(q.reshape(bsz, l, GDN_HEADS, GDN_DK)) * GDN_DK ** -0.5
    k = _l2norm(k.reshape(bsz, l, GDN_HEADS, GDN_DK))
    v = v.reshape(bsz, l, GDN_HEADS, GDN_DV)
    g = -jnp.exp(p['gdn_a_log'][e].astype(jnp.float32)) * jax.nn.softplus(a_raw.astype(jnp.float32) + p['gdn_dt_bias'][e].astype(jnp.float32))
    beta = jax.nn.sigmoid(b_raw.astype(jnp.float32))
    o, gdn_s = _gdn_scan(q, k, v, g, beta, gdn_s.astype(jnp.float32))
    o = _rms(o, p['gdn_norm'][e]) * jax.nn.silu(gate.astype(jnp.float32).reshape(bsz, l, GDN_HEADS, GDN_DV))
    mix = jnp.concatenate([y, o.reshape(bsz, l, GDN_V)], axis=-1).astype(h.dtype) @ p['w_out_even'][e]
    ssd_s = ssd_s.reshape(bsz, SSD_HEADS, SSD_HEAD_DIM, SSD_STATE)
    return mix, (ssd_s.astype(h.dtype), ssd_buf.astype(h.dtype), gdn_s.astype(h.dtype), gdn_buf.astype(h.dtype))


def _odd_mixer(h, p, j, lb, ckv_past, kr_past, hg_s):
    bsz, l, _ = h.shape
    t_past = ckv_past.shape[1]
    proj = h @ p['w_in_odd'][j]
    cq, ckv, kr, hq, hf, hi, hgate = _split(proj, [MLA_Q_RANK, MLA_KV_RANK, MLA_ROPE, HG_K, HG_K, HG_VW, HG_VW])
    q_pos = t_past + jnp.arange(l, dtype=jnp.int32)
    k_pos = jnp.arange(t_past + l, dtype=jnp.int32)
    posf = q_pos.astype(jnp.float32)
    q = (_rms(cq, p['mla_q_norm'][j]) @ p['mla_w_uq'][j]).reshape(bsz, l, MLA_HEADS, MLA_NOPE + MLA_ROPE)
    q_nope, q_rope = q[..., :MLA_NOPE], _rope(q[..., MLA_NOPE:], posf)
    c_kv = _rms(ckv, p['mla_kv_norm'][j])
    k_rope = _rope(kr, posf)
    q_lat = jnp.einsum('blhd,rhd->blhr', q_nope, p['mla_w_uk'][j])
    ckv_all = jnp.concatenate([ckv_past.astype(c_kv.dtype), c_kv], axis=1)
    kr_all = jnp.concatenate([kr_past.astype(k_rope.dtype), k_rope], axis=1)
    o_lat = _mla_attention(q_lat, q_rope, ckv_all, kr_all, q_pos, k_pos)
    o_mla = jnp.einsum('blhr,rhv->blhv', o_lat, p['mla_w_uv'][j]).reshape(bsz, l, MLA_HEADS * MLA_V)
    hf = hf.astype(jnp.float32)
    logf = jnp.log(lb + (1.0 - lb) * jax.nn.sigmoid(hf))
    kk = (1.0 - lb) * jax.nn.sigmoid(-hf)
    o_hg, hg_s = _hgrn_scan(hq.astype(jnp.float32).reshape(bsz, l, HG_HEADS, HG_DK),
                            logf.reshape(bsz, l, HG_HEADS, HG_DK), kk.reshape(bsz, l, HG_HEADS, HG_DK),
                            hi.astype(jnp.float32).reshape(bsz, l, HG_HEADS, HG_DV), hg_s.astype(jnp.float32))
    o_hg = _rms(o_hg, p['hg_norm'][j]) * jax.nn.silu(hgate.astype(jnp.float32).reshape(bsz, l, HG_HEADS, HG_DV))
    mix = jnp.concatenate([o_mla.astype(h.dtype), o_hg.reshape(bsz, l, HG_VW).astype(h.dtype)], axis=-1) @ p['w_out_odd'][j]
    return mix, (c_kv.astype(h.dtype), k_rope.astype(h.dtype), hg_s.astype(h.dtype))


def _mem_kv(mem, g, w_k, w_v):
    bsz, m, _ = mem.shape
    mn = _rms(mem, g)
    return ((mn @ w_k).reshape(bsz, m, XA_HEADS, XA_HEAD_DIM), (mn @ w_v).reshape(bsz, m, XA_HEADS, XA_HEAD_DIM))


def _cross_attn(h, mem_k, mem_v, w_q, w_o):
    bsz, l, _ = h.shape
    q = (h @ w_q).reshape(bsz, l, XA_HEADS, XA_HEAD_DIM)
    s = jnp.einsum('blhd,bmhd->bhlm', q, mem_k.astype(q.dtype)).astype(jnp.float32) * XA_HEAD_DIM ** -0.5
    pr = jax.nn.softmax(s, axis=-1).astype(h.dtype)
    o = jnp.einsum('bhlm,bmhd->blhd', pr, mem_v.astype(h.dtype)).reshape(bsz, l, XA_DIM)
    return o @ w_o


def _ffn(h, w1, w2):
    return jnp.square(jax.nn.relu(h @ w1)) @ w2


def _forward(x, mem_k, mem_v, ssd_s, ssd_buf, gdn_s, gdn_buf, ckv_past, kr_past, hg_s, p):
    lbs = jnp.cumsum(jax.nn.softmax(p['hg_lower_bound'].astype(jnp.float32), axis=0), axis=0)
    lbs = lbs - lbs[0]
    n_ssd, n_ssdb, n_gdn, n_gdnb, n_ckv, n_kr, n_hg = [], [], [], [], [], [], []
    for layer in range(DEPTH):
        ng = p['norm_g'][layer]
        hn = _rms(x, ng[0])
        if layer % 2 == 0:
            e = layer // 2
            mix, (s1, b1, s2, b2) = _even_mixer(hn, p, e, ssd_s[e], ssd_buf[e], gdn_s[e], gdn_buf[e])
            n_ssd.append(s1); n_ssdb.append(b1); n_gdn.append(s2); n_gdnb.append(b2)
        else:
            j = layer // 2
            mix, (c1, c2, s3) = _odd_mixer(hn, p, j, lbs[layer], ckv_past[j], kr_past[j], hg_s[j])
            n_ckv.append(c1); n_kr.append(c2); n_hg.append(s3)
        x = x + _rms(mix, ng[1])
        xa = _cross_attn(_rms(x, ng[2]), mem_k[layer], mem_v[layer], p['xa_w_q'][layer], p['xa_w_o'][layer])
        x = x + _rms(xa, ng[3])
        x = x + _rms(_ffn(_rms(x, ng[4]), p['ffn_w1'][layer], p['ffn_w2'][layer]), ng[5])
    return (x, jnp.stack(n_ssd), jnp.stack(n_ssdb), jnp.stack(n_gdn), jnp.stack(n_gdnb),
            jnp.stack(n_ckv), jnp.stack(n_kr), jnp.stack(n_hg))


def setup_inputs(seed: int = 0) -> dict:
    key = jax.random.key(seed)
    ks = jax.random.split(key, 48)

    def nrm(i, shape, scale):
        return jax.random.normal(ks[i], shape, jnp.float32) * scale

    def gain(i, shape):
        return 1.0 + nrm(i, shape, 0.02)

    def dt_bias(i, shape):
        dt0 = jnp.exp(jax.random.uniform(ks[i], shape, jnp.float32, math.log(1e-3), math.log(1e-1)))
        return dt0 + jnp.log(-jnp.expm1(-dt0))

    def a_log(i, shape):
        return jnp.log(jax.random.uniform(ks[i], shape, jnp.float32, 1.0, 16.0))

    return {
        'x_prompt': nrm(0, (BATCH, SEQ, D_MODEL), 1.0),
        'x_sample': nrm(1, (DEC_BATCH, DEC_SEQ, D_MODEL), 1.0),
        'mem_prompt': nrm(2, (BATCH, MEM_LEN, D_MODEL), 1.0),
        'state_ssd': nrm(3, (N_EVEN, DEC_BATCH, SSD_HEADS, SSD_HEAD_DIM, SSD_STATE), 0.1),
        'state_ssd_conv': nrm(4, (N_EVEN, DEC_BATCH, CONV_W - 1, SSD_CONV_DIM), 1.0),
        'state_gdn': nrm(5, (N_EVEN, DEC_BATCH, GDN_HEADS, GDN_DK, GDN_DV), 0.1),
        'state_gdn_conv': nrm(6, (N_EVEN, DEC_BATCH, CONV_W - 1, GDN_CONV_DIM), 1.0),
        'cache_mla_ckv': nrm(7, (N_ODD, DEC_BATCH, PAST_LEN, MLA_KV_RANK), 1.0),
        'cache_mla_krope': nrm(8, (N_ODD, DEC_BATCH, PAST_LEN, MLA_ROPE), 1.0),
        'state_hgrn': nrm(9, (N_ODD, DEC_BATCH, HG_HEADS, HG_DK, HG_DV), 0.1),
        'cache_mem_k': nrm(10, (DEPTH, DEC_BATCH, MEM_LEN, XA_HEADS, XA_HEAD_DIM), 1.0),
        'cache_mem_v': nrm(11, (DEPTH, DEC_BATCH, MEM_LEN, XA_HEADS, XA_HEAD_DIM), 1.0),
        'norm_g': gain(12, (DEPTH, 6, D_MODEL)),
        'w_in_even': nrm(13, (N_EVEN, D_MODEL, EVEN_IN), D_MODEL ** -0.5),
        'w_out_even': nrm(14, (N_EVEN, EVEN_MIX, D_MODEL), EVEN_MIX ** -0.5),
        'ssd_conv_w': nrm(15, (N_EVEN, CONV_W, SSD_CONV_DIM), CONV_W ** -0.5),
        'ssd_conv_b': nrm(16, (N_EVEN, SSD_CONV_DIM), 0.02),
        'ssd_dt_bias': dt_bias(17, (N_EVEN, SSD_HEADS)),
        'ssd_a_log': a_log(18, (N_EVEN, SSD_HEADS)),
        'ssd_d': gain(19, (N_EVEN, SSD_HEADS)),
        'ssd_norm': gain(20, (N_EVEN, SSD_INNER)),
        'gdn_conv_w': nrm(21, (N_EVEN, CONV_W, GDN_CONV_DIM), CONV_W ** -0.5),
        'gdn_a_log': a_log(22, (N_EVEN, GDN_HEADS)),
        'gdn_dt_bias': dt_bias(23, (N_EVEN, GDN_HEADS)),
        'gdn_norm': gain(24, (N_EVEN, GDN_DV)),
        'w_in_odd': nrm(25, (N_ODD, D_MODEL, ODD_IN), D_MODEL ** -0.5),
        'w_out_odd': nrm(26, (N_ODD, ODD_MIX, D_MODEL), ODD_MIX ** -0.5),
        'mla_q_norm': gain(27, (N_ODD, MLA_Q_RANK)),
        'mla_w_uq': nrm(28, (N_ODD, MLA_Q_RANK, MLA_HEADS * (MLA_NOPE + MLA_ROPE)), MLA_Q_RANK ** -0.5),
        'mla_kv_norm': gain(29, (N_ODD, MLA_KV_RANK)),
        'mla_w_uk': nrm(30, (N_ODD, MLA_KV_RANK, MLA_HEADS, MLA_NOPE), MLA_KV_RANK ** -0.5),
        'mla_w_uv': nrm(31, (N_ODD, MLA_KV_RANK, MLA_HEADS, MLA_V), MLA_KV_RANK ** -0.5),
        'hg_lower_bound': nrm(32, (DEPTH, HG_K), 1.0),
        'hg_norm': gain(33, (N_ODD, HG_DV)),
        'xa_mem_norm': gain(34, (DEPTH, D_MODEL)),
        'xa_w_q': nrm(35, (DEPTH, D_MODEL, XA_DIM), D_MODEL ** -0.5),
        'xa_w_k': nrm(36, (DEPTH, D_MODEL, XA_DIM), D_MODEL ** -0.5),
        'xa_w_v': nrm(37, (DEPTH, D_MODEL, XA_DIM), D_MODEL ** -0.5),
        'xa_w_o': nrm(38, (DEPTH, XA_DIM, D_MODEL), XA_DIM ** -0.5),
        'ffn_w1': nrm(39, (DEPTH, D_MODEL, D_FF), D_MODEL ** -0.5),
        'ffn_w2': nrm(40, (DEPTH, D_FF, D_MODEL), D_FF ** -0.5),
    }


def reference(x_prompt, x_sample, mem_prompt, state_ssd, state_ssd_conv, state_gdn, state_gdn_conv,
              cache_mla_ckv, cache_mla_krope, state_hgrn, cache_mem_k, cache_mem_v,
              norm_g, w_in_even, w_out_even, ssd_conv_w, ssd_conv_b, ssd_dt_bias, ssd_a_log, ssd_d, ssd_norm,
              gdn_conv_w, gdn_a_log, gdn_dt_bias, gdn_norm,
              w_in_odd, w_out_odd, mla_q_norm, mla_w_uq, mla_kv_norm, mla_w_uk, mla_w_uv, hg_lower_bound, hg_norm,
              xa_mem_norm, xa_w_q, xa_w_k, xa_w_v, xa_w_o, ffn_w1, ffn_w2):
    p = dict(norm_g=norm_g, w_in_even=w_in_even, w_out_even=w_out_even, ssd_conv_w=ssd_conv_w, ssd_conv_b=ssd_conv_b,
             ssd_dt_bias=ssd_dt_bias, ssd_a_log=ssd_a_log, ssd_d=ssd_d, ssd_norm=ssd_norm, gdn_conv_w=gdn_conv_w,
             gdn_a_log=gdn_a_log, gdn_dt_bias=gdn_dt_bias, gdn_norm=gdn_norm, w_in_odd=w_in_odd, w_out_odd=w_out_odd,
             mla_q_norm=mla_q_norm, mla_w_uq=mla_w_uq, mla_kv_norm=mla_kv_norm, mla_w_uk=mla_w_uk, mla_w_uv=mla_w_uv,
             hg_lower_bound=hg_lower_bound, hg_norm=hg_norm, xa_w_q=xa_w_q, xa_w_o=xa_w_o, ffn_w1=ffn_w1, ffn_w2=ffn_w2)
    mk, mv = [], []
    for layer in range(DEPTH):
        k_l, v_l = _mem_kv(mem_prompt, xa_mem_norm[layer], xa_w_k[layer], xa_w_v[layer])
        mk.append(k_l); mv.append(v_l)
    p_mem_k = jnp.stack(mk)
    p_mem_v = jnp.stack(mv)
    b = x_prompt.shape[0]
    dt = x_prompt.dtype
    y_prompt, p_ssd, p_ssd_conv, p_gdn, p_gdn_conv, p_mla_ckv, p_mla_krope, p_hgrn = _forward(
        x_prompt, p_mem_k, p_mem_v,
        jnp.zeros((N_EVEN, b, SSD_HEADS, SSD_HEAD_DIM, SSD_STATE), dt),
        jnp.zeros((N_EVEN, b, CONV_W - 1, SSD_CONV_DIM), dt),
        jnp.zeros((N_EVEN, b, GDN_HEADS, GDN_DK, GDN_DV), dt),
        jnp.zeros((N_EVEN, b, CONV_W - 1, GDN_CONV_DIM), dt),
        jnp.zeros((N_ODD, b, 0, MLA_KV_RANK), dt),
        jnp.zeros((N_ODD, b, 0, MLA_ROPE), dt),
        jnp.zeros((N_ODD, b, HG_HEADS, HG_DK, HG_DV), dt),
        p)
    y_sample, s_ssd, s_ssd_conv, s_gdn, s_gdn_conv, s_mla_ckv, s_mla_krope, s_hgrn = _forward(
        x_sample, cache_mem_k, cache_mem_v, state_ssd, state_ssd_conv, state_gdn, state_gdn_conv,
        cache_mla_ckv, cache_mla_krope, state_hgrn, p)
    return (y_prompt, y_sample, p_ssd, p_ssd_conv, p_gdn, p_gdn_conv, p_mla_ckv, p_mla_krope, p_hgrn, p_mem_k, p_mem_v,
            s_ssd, s_ssd_conv, s_gdn, s_gdn_conv, s_mla_ckv, s_mla_krope, s_hgrn)
```

```python
import functools

import numpy as np
import jax
import jax.numpy as jnp
from jax import lax
from jax.experimental import pallas as pl
from jax.experimental.pallas import tpu as pltpu

F32 = jnp.float32
BF16 = jnp.bfloat16
HIGHEST = lax.Precision.HIGHEST

VMEM_LIMIT_BYTES = 56 * 1024 * 1024
LANES = 128
NORM_EPS = 1e-6
CHUNK = 64
CHUNK_SHIFT = 6
assert 1 << CHUNK_SHIFT == CHUNK
CONV_W = 4
ROPE_THETA = 10000.0
MASKED = -1e30


def _params(*sem):
    return pltpu.CompilerParams(dimension_semantics=sem, vmem_limit_bytes=VMEM_LIMIT_BYTES)


def _tile(n, cap, mult):
    if n <= cap:
        return n
    for d in range(cap - cap % mult, 0, -mult):
        if n % d == 0:
            return d
    raise ValueError(f"no tile for {n} (cap {cap}, multiple of {mult})")


def _dot(a, b):
    return jnp.dot(a, b, preferred_element_type=F32)


def _dot_nt(a, b):
    return lax.dot_general(a, b, (((1,), (1,)), ((), ())), preferred_element_type=F32)


def _dot_tn(a, b):
    return lax.dot_general(a, b, (((0,), (0,)), ((), ())), preferred_element_type=F32)


def _dot_f32(a, b):
    return jnp.dot(a, b, preferred_element_type=F32, precision=HIGHEST)


def _dot_nt_f32(a, b):
    return lax.dot_general(a, b, (((1,), (1,)), ((), ())), preferred_element_type=F32, precision=HIGHEST)


def _softplus(x):
    return jnp.maximum(x, 0.0) + jnp.log1p(jnp.exp(-jnp.abs(x)))


def _sigmoid(x):
    return 1.0 / (1.0 + jnp.exp(-x))


def _silu(x):
    return x * _sigmoid(x)


def _pick_lane(a, idx):
    lane = lax.broadcasted_iota(jnp.int32, a.shape, 1)
    return jnp.sum(jnp.where(lane == idx, a, 0.0), axis=1, keepdims=True)


def _mm_body(*refs, n_ops, nks, act):
    a_refs, b_refs = refs[:n_ops], refs[n_ops:2 * n_ops]
    o_ref, acc_ref = refs[2 * n_ops], refs[2 * n_ops + 1]
    k = pl.program_id(2)
    nk = sum(nks)

    @pl.when(k == 0)
    def _():
        acc_ref[...] = jnp.zeros_like(acc_ref)

    off = 0
    for a_ref, b_ref, n in zip(a_refs, b_refs, nks):
        def step(a_ref=a_ref, b_ref=b_ref):
            acc_ref[...] += _dot(a_ref[...].astype(BF16), b_ref[...])
        if n_ops == 1:
            step()
        else:
            pl.when((k >= off) & (k < off + n))(step)
        off += n

    @pl.when(k == nk - 1)
    def _():
        r = acc_ref[...]
        if act == "relu2":
            r = jnp.square(jnp.maximum(r, 0.0))
        o_ref[...] = r.astype(o_ref.dtype)


def _matmul(a_list, b_list, out_dtype, act=None, tm_cap=1024, tn_cap=1024, tk_cap=1024):
    m = a_list[0].shape[0]
    n = b_list[0].shape[1]
    tm = _tile(m, tm_cap, 8)
    tn = _tile(n, tn_cap, LANES)
    tk = None
    for a in a_list:
        t = _tile(a.shape[1], tk_cap, LANES)
        tk = t if tk is None else min(tk, t)
    nks = []
    for a, b in zip(a_list, b_list):
        assert a.shape[0] == m and b.shape == (a.shape[1], n) and a.shape[1] % tk == 0
        nks.append(a.shape[1] // tk)
    in_specs = []
    off = 0
    offs = []
    for nki in nks:
        offs.append(off)
        off += nki
    for o, nki in zip(offs, nks):
        in_specs.append(pl.BlockSpec((tm, tk), lambda i, j, k, o=o, nki=nki: (i, jnp.clip(k - o, 0, nki - 1))))
    for o, nki in zip(offs, nks):
        in_specs.append(pl.BlockSpec((tk, tn), lambda i, j, k, o=o, nki=nki: (jnp.clip(k - o, 0, nki - 1), j)))
    return pl.pallas_call(
        functools.partial(_mm_body, n_ops=len(a_list), nks=tuple(nks), act=act),
        grid=(m // tm, n // tn, sum(nks)),
        in_specs=in_specs,
        out_specs=pl.BlockSpec((tm, tn), lambda i, j, k: (i, j)),
        out_shape=jax.ShapeDtypeStruct((m, n), out_dtype),
        scratch_shapes=[pltpu.VMEM((tm, tn), F32)],
        compiler_params=_params("parallel", "parallel", "arbitrary"),
        name="matmul",
    )(*a_list, *b_list)


def _head_mm_body(a_ref, b_ref, o_ref, *, scale):
    r = _dot(a_ref[...].astype(BF16), b_ref[...])
    if scale != 1.0:
        r = r * scale
    o_ref[...] = r.astype(o_ref.dtype)


def _head_split_matmul(a, w, scale, tm_cap=1024):
    t = a.shape[0]
    nh, dk, n = w.shape
    tm = _tile(t, tm_cap, 8)
    return pl.pallas_call(
        functools.partial(_head_mm_body, scale=scale),
        grid=(t // tm, nh),
        in_specs=[pl.BlockSpec((tm, dk), lambda i, h: (i, h)),
                  pl.BlockSpec((None, dk, n), lambda i, h: (h, 0, 0))],
        out_specs=pl.BlockSpec((None, tm, n), lambda i, h: (h, i, 0)),
        out_shape=jax.ShapeDtypeStruct((nh, t, n), BF16),
        compiler_params=_params("parallel", "arbitrary"),
        name="head_split_matmul",
    )(a, w)


def _head_merge_matmul(a, w, tm_cap=1024):
    nh, t, k = a.shape
    dv = w.shape[2]
    tm = _tile(t, tm_cap, 8)
    return pl.pallas_call(
        functools.partial(_head_mm_body, scale=1.0),
        grid=(t // tm, nh),
        in_specs=[pl.BlockSpec((None, tm, k), lambda i, h: (h, i, 0)),
                  pl.BlockSpec((None, k, dv), lambda i, h: (h, 0, 0))],
        out_specs=pl.BlockSpec((tm, dv), lambda i, h: (i, h)),
        out_shape=jax.ShapeDtypeStruct((t, nh * dv), BF16),
        compiler_params=_params("parallel", "arbitrary"),
        name="head_merge_matmul",
    )(a, w)


def _rms_rows(x, g):
    return x * lax.rsqrt(jnp.mean(x * x, axis=-1, keepdims=True) + NORM_EPS) * g


def _rms_body(x_ref, g_ref, o_ref):
    o_ref[...] = _rms_rows(x_ref[...], g_ref[...]).astype(o_ref.dtype)


def _rms(x, g, out_dtype, col_block=0, width=None, tr_cap=256):
    t = x.shape[0]
    width = x.shape[1] if width is None else width
    tr = _tile(t, tr_cap, 8)
    return pl.pallas_call(
        _rms_body,
        grid=(t // tr,),
        in_specs=[pl.BlockSpec((tr, width), lambda i: (i, col_block)),
                  pl.BlockSpec((1, width), lambda i: (0, 0))],
        out_specs=pl.BlockSpec((tr, width), lambda i: (i, 0)),
        out_shape=jax.ShapeDtypeStruct((t, width), out_dtype),
        compiler_params=_params("parallel"),
        name="rms",
    )(x, g.reshape(1, width))


def _add_rms_body(x_ref, y_ref, g1_ref, g2_ref, xo_ref, ho_ref):
    xn = x_ref[...] + _rms_rows(y_ref[...], g1_ref[...])
    xo_ref[...] = xn
    ho_ref[...] = _rms_rows(xn, g2_ref[...]).astype(ho_ref.dtype)


def _add_rms_last_body(x_ref, y_ref, g1_ref, xo_ref):
    xo_ref[...] = x_ref[...] + _rms_rows(y_ref[...], g1_ref[...])


def _add_rms(x, y, g1, g2, tr_cap=256):
    t, d = x.shape
    tr = _tile(t, tr_cap, 8)
    row = pl.BlockSpec((tr, d), lambda i: (i, 0))
    gain = pl.BlockSpec((1, d), lambda i: (0, 0))
    if g2 is None:
        out = pl.pallas_call(
            _add_rms_last_body, grid=(t // tr,), in_specs=[row, row, gain], out_specs=row,
            out_shape=jax.ShapeDtypeStruct((t, d), F32), compiler_params=_params("parallel"),
            name="add_rms_last",
        )(x, y, g1.reshape(1, d))
        return out, None
    return pl.pallas_call(
        _add_rms_body, grid=(t // tr,), in_specs=[row, row, gain, gain], out_specs=[row, row],
        out_shape=[jax.ShapeDtypeStruct((t, d), F32), jax.ShapeDtypeStruct((t, d), BF16)],
        compiler_params=_params("parallel"), name="add_rms",
    )(x, y, g1.reshape(1, d), g2.reshape(1, d))


CONV_HALO = 8


def _conv_body(u_ref, buf_ref, w_ref, b_ref, o_ref, ext_ref, *, tl):
    lt = pl.program_id(2)
    lo = CONV_HALO - (CONV_W - 1)

    @pl.when(lt == 0)
    def _():
        ext_ref[lo:CONV_HALO, :] = buf_ref[0]

    @pl.when(lt > 0)
    def _():
        ext_ref[lo:CONV_HALO, :] = ext_ref[tl + lo:tl + CONV_HALO, :]

    ext_ref[CONV_HALO:CONV_HALO + tl, :] = u_ref[...]
    w = w_ref[...]
    acc = b_ref[...] + ext_ref[lo:lo + tl, :] * w[0:1, :]
    for j in range(1, CONV_W):
        acc = acc + ext_ref[lo + j:lo + j + tl, :] * w[j:j + 1, :]
    o_ref[...] = _silu(acc)


def _conv_silu(u, col0, c, buf, w, b, bsz, l, tl_cap=512, tc=1024):
    assert l >= CONV_W - 1 and col0 % tc == 0 and c % tc == 0
    tl = _tile(l, tl_cap, 8)
    nl = l // tl
    cb0 = col0 // tc
    return pl.pallas_call(
        functools.partial(_conv_body, tl=tl),
        grid=(bsz, c // tc, nl),
        in_specs=[pl.BlockSpec((tl, tc), lambda bi, ci, li: (bi * nl + li, cb0 + ci)),
                  pl.BlockSpec((1, CONV_W - 1, tc), lambda bi, ci, li: (bi, 0, ci)),
                  pl.BlockSpec((CONV_W, tc), lambda bi, ci, li: (0, ci)),
                  pl.BlockSpec((1, tc), lambda bi, ci, li: (0, ci))],
        out_specs=pl.BlockSpec((tl, tc), lambda bi, ci, li: (bi * nl + li, ci)),
        out_shape=jax.ShapeDtypeStruct((bsz * l, c), F32),
        scratch_shapes=[pltpu.VMEM((CONV_HALO + tl, tc), F32)],
        compiler_params=_params("parallel", "parallel", "arbitrary"),
        name="conv_silu",
    )(u, buf, w, b.reshape(1, c))


def _tri_masks(n):
    r = np.arange(n)
    return (jnp.asarray((r[:, None] >= r[None, :]).astype(np.float32)),
            jnp.asarray(np.eye(n, dtype=np.float32)))


def _ssd_body(x_ref, bm_ref, cm_ref, z_ref, sm_ref, dtb_ref, alog_ref, d_ref, gn_ref, s0_ref, tril_ref, eye_ref,
              y_ref, so_ref, s_scr, cst_scr, *, nc, hpg, hd, nheads):
    g = pl.program_id(1)
    c = pl.program_id(2)
    lc = x_ref.shape[0]

    @pl.when(c == 0)
    def _():
        s_scr[...] = s0_ref[0]

    dt = _softplus(sm_ref[:, 0:nheads] + dtb_ref[...])
    da = dt * (-jnp.exp(alog_ref[...]))
    cs = _dot_f32(tril_ref[...], da)
    cst_scr[...] = _dot_nt_f32(eye_ref[...], cs)
    tri = tril_ref[...] > 0.0

    x = x_ref[...]
    bm = bm_ref[...].astype(BF16)
    cm = cm_ref[...].astype(BF16)
    s = s_scr[...]
    cb = _dot_nt(cm, bm)
    y_state = _dot_nt(cm, s.astype(BF16))
    ys, xws, decs = [], [], []
    for j in range(hpg):
        h = g * hpg + j
        cs_col = _pick_lane(cs, h)
        dt_col = _pick_lane(dt, h)
        d_h = _pick_lane(d_ref[...], h)
        cs_row = cst_scr[pl.ds(h, 1), :]
        cs_last = cs_row[:, lc - 1:lc]
        decay = jnp.exp(jnp.where(tri, cs_col - cs_row, MASKED))
        xh = x[:, j * hd:(j + 1) * hd]
        xdt = xh * dt_col
        yh = _dot((cb * decay).astype(BF16), xdt.astype(BF16))
        yh = yh + y_state[:, j * hd:(j + 1) * hd] * jnp.exp(cs_col) + d_h * xh
        ys.append(yh)
        xws.append(xdt * jnp.exp(cs_last - cs_col))
        decs.append(jnp.broadcast_to(jnp.exp(cs_last), (hd, s.shape[1])))
    y = jnp.concatenate(ys, axis=1)
    xw = jnp.concatenate(xws, axis=1)
    s_new = s * jnp.concatenate(decs, axis=0) + _dot_tn(xw.astype(BF16), bm)
    s_scr[...] = s_new

    @pl.when(c == nc - 1)
    def _():
        so_ref[0] = s_new

    y = y * _silu(z_ref[...])
    y_ref[...] = _rms_rows(y, gn_ref[...]).astype(y_ref.dtype)


def _ssd(xbc, pm, z_col0, sm, dt_bias, a_log, d, gn, s0, bsz, l, nheads, hd, ngroups, nstate):
    t = bsz * l
    lc = CHUNK
    nc = l // lc
    hpg = nheads // ngroups
    gw = hpg * hd
    inner = nheads * hd
    assert l % lc == 0 and nheads == lc and z_col0 % gw == 0
    tril, eye = _tri_masks(lc)
    rows = lambda bi, gi, ci: bi * nc + ci
    const2 = lambda bi, gi, ci: (0, 0)
    y, s_new = pl.pallas_call(
        functools.partial(_ssd_body, nc=nc, hpg=hpg, hd=hd, nheads=nheads),
        grid=(bsz, ngroups, nc),
        in_specs=[pl.BlockSpec((lc, gw), lambda bi, gi, ci: (rows(bi, gi, ci), gi)),
                  pl.BlockSpec((lc, nstate), lambda bi, gi, ci: (rows(bi, gi, ci), inner // nstate + gi)),
                  pl.BlockSpec((lc, nstate), lambda bi, gi, ci: (rows(bi, gi, ci), inner // nstate + ngroups + gi)),
                  pl.BlockSpec((lc, gw), lambda bi, gi, ci: (rows(bi, gi, ci), z_col0 // gw + gi)),
                  pl.BlockSpec((lc, sm.shape[1]), lambda bi, gi, ci: (rows(bi, gi, ci), 0)),
                  pl.BlockSpec((1, nheads), const2),
                  pl.BlockSpec((1, nheads), const2),
                  pl.BlockSpec((1, nheads), const2),
                  pl.BlockSpec((1, gw), lambda bi, gi, ci: (0, gi)),
                  pl.BlockSpec((1, gw, nstate), lambda bi, gi, ci: (bi, gi, 0)),
                  pl.BlockSpec((lc, lc), const2),
                  pl.BlockSpec((lc, lc), const2)],
        out_specs=[pl.BlockSpec((lc, gw), lambda bi, gi, ci: (rows(bi, gi, ci), gi)),
                   pl.BlockSpec((1, gw, nstate), lambda bi, gi, ci: (bi, gi, 0))],
        out_shape=[jax.ShapeDtypeStruct((t, inner), BF16),
                   jax.ShapeDtypeStruct((bsz, inner, nstate), F32)],
        scratch_shapes=[pltpu.VMEM((gw, nstate), F32), pltpu.VMEM((nheads, lc), F32)],
        compiler_params=_params("parallel", "parallel", "arbitrary"),
        name="ssd_scan",
    )(xbc, xbc, xbc, pm, sm, dt_bias.reshape(1, nheads), a_log.reshape(1, nheads), d.reshape(1, nheads),
      gn.reshape(1, inner), s0, tril, eye)
    return y, s_new


def _gdn_body(q_ref, k_ref, v_ref, gate_ref, sm_ref, alog_ref, dtb_ref, gn_ref, s0_ref, tril_ref, eye_ref, eyeh_ref,
              o_ref, so_ref, s_scr, cst_scr, *, nc, nheads, a_col0):
    h = pl.program_id(1)
    c = pl.program_id(2)
    lc, dk = q_ref.shape

    @pl.when(c == 0)
    def _():
        s_scr[...] = s0_ref[0, 0]

    a_raw = sm_ref[:, a_col0:a_col0 + nheads]
    b_raw = sm_ref[:, a_col0 + nheads:a_col0 + 2 * nheads]
    gl = -jnp.exp(alog_ref[...]) * _softplus(a_raw + dtb_ref[...])
    beta = _pick_lane(_sigmoid(b_raw), h)
    cs = _dot_f32(tril_ref[...], gl)
    cst_scr[...] = _dot_nt_f32(eyeh_ref[...], cs)
    cs_col = _pick_lane(cs, h)
    cs_row = cst_scr[pl.ds(h, 1), :]
    cs_last = cs_row[:, lc - 1:lc]
    tri = tril_ref[...] > 0.0
    eye = eye_ref[...]
    decay = jnp.exp(jnp.where(tri, cs_col - cs_row, MASKED))

    q = q_ref[...]
    k = k_ref[...]
    v = v_ref[...]
    qn = q * (lax.rsqrt(jnp.sum(q * q, axis=1, keepdims=True) + 1e-6) * dk ** -0.5)
    kn = k * lax.rsqrt(jnp.sum(k * k, axis=1, keepdims=True) + 1e-6)
    kb = kn * beta
    kn16 = kn.astype(BF16)
    a = _dot_nt(kb.astype(BF16), kn16) * decay * (1.0 - eye)
    tinv = eye - a
    p = _dot_f32(a, a)
    steps = max(1, int(np.ceil(np.log2(lc))) - 1)
    for i in range(steps):
        tinv = tinv + _dot_f32(tinv, p)
        if i + 1 < steps:
            p = _dot_f32(p, p)
    rhs = jnp.concatenate([v * beta, kb * jnp.exp(cs_col)], axis=1)
    sol = _dot(tinv.astype(BF16), rhs.astype(BF16))
    dv = v.shape[1]
    u, w = sol[:, :dv], sol[:, dv:]
    s = s_scr[...]
    s16 = s.astype(BF16)
    v_new = u - _dot(w.astype(BF16), s16)
    vn16 = v_new.astype(BF16)
    qk = _dot_nt(qn.astype(BF16), kn16) * decay
    o = _dot((qn * jnp.exp(cs_col)).astype(BF16), s16) + _dot(qk.astype(BF16), vn16)
    k_end = kn * jnp.exp(cs_last - cs_col)
    s_new = s * jnp.exp(cs_last) + _dot_tn(k_end.astype(BF16), vn16)
    s_scr[...] = s_new

    @pl.when(c == nc - 1)
    def _():
        so_ref[0, 0] = s_new

    o_ref[...] = (_rms_rows(o, gn_ref[...]) * _silu(gate_ref[...])).astype(o_ref.dtype)


def _gdn(qkv, pm, gate_col0, sm, a_col0, a_log, dt_bias, gn, s0, bsz, l, nheads, dk, dv):
    assert dk == dv and gate_col0 % dv == 0
    t = bsz * l
    lc = CHUNK
    nc = l // lc
    tril, eye = _tri_masks(lc)
    eyeh = jnp.asarray(np.eye(nheads, dtype=np.float32))
    rows = lambda bi, hi, ci: bi * nc + ci
    const2 = lambda bi, hi, ci: (0, 0)
    o, s_new = pl.pallas_call(
        functools.partial(_gdn_body, nc=nc, nheads=nheads, a_col0=a_col0),
        grid=(bsz, nheads, nc),
        in_specs=[pl.BlockSpec((lc, dk), lambda bi, hi, ci: (rows(bi, hi, ci), hi)),
                  pl.BlockSpec((lc, dk), lambda bi, hi, ci: (rows(bi, hi, ci), nheads + hi)),
                  pl.BlockSpec((lc, dv), lambda bi, hi, ci: (rows(bi, hi, ci), 2 * nheads + hi)),
                  pl.BlockSpec((lc, dv), lambda bi, hi, ci: (rows(bi, hi, ci), gate_col0 // dv + hi)),
                  pl.BlockSpec((lc, sm.shape[1]), lambda bi, hi, ci: (rows(bi, hi, ci), 0)),
                  pl.BlockSpec((1, nheads), const2),
                  pl.BlockSpec((1, nheads), const2),
                  pl.BlockSpec((1, dv), const2),
                  pl.BlockSpec((1, 1, dk, dv), lambda bi, hi, ci: (bi, hi, 0, 0)),
                  pl.BlockSpec((lc, lc), const2),
                  pl.BlockSpec((lc, lc), const2),
                  pl.BlockSpec((nheads, nheads), const2)],
        out_specs=[pl.BlockSpec((lc, dv), lambda bi, hi, ci: (rows(bi, hi, ci), hi)),
                   pl.BlockSpec((1, 1, dk, dv), lambda bi, hi, ci: (bi, hi, 0, 0))],
        out_shape=[jax.ShapeDtypeStruct((t, nheads * dv), BF16),
                   jax.ShapeDtypeStruct((bsz, nheads, dk, dv), F32)],
        scratch_shapes=[pltpu.VMEM((dk, dv), F32), pltpu.VMEM((nheads, lc), F32)],
        compiler_params=_params("parallel", "parallel", "arbitrary"),
        name="gdn_scan",
    )(qkv, qkv, qkv, pm, sm, a_log.reshape(1, nheads), dt_bias.reshape(1, nheads), gn.reshape(1, dv), s0,
      tril, eye, eyeh)
    return o, s_new


def _hgrn_levels(lc):
    hs = []
    h = lc // 2
    while h >= 1:
        hs.append(h)
        h //= 2
    return hs


def _hgrn_masks(lc):
    r = np.arange(lc)
    t, u = r[:, None], r[None, :]
    mats = [u <= t, u > t]
    for h in _hgrn_levels(lc):
        start = (t // h) * h
        mats.append((u >= start) & (u <= t))
        mats.append((u > t) & (u <= start + h - 1))
    return jnp.asarray(np.concatenate(mats, axis=0).astype(np.float32))


def _hgrn_body(q_ref, f_ref, i_ref, gate_ref, lb_ref, gn_ref, s0_ref, mst_ref, o_ref, so_ref, st_scr, *, nc):
    c = pl.program_id(2)
    lc = q_ref.shape[0]

    @pl.when(c == 0)
    def _():
        st_scr[...] = s0_ref[0, 0].T

    lb = lb_ref[...]
    hf = f_ref[...]
    logf = jnp.log(lb + (1.0 - lb) * _sigmoid(hf))
    kk = (1.0 - lb) * _sigmoid(-hf)
    cums = _dot_f32(mst_ref[...], logf)
    b_full = cums[0:lc]
    after = cums[lc:2 * lc]
    q = q_ref[...]
    v16 = i_ref[...].astype(BF16)

    row = lax.broadcasted_iota(jnp.int32, (lc, 1), 0)
    ti = lax.broadcasted_iota(jnp.int32, (lc, lc), 0)
    si = lax.broadcasted_iota(jnp.int32, (lc, lc), 1)
    att = jnp.where(ti == si, jnp.sum(q * kk, axis=1, keepdims=True), 0.0)
    for li, hs in enumerate(_hgrn_levels(lc)):
        sh = int(np.log2(hs))
        seg = cums[(2 + 2 * li) * lc:(3 + 2 * li) * lc]
        rest = cums[(3 + 2 * li) * lc:(4 + 2 * li) * lc]
        upper = ((row >> sh) & 1) == 1
        ql = jnp.where(upper, q * jnp.exp(seg), 0.0)
        kl = jnp.where(upper, 0.0, kk * jnp.exp(rest))
        al = _dot_nt(ql.astype(BF16), kl.astype(BF16))
        att = att + jnp.where((ti >> (sh + 1)) == (si >> (sh + 1)), al, 0.0)

    st = st_scr[...]
    o = _dot_nt((q * jnp.exp(b_full)).astype(BF16), st.astype(BF16)) + _dot(att.astype(BF16), v16)
    k_end = kk * jnp.exp(after)
    st_new = st * jnp.exp(b_full[lc - 1:lc, :]) + _dot_tn(v16, k_end.astype(BF16))
    st_scr[...] = st_new

    @pl.when(c == nc - 1)
    def _():
        so_ref[0, 0] = st_new.T

    o_ref[...] = (_rms_rows(o, gn_ref[...]) * _silu(gate_ref[...])).astype(o_ref.dtype)


def _hgrn(pm, lb, gn, s0, bsz, l, nheads, dk):
    t = bsz * l
    lc = CHUNK
    nc = l // lc
    mst = _hgrn_masks(lc)
    rows = lambda bi, hi, ci: bi * nc + ci
    const2 = lambda bi, hi, ci: (0, 0)
    col = lambda k: pl.BlockSpec((lc, dk), lambda bi, hi, ci: (rows(bi, hi, ci), k * nheads + hi))
    o, s_new = pl.pallas_call(
        functools.partial(_hgrn_body, nc=nc),
        grid=(bsz, nheads, nc),
        in_specs=[col(0), col(1), col(2), col(3),
                  pl.BlockSpec((1, dk), lambda bi, hi, ci: (0, hi)),
                  pl.BlockSpec((1, dk), const2),
                  pl.BlockSpec((1, 1, dk, dk), lambda bi, hi, ci: (bi, hi, 0, 0)),
                  pl.BlockSpec(mst.shape, const2)],
        out_specs=[pl.BlockSpec((lc, dk), lambda bi, hi, ci: (rows(bi, hi, ci), hi)),
                   pl.BlockSpec((1, 1, dk, dk), lambda bi, hi, ci: (bi, hi, 0, 0))],
        out_shape=[jax.ShapeDtypeStruct((t, nheads * dk), BF16),
                   jax.ShapeDtypeStruct((bsz, nheads, dk, dk), F32)],
        scratch_shapes=[pltpu.VMEM((dk, dk), F32)],
        compiler_params=_params("parallel", "parallel", "arbitrary"),
        name="hgrn_scan",
    )(pm, pm, pm, pm, lb.reshape(1, nheads * dk), gn.reshape(1, dk), s0, mst)
    return o, s_new


def _mla_body(ql_ref, qr_ref, ck_ref, kr_ref, o_ref, m_scr, l_scr, acc_scr, *, tq, tk, t_past, nkv):
    iq = pl.program_id(1)
    ikv = pl.program_id(2)
    nh, _, r = ql_ref.shape
    rows = nh * tq
    first_limit = t_past + iq * tq + CHUNK
    last_limit = t_past + (iq + 1) * tq

    @pl.when(ikv == 0)
    def _():
        m_scr[...] = jnp.full_like(m_scr, MASKED)
        l_scr[...] = jnp.zeros_like(l_scr)
        acc_scr[...] = jnp.zeros_like(acc_scr)

    def update(masked):
        ql = ql_ref[...].reshape(rows, r)
        qr = qr_ref[...].reshape(rows, qr_ref.shape[2])
        ck = ck_ref[0]
        s = _dot_nt(ql, ck) + _dot_nt(qr, kr_ref[0])
        if masked:
            tok = lax.broadcasted_iota(jnp.int32, (rows, 1), 0) & (tq - 1)
            limit = t_past + iq * tq + ((tok >> CHUNK_SHIFT) + 1) * CHUNK
            key = ikv * tk + lax.broadcasted_iota(jnp.int32, (1, tk), 1)
            s = jnp.where(key < limit, s, MASKED)
        m_old = m_scr[...]
        m_new = jnp.maximum(m_old, jnp.max(s, axis=1, keepdims=True))
        alpha = jnp.exp(m_old - m_new)
        p = jnp.exp(s - m_new)
        l_scr[...] = alpha * l_scr[...] + jnp.sum(p, axis=1, keepdims=True)
        acc_scr[...] = alpha * acc_scr[...] + _dot(p.astype(BF16), ck)
        m_scr[...] = m_new

    block_end = (ikv + 1) * tk
    pl.when(block_end <= first_limit)(lambda: update(False))
    pl.when((block_end > first_limit) & (ikv * tk < last_limit))(lambda: update(True))

    @pl.when(ikv == nkv - 1)
    def _():
        o = acc_scr[...] / l_scr[...]
        o_ref[...] = o.reshape(nh, tq, r).astype(o_ref.dtype)


def _mla_attention(q_lat, q_rope, ckv_all, kr_all, bsz, l, t_past, tq, tk):
    nh, t, r = q_lat.shape
    p = q_rope.shape[2]
    tkeys = ckv_all.shape[1]
    assert l % tq == 0 and tq % CHUNK == 0 and t_past % CHUNK == 0 and tkeys % tk == 0 and tkeys >= t_past + l
    assert tq & (tq - 1) == 0
    nq = l // tq
    nkv = tkeys // tk

    def kv_index(bi, qi, ki):
        need = (t_past + (qi + 1) * tq + tk - 1) // tk
        return (bi, jnp.minimum(ki, need - 1), 0)

    return pl.pallas_call(
        functools.partial(_mla_body, tq=tq, tk=tk, t_past=t_past, nkv=nkv),
        grid=(bsz, nq, nkv),
        in_specs=[pl.BlockSpec((nh, tq, r), lambda bi, qi, ki: (0, bi * nq + qi, 0)),
                  pl.BlockSpec((nh, tq, p), lambda bi, qi, ki: (0, bi * nq + qi, 0)),
                  pl.BlockSpec((1, tk, r), kv_index),
                  pl.BlockSpec((1, tk, p), kv_index)],
        out_specs=pl.BlockSpec((nh, tq, r), lambda bi, qi, ki: (0, bi * nq + qi, 0)),
        out_shape=jax.ShapeDtypeStruct((nh, t, r), BF16),
        scratch_shapes=[pltpu.VMEM((nh * tq, 1), F32), pltpu.VMEM((nh * tq, 1), F32),
                        pltpu.VMEM((nh * tq, r), F32)],
        compiler_params=_params("parallel", "parallel", "arbitrary"),
        name="mla_attention",
    )(q_lat, q_rope, ckv_all, kr_all)


def _xattn_body(q_ref, k_ref, v_ref, o_ref, *, nheads, hd):
    q = q_ref[...]
    k = k_ref[0]
    v = v_ref[0]
    outs = []
    for h in range(nheads):
        sl = slice(h * hd, (h + 1) * hd)
        s = _dot_nt(q[:, sl], k[:, sl]) * hd ** -0.5
        s = s - jnp.max(s, axis=1, keepdims=True)
        p = jnp.exp(s)
        p = p / jnp.sum(p, axis=1, keepdims=True)
        outs.append(_dot(p.astype(BF16), v[:, sl]))
    o_ref[...] = jnp.concatenate(outs, axis=1).astype(o_ref.dtype)


def _cross_attention(q, mem_k, mem_v, bsz, l, nheads, hd, tl_cap=512):
    t, d = q.shape
    m = mem_k.shape[1]
    tl = _tile(l, tl_cap, 8)
    nl = l // tl
    return pl.pallas_call(
        functools.partial(_xattn_body, nheads=nheads, hd=hd),
        grid=(bsz, nl),
        in_specs=[pl.BlockSpec((tl, d), lambda bi, li: (bi * nl + li, 0)),
                  pl.BlockSpec((1, m, d), lambda bi, li: (bi, 0, 0)),
                  pl.BlockSpec((1, m, d), lambda bi, li: (bi, 0, 0))],
        out_specs=pl.BlockSpec((tl, d), lambda bi, li: (bi * nl + li, 0)),
        out_shape=jax.ShapeDtypeStruct((t, d), BF16),
        compiler_params=_params("parallel", "parallel"),
        name="cross_attention",
    )(q, mem_k, mem_v)


SSD_HEADS, SSD_HEAD_DIM, SSD_GROUPS, SSD_STATE = 64, 64, 8, 128
SSD_INNER = SSD_HEADS * SSD_HEAD_DIM
SSD_CONV_DIM = SSD_INNER + 2 * SSD_GROUPS * SSD_STATE
GDN_HEADS, GDN_DK, GDN_DV = 32, 128, 128
GDN_CONV_DIM = 2 * GDN_HEADS * GDN_DK + GDN_HEADS * GDN_DV
MLA_HEADS, MLA_Q_RANK, MLA_KV_RANK, MLA_NOPE, MLA_ROPE, MLA_V = 32, 1024, 512, 128, 64, 128
HG_HEADS, HG_DK = 32, 128
HG_W = HG_HEADS * HG_DK
XA_HEADS, XA_HEAD_DIM = 4, 128
MLA_TK = 512


def _rope(x, pos):
    half = x.shape[-1] // 2
    inv = ROPE_THETA ** (-jnp.arange(half, dtype=F32) / half)
    ang = pos[:, None] * inv[None, :]
    ang = ang.reshape((ang.shape[0],) + (1,) * (x.ndim - 3) + (half,))
    cos, sin = jnp.cos(ang), jnp.sin(ang)
    x1, x2 = x[..., :half], x[..., half:]
    return jnp.concatenate([x1 * cos - x2 * sin, x2 * cos + x1 * sin], axis=-1)


def _prep_weights(w):
    out = {}
    e_sizes = [SSD_INNER, SSD_CONV_DIM, SSD_HEADS, GDN_CONV_DIM, GDN_HEADS, GDN_HEADS, GDN_HEADS * GDN_DV]
    e_off = np.concatenate([[0], np.cumsum(e_sizes)])
    wi = w["w_in_even"]
    sec = lambda a, offs, i: a[:, :, offs[i]:offs[i + 1]]
    out["w_in_even_main"] = jnp.concatenate(
        [sec(wi, e_off, 0), sec(wi, e_off, 1), sec(wi, e_off, 3), sec(wi, e_off, 6)], axis=-1).astype(BF16)
    out["w_in_even_small"] = jnp.concatenate(
        [sec(wi, e_off, 2), sec(wi, e_off, 4), sec(wi, e_off, 5)], axis=-1).astype(BF16)
    o_sizes = [MLA_Q_RANK, MLA_KV_RANK, MLA_ROPE, HG_W, HG_W, HG_W, HG_W]
    o_off = np.concatenate([[0], np.cumsum(o_sizes)])
    wo = w["w_in_odd"]
    out["w_in_odd_main"] = jnp.concatenate(
        [sec(wo, o_off, 3), sec(wo, o_off, 4), sec(wo, o_off, 5), sec(wo, o_off, 6), sec(wo, o_off, 0)],
        axis=-1).astype(BF16)
    out["w_in_odd_small"] = jnp.concatenate([sec(wo, o_off, 1), sec(wo, o_off, 2)], axis=-1).astype(BF16)
    n_odd = wo.shape[0]
    uq = w["mla_w_uq"].reshape(n_odd, MLA_Q_RANK, MLA_HEADS, MLA_NOPE + MLA_ROPE)
    out["w_uq"] = jnp.concatenate(
        [uq[..., :MLA_NOPE].reshape(n_odd, MLA_Q_RANK, MLA_HEADS * MLA_NOPE),
         uq[..., MLA_NOPE:].reshape(n_odd, MLA_Q_RANK, MLA_HEADS * MLA_ROPE)], axis=-1).astype(BF16)
    out["w_uk"] = jnp.transpose(w["mla_w_uk"], (0, 2, 3, 1)).astype(BF16)
    out["w_uv"] = jnp.transpose(w["mla_w_uv"], (0, 2, 1, 3)).astype(BF16)
    for name in ("w_out_even", "w_out_odd", "xa_w_q", "xa_w_k", "xa_w_v", "xa_w_o", "ffn_w1", "ffn_w2"):
        out[name] = w[name].astype(BF16)
    return out


def _even_mixer(hn, w, wb, e, bsz, l, ssd_s, ssd_buf, gdn_s, gdn_buf):
    pm = _matmul([hn], [wb["w_in_even_main"][e]], F32)
    sm = _matmul([hn], [wb["w_in_even_small"][e]], F32)
    xbc_col0, qkv_col0 = SSD_INNER, SSD_INNER + SSD_CONV_DIM
    gate_col0 = qkv_col0 + GDN_CONV_DIM
    xbc = _conv_silu(pm, xbc_col0, SSD_CONV_DIM, ssd_buf, w["ssd_conv_w"][e], w["ssd_conv_b"][e], bsz, l)
    qkv = _conv_silu(pm, qkv_col0, GDN_CONV_DIM, gdn_buf, w["gdn_conv_w"][e],
                     jnp.zeros((GDN_CONV_DIM,), F32), bsz, l)
    y, ssd_new = _ssd(xbc, pm, 0, sm, w["ssd_dt_bias"][e], w["ssd_a_log"][e], w["ssd_d"][e], w["ssd_norm"][e],
                      ssd_s.reshape(bsz, SSD_INNER, SSD_STATE), bsz, l, SSD_HEADS, SSD_HEAD_DIM, SSD_GROUPS,
                      SSD_STATE)
    o, gdn_new = _gdn(qkv, pm, gate_col0, sm, SSD_HEADS, w["gdn_a_log"][e], w["gdn_dt_bias"][e], w["gdn_norm"][e],
                      gdn_s, bsz, l, GDN_HEADS, GDN_DK, GDN_DV)
    w_out = wb["w_out_even"][e]
    mix = _matmul([y, o], [w_out[:SSD_INNER], w_out[SSD_INNER:]], F32)
    pm3 = pm.reshape(bsz, l, pm.shape[1])
    tail = pm3[:, l - (CONV_W - 1):]
    states = (ssd_new.reshape(bsz, SSD_HEADS, SSD_HEAD_DIM, SSD_STATE), tail[:, :, xbc_col0:qkv_col0],
              gdn_new, tail[:, :, qkv_col0:gate_col0])
    return mix, states


def _odd_mixer(hn, w, wb, j, lb, bsz, l, ckv_past, kr_past, hg_s):
    t_past = ckv_past.shape[1]
    pm = _matmul([hn], [wb["w_in_odd_main"][j]], F32)
    sm = _matmul([hn], [wb["w_in_odd_small"][j]], F32)
    posf = (t_past + jnp.arange(l, dtype=jnp.int32)).astype(F32)
    scale = (MLA_NOPE + MLA_ROPE) ** -0.5
    cqn = _rms(pm, w["mla_q_norm"][j], BF16, col_block=4 * HG_W // MLA_Q_RANK, width=MLA_Q_RANK)
    q = _matmul([cqn], [wb["w_uq"][j]], F32)
    q_lat = _head_split_matmul(q, wb["w_uk"][j], scale)
    nope_w = MLA_HEADS * MLA_NOPE
    q_rope = _rope(q[:, nope_w:].reshape(bsz, l, MLA_HEADS, MLA_ROPE), posf) * scale
    q_rope = jnp.transpose(q_rope.reshape(bsz * l, MLA_HEADS, MLA_ROPE), (1, 0, 2)).astype(BF16)
    c_kv = _rms(sm, w["mla_kv_norm"][j], F32, col_block=0, width=MLA_KV_RANK).reshape(bsz, l, MLA_KV_RANK)
    k_rope = _rope(sm[:, MLA_KV_RANK:].reshape(bsz, l, MLA_ROPE), posf)
    tkeys = t_past + l
    pad = (-tkeys) % MLA_TK
    ckv_all = jnp.concatenate([ckv_past.astype(BF16), c_kv.astype(BF16),
                               jnp.zeros((bsz, pad, MLA_KV_RANK), BF16)], axis=1)
    kr_all = jnp.concatenate([kr_past.astype(BF16), k_rope.astype(BF16),
                              jnp.zeros((bsz, pad, MLA_ROPE), BF16)], axis=1)
    tq = CHUNK
    o_lat = _mla_attention(q_lat, q_rope, ckv_all, kr_all, bsz, l, t_past, tq, MLA_TK)
    o_mla = _head_merge_matmul(o_lat, wb["w_uv"][j])
    o_hg, hg_new = _hgrn(pm, lb, w["hg_norm"][j], hg_s, bsz, l, HG_HEADS, HG_DK)
    w_out = wb["w_out_odd"][j]
    mix = _matmul([o_mla, o_hg], [w_out[:MLA_HEADS * MLA_V], w_out[MLA_HEADS * MLA_V:]], F32)
    return mix, (c_kv, k_rope, hg_new)


def _forward(x, mem_k, mem_v, ssd_s, ssd_buf, gdn_s, gdn_buf, ckv_past, kr_past, hg_s, w, wb, lbs):
    bsz, l, d = x.shape
    depth = w["norm_g"].shape[0]
    x = x.reshape(bsz * l, d)
    hn = _rms(x, w["norm_g"][0, 0], BF16)
    ev, od = [], []
    for layer in range(depth):
        ng = w["norm_g"][layer]
        if layer % 2 == 0:
            e = layer // 2
            mix, st = _even_mixer(hn, w, wb, e, bsz, l, ssd_s[e], ssd_buf[e], gdn_s[e], gdn_buf[e])
            ev.append(st)
        else:
            j = layer // 2
            mix, st = _odd_mixer(hn, w, wb, j, lbs[layer], bsz, l, ckv_past[j], kr_past[j], hg_s[j])
            od.append(st)
        x, hn = _add_rms(x, mix, ng[1], ng[2])
        q = _matmul([hn], [wb["xa_w_q"][layer]], BF16)
        xo = _cross_attention(q, mem_k[layer], mem_v[layer], bsz, l, XA_HEADS, XA_HEAD_DIM)
        xa = _matmul([xo], [wb["xa_w_o"][layer]], F32)
        x, hn = _add_rms(x, xa, ng[3], ng[4])
        h1 = _matmul([hn], [wb["ffn_w1"][layer]], BF16, act="relu2")
        f = _matmul([h1], [wb["ffn_w2"][layer]], F32)
        x, hn = _add_rms(x, f, ng[5], w["norm_g"][layer + 1, 0] if layer + 1 < depth else None)
    stack = lambda items, i: jnp.stack([it[i] for it in items])
    return (x.reshape(bsz, l, d), stack(ev, 0), stack(ev, 1), stack(ev, 2), stack(ev, 3),
            stack(od, 0), stack(od, 1), stack(od, 2))


def kernel(x_prompt, x_sample, mem_prompt, state_ssd, state_ssd_conv, state_gdn, state_gdn_conv, cache_mla_ckv, cache_mla_krope, state_hgrn, cache_mem_k, cache_mem_v, norm_g, w_in_even, w_out_even, ssd_conv_w, ssd_conv_b, ssd_dt_bias, ssd_a_log, ssd_d, ssd_norm, gdn_conv_w, gdn_a_log, gdn_dt_bias, gdn_norm, w_in_odd, w_out_odd, mla_q_norm, mla_w_uq, mla_kv_norm, mla_w_uk, mla_w_uv, hg_lower_bound, hg_norm, xa_mem_norm, xa_w_q, xa_w_k, xa_w_v, xa_w_o, ffn_w1, ffn_w2):
    w = dict(norm_g=norm_g, w_in_even=w_in_even, w_out_even=w_out_even, ssd_conv_w=ssd_conv_w, ssd_conv_b=ssd_conv_b,
             ssd_dt_bias=ssd_dt_bias, ssd_a_log=ssd_a_log, ssd_d=ssd_d, ssd_norm=ssd_norm, gdn_conv_w=gdn_conv_w,
             gdn_a_log=gdn_a_log, gdn_dt_bias=gdn_dt_bias, gdn_norm=gdn_norm, w_in_odd=w_in_odd, w_out_odd=w_out_odd,
             mla_q_norm=mla_q_norm, mla_w_uq=mla_w_uq, mla_kv_norm=mla_kv_norm, mla_w_uk=mla_w_uk, mla_w_uv=mla_w_uv,
             hg_norm=hg_norm, xa_w_q=xa_w_q, xa_w_k=xa_w_k, xa_w_v=xa_w_v, xa_w_o=xa_w_o, ffn_w1=ffn_w1,
             ffn_w2=ffn_w2)
    wb = _prep_weights(w)
    depth = norm_g.shape[0]
    n_even, n_odd = (depth + 1) // 2, depth // 2
    lbs = jnp.cumsum(jax.nn.softmax(hg_lower_bound.astype(F32), axis=0), axis=0)
    lbs = lbs - lbs[0]

    b, m, d = mem_prompt.shape
    xa_dim = XA_HEADS * XA_HEAD_DIM
    mk, mv = [], []
    for layer in range(depth):
        mn = _rms(mem_prompt.reshape(b * m, d), xa_mem_norm[layer], BF16)
        mk.append(_matmul([mn], [wb["xa_w_k"][layer]], F32).reshape(b, m, xa_dim))
        mv.append(_matmul([mn], [wb["xa_w_v"][layer]], F32).reshape(b, m, xa_dim))
    p_mem_k = jnp.stack(mk)
    p_mem_v = jnp.stack(mv)
    dt = x_prompt.dtype
    prompt = _forward(
        x_prompt, p_mem_k.astype(BF16), p_mem_v.astype(BF16),
        jnp.zeros((n_even, b, SSD_HEADS, SSD_HEAD_DIM, SSD_STATE), dt),
        jnp.zeros((n_even, b, CONV_W - 1, SSD_CONV_DIM), dt),
        jnp.zeros((n_even, b, GDN_HEADS, GDN_DK, GDN_DV), dt),
        jnp.zeros((n_even, b, CONV_W - 1, GDN_CONV_DIM), dt),
        jnp.zeros((n_odd, b, 0, MLA_KV_RANK), dt),
        jnp.zeros((n_odd, b, 0, MLA_ROPE), dt),
        jnp.zeros((n_odd, b, HG_HEADS, HG_DK, HG_DK), dt),
        w, wb, lbs)
    db = x_sample.shape[0]
    sample = _forward(
        x_sample, cache_mem_k.reshape(depth, db, m, xa_dim).astype(BF16),
        cache_mem_v.reshape(depth, db, m, xa_dim).astype(BF16),
        state_ssd, state_ssd_conv, state_gdn, state_gdn_conv, cache_mla_ckv, cache_mla_krope, state_hgrn,
        w, wb, lbs)
    y_prompt, p_rest = prompt[0], prompt[1:]
    y_sample, s_rest = sample[0], sample[1:]
    return (y_prompt, y_sample, *p_rest,
            p_mem_k.reshape(depth, b, m, XA_HEADS, XA_HEAD_DIM), p_mem_v.reshape(depth, b, m, XA_HEADS, XA_HEAD_DIM),
            *s_rest)
```

```python
import functools

import numpy as np
import jax
import jax.numpy as jnp
from jax import lax
from jax.experimental import pallas as pl
from jax.experimental.pallas import tpu as pltpu

F32 = jnp.float32
BF16 = jnp.bfloat16

VMEM_LIMIT_BYTES = 56 * 1024 * 1024
LANES = 128
NORM_EPS = 1e-6
CHUNK = 64
CHUNK_SHIFT = 6
assert 1 << CHUNK_SHIFT == CHUNK
CONV_W = 4
ROPE_THETA = 10000.0
MASKED = -1e30


def _params(*sem):
    return pltpu.CompilerParams(dimension_semantics=sem, vmem_limit_bytes=VMEM_LIMIT_BYTES)


def _tile(n, cap, mult):
    if n <= cap:
        return n
    for d in range(cap - cap % mult, 0, -mult):
        if n % d == 0:
            return d
    return n


def _dot(a, b):
    return jnp.dot(a, b, preferred_element_type=F32)


def _dot_nt(a, b):
    return lax.dot_general(a, b, (((1,), (1,)), ((), ())), preferred_element_type=F32)


def _dot_tn(a, b):
    return lax.dot_general(a, b, (((0,), (0,)), ((), ())), preferred_element_type=F32)


def _split3(x):
    hi = x.astype(BF16)
    r = x - hi.astype(F32)
    mid = r.astype(BF16)
    lo = (r - mid.astype(F32)).astype(BF16)
    return hi, mid, lo


def _mask_dot(mask16, x):
    hi, mid, lo = _split3(x)
    return _dot(mask16, hi) + (_dot(mask16, mid) + _dot(mask16, lo))


def _mask_dot_nt(mask16, x):
    hi, mid, lo = _split3(x)
    return _dot_nt(mask16, hi) + (_dot_nt(mask16, mid) + _dot_nt(mask16, lo))


def _dot_x3(a, b):
    a_hi = a.astype(BF16)
    a_lo = (a - a_hi.astype(F32)).astype(BF16)
    b_hi = b.astype(BF16)
    b_lo = (b - b_hi.astype(F32)).astype(BF16)
    return _dot(a_hi, b_hi) + (_dot(a_hi, b_lo) + _dot(a_lo, b_hi))


def _softplus(x):
    return jnp.maximum(x, 0.0) + jnp.log1p(jnp.exp(-jnp.abs(x)))


def _sigmoid(x):
    return 1.0 / (1.0 + jnp.exp(-x))


def _silu(x):
    return x * _sigmoid(x)


def _pick_lane(a, idx):
    lane = lax.broadcasted_iota(jnp.int32, a.shape, 1)
    return jnp.sum(jnp.where(lane == idx, a, 0.0), axis=1, keepdims=True)


def _mm_body(*refs, n_ops, nks, act):
    a_refs, b_refs = refs[:n_ops], refs[n_ops:2 * n_ops]
    o_ref, acc_ref = refs[2 * n_ops], refs[2 * n_ops + 1]
    k = pl.program_id(2)
    nk = sum(nks)

    @pl.when(k == 0)
    def _():
        acc_ref[...] = jnp.zeros_like(acc_ref)

    off = 0
    for a_ref, b_ref, n in zip(a_refs, b_refs, nks):
        def step(a_ref=a_ref, b_ref=b_ref):
            acc_ref[...] += _dot(a_ref[...], b_ref[...])
        if n_ops == 1:
            step()
        else:
            pl.when((k >= off) & (k < off + n))(step)
        off += n

    @pl.when(k == nk - 1)
    def _():
        r = acc_ref[...]
        if act == "relu2":
            r = jnp.square(jnp.maximum(r, 0.0))
        o_ref[...] = r.astype(o_ref.dtype)


def _mm_full_k_body(a_ref, b_ref, o_ref, *, act):
    r = _dot(a_ref[...], b_ref[...])
    if act == "relu2":
        r = jnp.square(jnp.maximum(r, 0.0))
    o_ref[...] = r.astype(o_ref.dtype)


FULL_K_MAX = 4096
FULL_K_B_BLOCK_BYTES = 4 * 1024 * 1024


def _matmul(a_list, b_list, out_dtype, act=None, b_rows=None, tm_cap=1024, tn_cap=None, tk_cap=512):
    m = a_list[0].shape[0]
    n = b_list[0].shape[1]
    b_rows = [0] * len(a_list) if b_rows is None else b_rows
    tm = _tile(m, tm_cap, 8)
    k0 = a_list[0].shape[1]
    if len(a_list) == 1 and k0 <= FULL_K_MAX and b_rows[0] == 0 and b_list[0].shape[0] == k0:
        cap = tn_cap or min(2048, max(512, FULL_K_B_BLOCK_BYTES // (2 * k0)))
        tn = _tile(n, cap, LANES)
        return pl.pallas_call(
            functools.partial(_mm_full_k_body, act=act),
            grid=(m // tm, n // tn),
            in_specs=[pl.BlockSpec((tm, k0), lambda i, j: (i, 0)),
                      pl.BlockSpec((k0, tn), lambda i, j: (0, j))],
            out_specs=pl.BlockSpec((tm, tn), lambda i, j: (i, j)),
            out_shape=jax.ShapeDtypeStruct((m, n), out_dtype),
            compiler_params=_params("parallel", "arbitrary"),
            name="matmul_full_k",
        )(a_list[0], b_list[0])
    tn = _tile(n, tn_cap or 2048, LANES)
    tk = min(_tile(a.shape[1], tk_cap, LANES) for a in a_list)
    nks, offs = [], []
    for a, b, r0 in zip(a_list, b_list, b_rows):
        assert a.shape[0] == m and b.shape[1] == n and a.shape[1] % tk == 0 and r0 % tk == 0
        assert r0 + a.shape[1] <= b.shape[0]
        offs.append(sum(nks))
        nks.append(a.shape[1] // tk)
    in_specs = []
    for o, nki in zip(offs, nks):
        in_specs.append(pl.BlockSpec((tm, tk), lambda i, j, k, o=o, nki=nki: (i, jnp.clip(k - o, 0, nki - 1))))
    for o, nki, r0 in zip(offs, nks, b_rows):
        in_specs.append(pl.BlockSpec(
            (tk, tn), lambda i, j, k, o=o, nki=nki, rb=r0 // tk: (rb + jnp.clip(k - o, 0, nki - 1), j)))
    return pl.pallas_call(
        functools.partial(_mm_body, n_ops=len(a_list), nks=tuple(nks), act=act),
        grid=(m // tm, n // tn, sum(nks)),
        in_specs=in_specs,
        out_specs=pl.BlockSpec((tm, tn), lambda i, j, k: (i, j)),
        out_shape=jax.ShapeDtypeStruct((m, n), out_dtype),
        scratch_shapes=[pltpu.VMEM((tm, tn), F32)],
        compiler_params=_params("parallel", "parallel", "arbitrary"),
        name="matmul",
    )(*a_list, *b_list)


def _head_mm_body(a_ref, b_ref, o_ref, *, scale):
    r = _dot(a_ref[...].astype(BF16), b_ref[...])
    if scale != 1.0:
        r = r * scale
    o_ref[...] = r.astype(o_ref.dtype)


def _head_split_matmul(a, w, scale, tm_cap=1024):
    t = a.shape[0]
    nh, dk, n = w.shape
    tm = _tile(t, tm_cap, 8)
    return pl.pallas_call(
        functools.partial(_head_mm_body, scale=scale),
        grid=(t // tm, nh),
        in_specs=[pl.BlockSpec((tm, dk), lambda i, h: (i, h)),
                  pl.BlockSpec((None, dk, n), lambda i, h: (h, 0, 0))],
        out_specs=pl.BlockSpec((None, tm, n), lambda i, h: (h, i, 0)),
        out_shape=jax.ShapeDtypeStruct((nh, t, n), BF16),
        compiler_params=_params("parallel", "arbitrary"),
        name="head_split_matmul",
    )(a, w)


def _head_merge_matmul(a, w, tm_cap=1024):
    nh, t, k = a.shape
    dv = w.shape[2]
    tm = _tile(t, tm_cap, 8)
    return pl.pallas_call(
        functools.partial(_head_mm_body, scale=1.0),
        grid=(t // tm, nh),
        in_specs=[pl.BlockSpec((None, tm, k), lambda i, h: (h, i, 0)),
                  pl.BlockSpec((None, k, dv), lambda i, h: (h, 0, 0))],
        out_specs=pl.BlockSpec((tm, dv), lambda i, h: (i, h)),
        out_shape=jax.ShapeDtypeStruct((t, nh * dv), BF16),
        compiler_params=_params("parallel", "arbitrary"),
        name="head_merge_matmul",
    )(a, w)


def _rms_rows(x, g):
    return x * lax.rsqrt(jnp.mean(x * x, axis=-1, keepdims=True) + NORM_EPS) * g


def _rms_body(x_ref, g_ref, o_ref):
    o_ref[...] = _rms_rows(x_ref[...], g_ref[...]).astype(o_ref.dtype)


def _rms(x, g, out_dtype, col_block=0, width=None, tr_cap=256):
    t = x.shape[0]
    width = x.shape[1] if width is None else width
    tr = _tile(t, tr_cap, 8)
    return pl.pallas_call(
        _rms_body,
        grid=(t // tr,),
        in_specs=[pl.BlockSpec((tr, width), lambda i: (i, col_block)),
                  pl.BlockSpec((1, width), lambda i: (0, 0))],
        out_specs=pl.BlockSpec((tr, width), lambda i: (i, 0)),
        out_shape=jax.ShapeDtypeStruct((t, width), out_dtype),
        compiler_params=_params("parallel"),
        name="rms",
    )(x, g.reshape(1, width))


def _add_rms_body(x_ref, y_ref, g1_ref, g2_ref, xo_ref, ho_ref):
    xn = x_ref[...] + _rms_rows(y_ref[...], g1_ref[...])
    xo_ref[...] = xn
    ho_ref[...] = _rms_rows(xn, g2_ref[...]).astype(ho_ref.dtype)


def _add_rms_last_body(x_ref, y_ref, g1_ref, xo_ref):
    xo_ref[...] = x_ref[...] + _rms_rows(y_ref[...], g1_ref[...])


def _add_rms(x, y, g1, g2, tr_cap=256):
    t, d = x.shape
    tr = _tile(t, tr_cap, 8)
    row = pl.BlockSpec((tr, d), lambda i: (i, 0))
    gain = pl.BlockSpec((1, d), lambda i: (0, 0))
    if g2 is None:
        out = pl.pallas_call(
            _add_rms_last_body, grid=(t // tr,), in_specs=[row, row, gain], out_specs=row,
            out_shape=jax.ShapeDtypeStruct((t, d), F32), compiler_params=_params("parallel"),
            name="add_rms_last",
        )(x, y, g1.reshape(1, d))
        return out, None
    return pl.pallas_call(
        _add_rms_body, grid=(t // tr,), in_specs=[row, row, gain, gain], out_specs=[row, row],
        out_shape=[jax.ShapeDtypeStruct((t, d), F32), jax.ShapeDtypeStruct((t, d), BF16)],
        compiler_params=_params("parallel"), name="add_rms",
    )(x, y, g1.reshape(1, d), g2.reshape(1, d))


CONV_HALO = 8


def _conv_body(u_ref, buf_ref, w_ref, b_ref, o_ref, ext_ref, *, tl):
    lt = pl.program_id(2)
    lo = CONV_HALO - (CONV_W - 1)

    @pl.when(lt == 0)
    def _():
        ext_ref[lo:CONV_HALO, :] = buf_ref[0]

    @pl.when(lt > 0)
    def _():
        ext_ref[lo:CONV_HALO, :] = ext_ref[tl + lo:tl + CONV_HALO, :]

    ext_ref[CONV_HALO:CONV_HALO + tl, :] = u_ref[...]
    w = w_ref[...]
    acc = b_ref[...] + ext_ref[lo:lo + tl, :] * w[0:1, :]
    for j in range(1, CONV_W):
        acc = acc + ext_ref[lo + j:lo + j + tl, :] * w[j:j + 1, :]
    o_ref[...] = _silu(acc)


def _conv_silu(u, col0, c, buf, w, b, bsz, l, tl_cap=512, tc=1024):
    assert l >= CONV_W - 1 and col0 % tc == 0 and c % tc == 0
    tl = _tile(l, tl_cap, 8)
    nl = l // tl
    cb0 = col0 // tc
    return pl.pallas_call(
        functools.partial(_conv_body, tl=tl),
        grid=(bsz, c // tc, nl),
        in_specs=[pl.BlockSpec((tl, tc), lambda bi, ci, li: (bi * nl + li, cb0 + ci)),
                  pl.BlockSpec((1, CONV_W - 1, tc), lambda bi, ci, li: (bi, 0, ci)),
                  pl.BlockSpec((CONV_W, tc), lambda bi, ci, li: (0, ci)),
                  pl.BlockSpec((1, tc), lambda bi, ci, li: (0, ci))],
        out_specs=pl.BlockSpec((tl, tc), lambda bi, ci, li: (bi * nl + li, ci)),
        out_shape=jax.ShapeDtypeStruct((bsz * l, c), F32),
        scratch_shapes=[pltpu.VMEM((CONV_HALO + tl, tc), F32)],
        compiler_params=_params("parallel", "parallel", "arbitrary"),
        name="conv_silu",
    )(u, buf, w, b.reshape(1, c))


def _tri_masks(n):
    r = np.arange(n)
    return (jnp.asarray((r[:, None] >= r[None, :]).astype(np.float32), dtype=BF16),
            jnp.asarray(np.eye(n, dtype=np.float32), dtype=BF16))


def _ssd_body(x_ref, bm_ref, cm_ref, z_ref, sm_ref, dtb_ref, alog_ref, d_ref, gn_ref, s0_ref, tril_ref, eye_ref,
              y_ref, so_ref, s_scr, cst_scr, *, nc, hpg, hd, nheads):
    g = pl.program_id(1)
    c = pl.program_id(2)
    lc = x_ref.shape[0]

    @pl.when(c == 0)
    def _():
        s_scr[...] = s0_ref[0]

    dt = _softplus(sm_ref[:, 0:nheads] + dtb_ref[...])
    da = dt * (-jnp.exp(alog_ref[...]))
    cs = _mask_dot(tril_ref[...], da)
    cst_scr[...] = _mask_dot_nt(eye_ref[...], cs)
    tri = tril_ref[...].astype(F32) > 0.0

    x = x_ref[...]
    bm = bm_ref[...].astype(BF16)
    cm = cm_ref[...].astype(BF16)
    s = s_scr[...]
    cb = _dot_nt(cm, bm)
    y_state = _dot_nt(cm, s.astype(BF16))
    ys, xws, decs = [], [], []
    for j in range(hpg):
        h = g * hpg + j
        cs_col = _pick_lane(cs, h)
        dt_col = _pick_lane(dt, h)
        d_h = _pick_lane(d_ref[...], h)
        cs_row = cst_scr[pl.ds(h, 1), :]
        cs_last = cs_row[:, lc - 1:lc]
        decay = jnp.exp(jnp.where(tri, cs_col - cs_row, MASKED))
        xh = x[:, j * hd:(j + 1) * hd]
        xdt = xh * dt_col
        yh = _dot((cb * decay).astype(BF16), xdt.astype(BF16))
        yh = yh + y_state[:, j * hd:(j + 1) * hd] * jnp.exp(cs_col) + d_h * xh
        ys.append(yh)
        xws.append(xdt * jnp.exp(cs_last - cs_col))
        decs.append(jnp.broadcast_to(jnp.exp(cs_last), (hd, s.shape[1])))
    y = jnp.concatenate(ys, axis=1)
    xw = jnp.concatenate(xws, axis=1)
    s_new = s * jnp.concatenate(decs, axis=0) + _dot_tn(xw.astype(BF16), bm)
    s_scr[...] = s_new

    @pl.when(c == nc - 1)
    def _():
        so_ref[0] = s_new

    y = y * _silu(z_ref[...])
    y_ref[...] = _rms_rows(y, gn_ref[...]).astype(y_ref.dtype)


def _ssd(xbc, pm, z_col0, sm, dt_bias, a_log, d, gn, s0, bsz, l, nheads, hd, ngroups, nstate):
    t = bsz * l
    lc = CHUNK
    nc = l // lc
    hpg = nheads // ngroups
    gw = hpg * hd
    inner = nheads * hd
    assert l % lc == 0 and nheads == lc and z_col0 % gw == 0
    tril, eye = _tri_masks(lc)
    rows = lambda bi, gi, ci: bi * nc + ci
    const2 = lambda bi, gi, ci: (0, 0)
    y, s_new = pl.pallas_call(
        functools.partial(_ssd_body, nc=nc, hpg=hpg, hd=hd, nheads=nheads),
        grid=(bsz, ngroups, nc),
        in_specs=[pl.BlockSpec((lc, gw), lambda bi, gi, ci: (rows(bi, gi, ci), gi)),
                  pl.BlockSpec((lc, nstate), lambda bi, gi, ci: (rows(bi, gi, ci), inner // nstate + gi)),
                  pl.BlockSpec((lc, nstate), lambda bi, gi, ci: (rows(bi, gi, ci), inner // nstate + ngroups + gi)),
                  pl.BlockSpec((lc, gw), lambda bi, gi, ci: (rows(bi, gi, ci), z_col0 // gw + gi)),
                  pl.BlockSpec((lc, sm.shape[1]), lambda bi, gi, ci: (rows(bi, gi, ci), 0)),
                  pl.BlockSpec((1, nheads), const2),
                  pl.BlockSpec((1, nheads), const2),
                  pl.BlockSpec((1, nheads), const2),
                  pl.BlockSpec((1, gw), lambda bi, gi, ci: (0, gi)),
                  pl.BlockSpec((1, gw, nstate), lambda bi, gi, ci: (bi, gi, 0)),
                  pl.BlockSpec((lc, lc), const2),
                  pl.BlockSpec((lc, lc), const2)],
        out_specs=[pl.BlockSpec((lc, gw), lambda bi, gi, ci: (rows(bi, gi, ci), gi)),
                   pl.BlockSpec((1, gw, nstate), lambda bi, gi, ci: (bi, gi, 0))],
        out_shape=[jax.ShapeDtypeStruct((t, inner), BF16),
                   jax.ShapeDtypeStruct((bsz, inner, nstate), F32)],
        scratch_shapes=[pltpu.VMEM((gw, nstate), F32), pltpu.VMEM((nheads, lc), F32)],
        compiler_params=_params("parallel", "parallel", "arbitrary"),
        name="ssd_scan",
    )(xbc, xbc, xbc, pm, sm, dt_bias.reshape(1, nheads), a_log.reshape(1, nheads), d.reshape(1, nheads),
      gn.reshape(1, inner), s0, tril, eye)
    return y, s_new


def _gdn_heads(qs, ks, vs, gates, betas, cs_cols, cs_rows, states, tri, eye, gn):
    n = len(qs)
    hd = range(n)
    lc, dk = qs[0].shape
    dv = vs[0].shape[1]
    cs_last = [cs_rows[i][:, lc - 1:lc] for i in hd]
    decay = [jnp.exp(jnp.where(tri, cs_cols[i] - cs_rows[i], MASKED)) for i in hd]
    qn = [qs[i] * (lax.rsqrt(jnp.sum(qs[i] * qs[i], axis=1, keepdims=True) + 1e-6) * dk ** -0.5) for i in hd]
    kn = [ks[i] * lax.rsqrt(jnp.sum(ks[i] * ks[i], axis=1, keepdims=True) + 1e-6) for i in hd]
    kb = [kn[i] * betas[i] for i in hd]
    kn16 = [kn[i].astype(BF16) for i in hd]
    off_diag = 1.0 - eye
    a = [_dot_nt(kb[i].astype(BF16), kn16[i]) * decay[i] * off_diag for i in hd]
    qk = [_dot_nt(qn[i].astype(BF16), kn16[i]) * decay[i] for i in hd]
    s16 = [states[i].astype(BF16) for i in hd]
    o_state = [_dot((qn[i] * jnp.exp(cs_cols[i])).astype(BF16), s16[i]) for i in hd]
    tinv = [eye - a[i] for i in hd]
    p = [_dot_x3(a[i], a[i]) for i in hd]
    steps = max(1, int(np.ceil(np.log2(lc))) - 1)
    for st in range(steps):
        tinv = [tinv[i] + _dot_x3(tinv[i], p[i]) for i in hd]
        if st + 1 < steps:
            p = [_dot_x3(p[i], p[i]) for i in hd]
    rhs = [jnp.concatenate([vs[i] * betas[i], kb[i] * jnp.exp(cs_cols[i])], axis=1).astype(BF16) for i in hd]
    sol = [_dot(tinv[i].astype(BF16), rhs[i]) for i in hd]
    vn16 = [(sol[i][:, :dv] - _dot(sol[i][:, dv:].astype(BF16), s16[i])).astype(BF16) for i in hd]
    o = [o_state[i] + _dot(qk[i].astype(BF16), vn16[i]) for i in hd]
    k_end = [(kn[i] * jnp.exp(cs_last[i] - cs_cols[i])).astype(BF16) for i in hd]
    s_new = [states[i] * jnp.exp(cs_last[i]) + _dot_tn(k_end[i], vn16[i]) for i in hd]
    outs = [_rms_rows(o[i], gn) * _silu(gates[i]) for i in hd]
    return outs, s_new


def _gdn_body(q_ref, k_ref, v_ref, gate_ref, sm_ref, alog_ref, dtb_ref, gn_ref, s0_ref, tril_ref, eye_ref, eyeh_ref,
              o_ref, so_ref, s_scr, cst_scr, *, nc, nheads, a_col0, hps):
    hb = pl.program_id(1)
    c = pl.program_id(2)
    d = q_ref.shape[1] // hps

    @pl.when(c == 0)
    def _():
        s_scr[...] = s0_ref[0]

    a_raw = sm_ref[:, a_col0:a_col0 + nheads]
    b_raw = sm_ref[:, a_col0 + nheads:a_col0 + 2 * nheads]
    gl = -jnp.exp(alog_ref[...]) * _softplus(a_raw + dtb_ref[...])
    beta_all = _sigmoid(b_raw)
    cs = _mask_dot(tril_ref[...], gl)
    cst_scr[...] = _mask_dot_nt(eyeh_ref[...], cs)
    tri = tril_ref[...].astype(F32) > 0.0
    eye = eye_ref[...].astype(F32)
    gn = gn_ref[...]
    sls = [slice(j * d, (j + 1) * d) for j in range(hps)]
    hidx = [hb * hps + j for j in range(hps)]
    outs, s_new = _gdn_heads(
        [q_ref[:, sl] for sl in sls], [k_ref[:, sl] for sl in sls], [v_ref[:, sl] for sl in sls],
        [gate_ref[:, sl] for sl in sls], [_pick_lane(beta_all, h) for h in hidx], [_pick_lane(cs, h) for h in hidx],
        [cst_scr[pl.ds(h, 1), :] for h in hidx], [s_scr[j] for j in range(hps)], tri, eye, gn)
    for j in range(hps):
        s_scr[j] = s_new[j]
        o_ref[:, sls[j]] = outs[j].astype(o_ref.dtype)

    @pl.when(c == nc - 1)
    def _():
        so_ref[0] = s_scr[...]


GDN_HEADS_PER_STEP = 8


def _gdn(qkv, pm, gate_col0, sm, a_col0, a_log, dt_bias, gn, s0, bsz, l, nheads, dk, dv):
    hps = GDN_HEADS_PER_STEP
    w = hps * dv
    assert dk == dv and gate_col0 % w == 0 and nheads % hps == 0
    t = bsz * l
    lc = CHUNK
    nc = l // lc
    nhb = nheads // hps
    tril, eye = _tri_masks(lc)
    eyeh = jnp.asarray(np.eye(nheads, dtype=np.float32), dtype=BF16)
    rows = lambda bi, hi, ci: bi * nc + ci
    const2 = lambda bi, hi, ci: (0, 0)
    col = lambda k: pl.BlockSpec((lc, w), lambda bi, hi, ci: (rows(bi, hi, ci), k * nhb + hi))
    state = pl.BlockSpec((1, hps, dk, dv), lambda bi, hi, ci: (bi, hi, 0, 0))
    o, s_new = pl.pallas_call(
        functools.partial(_gdn_body, nc=nc, nheads=nheads, a_col0=a_col0, hps=hps),
        grid=(bsz, nhb, nc),
        in_specs=[col(0), col(1), col(2),
                  pl.BlockSpec((lc, w), lambda bi, hi, ci: (rows(bi, hi, ci), gate_col0 // w + hi)),
                  pl.BlockSpec((lc, sm.shape[1]), lambda bi, hi, ci: (rows(bi, hi, ci), 0)),
                  pl.BlockSpec((1, nheads), const2),
                  pl.BlockSpec((1, nheads), const2),
                  pl.BlockSpec((1, dv), const2),
                  state,
                  pl.BlockSpec((lc, lc), const2),
                  pl.BlockSpec((lc, lc), const2),
                  pl.BlockSpec((nheads, nheads), const2)],
        out_specs=[pl.BlockSpec((lc, w), lambda bi, hi, ci: (rows(bi, hi, ci), hi)), state],
        out_shape=[jax.ShapeDtypeStruct((t, nheads * dv), BF16),
                   jax.ShapeDtypeStruct((bsz, nheads, dk, dv), F32)],
        scratch_shapes=[pltpu.VMEM((hps, dk, dv), F32), pltpu.VMEM((nheads, lc), F32)],
        compiler_params=_params("parallel", "parallel", "arbitrary"),
        name="gdn_scan",
    )(qkv, qkv, qkv, pm, sm, a_log.reshape(1, nheads), dt_bias.reshape(1, nheads), gn.reshape(1, dv), s0,
      tril, eye, eyeh)
    return o, s_new


def _hgrn_levels(lc):
    hs = []
    h = lc // 2
    while h >= 1:
        hs.append(h)
        h //= 2
    return hs


def _hgrn_masks(lc):
    r = np.arange(lc)
    t, u = r[:, None], r[None, :]
    mats = [u <= t, u > t]
    for h in _hgrn_levels(lc):
        start = (t // h) * h
        mats.append((u >= start) & (u <= t))
        mats.append((u > t) & (u <= start + h - 1))
    return jnp.asarray(np.concatenate(mats, axis=0).astype(np.float32), dtype=BF16)


def _hgrn_heads(qs, hfs, vs, gates, lbs, gn, sts, mst16):
    n = len(qs)
    hd = range(n)
    lc, d = qs[0].shape
    logf = [jnp.log(lbs[i] + (1.0 - lbs[i]) * _sigmoid(hfs[i])) for i in hd]
    kk = [(1.0 - lbs[i]) * _sigmoid(-hfs[i]) for i in hd]
    c3 = [_dot(mst16, jnp.concatenate(_split3(logf[i]), axis=1)) for i in hd]
    cums = [c3[i][:, 0:d] + (c3[i][:, d:2 * d] + c3[i][:, 2 * d:3 * d]) for i in hd]
    v16 = [vs[i].astype(BF16) for i in hd]
    st16 = [sts[i].astype(BF16) for i in hd]
    o_state = [_dot_nt((qs[i] * jnp.exp(cums[i][0:lc])).astype(BF16), st16[i]) for i in hd]

    row = lax.broadcasted_iota(jnp.int32, (lc, 1), 0)
    ti = lax.broadcasted_iota(jnp.int32, (lc, lc), 0)
    si = lax.broadcasted_iota(jnp.int32, (lc, lc), 1)
    att = [jnp.where(ti == si, jnp.sum(qs[i] * kk[i], axis=1, keepdims=True), 0.0) for i in hd]
    for li, hs in enumerate(_hgrn_levels(lc)):
        sh = int(np.log2(hs))
        upper = ((row >> sh) & 1) == 1
        same_block = (ti >> (sh + 1)) == (si >> (sh + 1))
        for i in hd:
            seg = cums[i][(2 + 2 * li) * lc:(3 + 2 * li) * lc]
            rest = cums[i][(3 + 2 * li) * lc:(4 + 2 * li) * lc]
            ql = jnp.where(upper, qs[i] * jnp.exp(seg), 0.0)
            kl = jnp.where(upper, 0.0, kk[i] * jnp.exp(rest))
            att[i] = att[i] + jnp.where(same_block, _dot_nt(ql.astype(BF16), kl.astype(BF16)), 0.0)

    outs, st_new = [], []
    for i in hd:
        o = o_state[i] + _dot(att[i].astype(BF16), v16[i])
        k_end = kk[i] * jnp.exp(cums[i][lc:2 * lc])
        st_new.append(sts[i] * jnp.exp(cums[i][lc - 1:lc, :]) + _dot_tn(v16[i], k_end.astype(BF16)))
        outs.append(_rms_rows(o, gn) * _silu(gates[i]))
    return outs, st_new


def _hgrn_body(q_ref, f_ref, i_ref, gate_ref, lb_ref, gn_ref, s0_ref, mst_ref, o_ref, so_ref, st_scr, *, nc, hps):
    c = pl.program_id(2)
    d = q_ref.shape[1] // hps

    @pl.when(c == 0)
    def _():
        for j in range(hps):
            st_scr[j] = s0_ref[0, j].T

    mst16 = mst_ref[...]
    gn = gn_ref[...]
    sls = [slice(j * d, (j + 1) * d) for j in range(hps)]
    outs, st_new = _hgrn_heads(
        [q_ref[:, sl] for sl in sls], [f_ref[:, sl] for sl in sls], [i_ref[:, sl] for sl in sls],
        [gate_ref[:, sl] for sl in sls], [lb_ref[:, sl] for sl in sls], gn, [st_scr[j] for j in range(hps)], mst16)
    for j in range(hps):
        st_scr[j] = st_new[j]
        o_ref[:, sls[j]] = outs[j].astype(o_ref.dtype)

    @pl.when(c == nc - 1)
    def _():
        for j in range(hps):
            so_ref[0, j] = st_scr[j].T


HGRN_HEADS_PER_STEP = 4


def _hgrn(pm, lb, gn, s0, bsz, l, nheads, dk):
    hps = HGRN_HEADS_PER_STEP
    assert nheads % hps == 0
    w = hps * dk
    nhb = nheads // hps
    t = bsz * l
    lc = CHUNK
    nc = l // lc
    mst = _hgrn_masks(lc)
    rows = lambda bi, hi, ci: bi * nc + ci
    const2 = lambda bi, hi, ci: (0, 0)
    col = lambda k: pl.BlockSpec((lc, w), lambda bi, hi, ci: (rows(bi, hi, ci), k * nhb + hi))
    state = pl.BlockSpec((1, hps, dk, dk), lambda bi, hi, ci: (bi, hi, 0, 0))
    o, s_new = pl.pallas_call(
        functools.partial(_hgrn_body, nc=nc, hps=hps),
        grid=(bsz, nhb, nc),
        in_specs=[col(0), col(1), col(2), col(3),
                  pl.BlockSpec((1, w), lambda bi, hi, ci: (0, hi)),
                  pl.BlockSpec((1, dk), const2),
                  state,
                  pl.BlockSpec(mst.shape, const2)],
        out_specs=[pl.BlockSpec((lc, w), lambda bi, hi, ci: (rows(bi, hi, ci), hi)), state],
        out_shape=[jax.ShapeDtypeStruct((t, nheads * dk), BF16),
                   jax.ShapeDtypeStruct((bsz, nheads, dk, dk), F32)],
        scratch_shapes=[pltpu.VMEM((hps, dk, dk), F32)],
        compiler_params=_params("parallel", "parallel", "arbitrary"),
        name="hgrn_scan",
    )(pm, pm, pm, pm, lb.reshape(1, nheads * dk), gn.reshape(1, dk), s0, mst)
    return o, s_new


MLA_SUB_ROWS = 512


def _mla_body(qi_ref, ki_ref, last_ref, ql_ref, qr_ref, ck_ref, kr_ref, o_ref, m_scr, l_scr, acc_scr, *,
              tq, tk, t_past, hs):
    step = pl.program_id(1)
    iq = qi_ref[step]
    ikv = ki_ref[step]
    nh, _, r = ql_ref.shape
    rows = hs * tq
    first_limit = t_past + iq * tq + CHUNK

    @pl.when(ikv == 0)
    def _():
        m_scr[...] = jnp.full_like(m_scr, MASKED)
        l_scr[...] = jnp.zeros_like(l_scr)
        acc_scr[...] = jnp.zeros_like(acc_scr)

    def update(masked):
        ck = ck_ref[0]
        kr = kr_ref[0]
        if masked:
            tok = lax.broadcasted_iota(jnp.int32, (rows, 1), 0) & (tq - 1)
            limit = t_past + iq * tq + ((tok >> CHUNK_SHIFT) + 1) * CHUNK
            visible = (ikv * tk + lax.broadcasted_iota(jnp.int32, (1, tk), 1)) < limit

        def scores(g):
            hsl = slice(g * hs, (g + 1) * hs)
            ql = ql_ref[hsl].reshape(rows, r)
            qr = qr_ref[hsl].reshape(rows, qr_ref.shape[2])
            return _dot_nt(ql, ck) + _dot_nt(qr, kr)

        ngroups = nh // hs
        s_next = scores(0)
        for g in range(ngroups):
            rsl = slice(g * rows, (g + 1) * rows)
            s = s_next
            if g + 1 < ngroups:
                s_next = scores(g + 1)
            if masked:
                s = jnp.where(visible, s, MASKED)
            m_old = m_scr[rsl]
            m_new = jnp.maximum(m_old, jnp.max(s, axis=1, keepdims=True))
            alpha = jnp.exp(m_old - m_new)
            p = jnp.exp(s - m_new)
            l_scr[rsl] = alpha * l_scr[rsl] + jnp.sum(p, axis=1, keepdims=True)
            acc_scr[rsl] = alpha * acc_scr[rsl] + _dot(p.astype(BF16), ck)
            m_scr[rsl] = m_new

    fully_visible = (ikv + 1) * tk <= first_limit
    pl.when(fully_visible)(lambda: update(False))
    pl.when(jnp.logical_not(fully_visible))(lambda: update(True))

    @pl.when(last_ref[step] == 1)
    def _():
        o = acc_scr[...] / l_scr[...]
        o_ref[...] = o.reshape(nh, tq, r).astype(o_ref.dtype)


def _mla_attention(q_lat, q_rope, ckv_all, kr_all, bsz, l, t_past, tq, tk):
    nh, t, r = q_lat.shape
    p = q_rope.shape[2]
    tkeys = ckv_all.shape[1]
    assert l % tq == 0 and tq % CHUNK == 0 and t_past % CHUNK == 0 and tkeys % tk == 0 and tkeys >= t_past + l
    assert tq & (tq - 1) == 0
    nq = l // tq
    hs = min(nh, max(1, MLA_SUB_ROWS // tq))
    assert nh % hs == 0
    qi, ki, last = [], [], []
    for q in range(nq):
        need = -(-(t_past + (q + 1) * tq) // tk)
        qi += [q] * need
        ki += list(range(need))
        last += [0] * (need - 1) + [1]
    tables = [jnp.asarray(np.asarray(a, np.int32)) for a in (qi, ki, last)]
    qmap = lambda bi, si, qi_ref, ki_ref, last_ref: (0, bi * nq + qi_ref[si], 0)
    kmap = lambda bi, si, qi_ref, ki_ref, last_ref: (bi, ki_ref[si], 0)
    return pl.pallas_call(
        functools.partial(_mla_body, tq=tq, tk=tk, t_past=t_past, hs=hs),
        grid_spec=pltpu.PrefetchScalarGridSpec(
            num_scalar_prefetch=3,
            grid=(bsz, len(qi)),
            in_specs=[pl.BlockSpec((nh, tq, r), qmap),
                      pl.BlockSpec((nh, tq, p), qmap),
                      pl.BlockSpec((1, tk, r), kmap),
                      pl.BlockSpec((1, tk, p), kmap)],
            out_specs=pl.BlockSpec((nh, tq, r), qmap),
            scratch_shapes=[pltpu.VMEM((nh * tq, 1), F32), pltpu.VMEM((nh * tq, 1), F32),
                            pltpu.VMEM((nh * tq, r), F32)]),
        out_shape=jax.ShapeDtypeStruct((nh, t, r), BF16),
        compiler_params=_params("parallel", "arbitrary"),
        name="mla_attention",
    )(*tables, q_lat, q_rope, ckv_all, kr_all)


def _xattn_body(q_ref, k_ref, v_ref, o_ref, *, nheads, hd):
    q = q_ref[...]
    k = k_ref[0]
    v = v_ref[0]
    outs = []
    for h in range(nheads):
        sl = slice(h * hd, (h + 1) * hd)
        s = _dot_nt(q[:, sl], k[:, sl]) * hd ** -0.5
        s = s - jnp.max(s, axis=1, keepdims=True)
        p = jnp.exp(s)
        p = p / jnp.sum(p, axis=1, keepdims=True)
        outs.append(_dot(p.astype(BF16), v[:, sl]))
    o_ref[...] = jnp.concatenate(outs, axis=1).astype(o_ref.dtype)


def _cross_attention(q, mem_k, mem_v, bsz, l, nheads, hd, tl_cap=512):
    t, d = q.shape
    m = mem_k.shape[1]
    tl = _tile(l, tl_cap, 8)
    nl = l // tl
    return pl.pallas_call(
        functools.partial(_xattn_body, nheads=nheads, hd=hd),
        grid=(bsz, nl),
        in_specs=[pl.BlockSpec((tl, d), lambda bi, li: (bi * nl + li, 0)),
                  pl.BlockSpec((1, m, d), lambda bi, li: (bi, 0, 0)),
                  pl.BlockSpec((1, m, d), lambda bi, li: (bi, 0, 0))],
        out_specs=pl.BlockSpec((tl, d), lambda bi, li: (bi * nl + li, 0)),
        out_shape=jax.ShapeDtypeStruct((t, d), BF16),
        compiler_params=_params("parallel", "parallel"),
        name="cross_attention",
    )(q, mem_k, mem_v)


SSD_HEADS, SSD_HEAD_DIM, SSD_GROUPS, SSD_STATE = 64, 64, 8, 128
SSD_INNER = SSD_HEADS * SSD_HEAD_DIM
SSD_CONV_DIM = SSD_INNER + 2 * SSD_GROUPS * SSD_STATE
GDN_HEADS, GDN_DK, GDN_DV = 32, 128, 128
GDN_CONV_DIM = 2 * GDN_HEADS * GDN_DK + GDN_HEADS * GDN_DV
MLA_HEADS, MLA_Q_RANK, MLA_KV_RANK, MLA_NOPE, MLA_ROPE, MLA_V = 32, 1024, 512, 128, 64, 128
HG_HEADS, HG_DK = 32, 128
HG_W = HG_HEADS * HG_DK
XA_HEADS, XA_HEAD_DIM = 4, 128
MLA_TK = 512
MLA_TQ = 128


def _rope(x, pos):
    half = x.shape[-1] // 2
    inv = ROPE_THETA ** (-jnp.arange(half, dtype=F32) / half)
    ang = pos[:, None] * inv[None, :]
    ang = ang.reshape((ang.shape[0],) + (1,) * (x.ndim - 3) + (half,))
    cos, sin = jnp.cos(ang), jnp.sin(ang)
    x1, x2 = x[..., :half], x[..., half:]
    return jnp.concatenate([x1 * cos - x2 * sin, x2 * cos + x1 * sin], axis=-1)


def _prep_weights(w):
    out = {}
    e_sizes = [SSD_INNER, SSD_CONV_DIM, SSD_HEADS, GDN_CONV_DIM, GDN_HEADS, GDN_HEADS, GDN_HEADS * GDN_DV]
    e_off = np.concatenate([[0], np.cumsum(e_sizes)])
    wi = w["w_in_even"]
    sec = lambda a, offs, i: a[:, :, offs[i]:offs[i + 1]]
    out["w_in_even_main"] = jnp.concatenate(
        [sec(wi, e_off, 0), sec(wi, e_off, 1), sec(wi, e_off, 3), sec(wi, e_off, 6)], axis=-1).astype(BF16)
    out["w_in_even_small"] = jnp.concatenate(
        [sec(wi, e_off, 2), sec(wi, e_off, 4), sec(wi, e_off, 5)], axis=-1).astype(BF16)
    o_sizes = [MLA_Q_RANK, MLA_KV_RANK, MLA_ROPE, HG_W, HG_W, HG_W, HG_W]
    o_off = np.concatenate([[0], np.cumsum(o_sizes)])
    wo = w["w_in_odd"]
    out["w_in_odd_main"] = jnp.concatenate(
        [sec(wo, o_off, 3), sec(wo, o_off, 4), sec(wo, o_off, 5), sec(wo, o_off, 6), sec(wo, o_off, 0)],
        axis=-1).astype(BF16)
    out["w_in_odd_small"] = jnp.concatenate([sec(wo, o_off, 1), sec(wo, o_off, 2)], axis=-1).astype(BF16)
    n_odd = wo.shape[0]
    uq = w["mla_w_uq"].reshape(n_odd, MLA_Q_RANK, MLA_HEADS, MLA_NOPE + MLA_ROPE)
    out["w_uq"] = jnp.concatenate(
        [uq[..., :MLA_NOPE].reshape(n_odd, MLA_Q_RANK, MLA_HEADS * MLA_NOPE),
         uq[..., MLA_NOPE:].reshape(n_odd, MLA_Q_RANK, MLA_HEADS * MLA_ROPE)], axis=-1).astype(BF16)
    out["w_uk"] = jnp.transpose(w["mla_w_uk"], (0, 2, 3, 1)).astype(BF16)
    out["w_uv"] = jnp.transpose(w["mla_w_uv"], (0, 2, 1, 3)).astype(BF16)
    for name in ("w_out_even", "w_out_odd", "xa_w_q", "xa_w_k", "xa_w_v", "xa_w_o", "ffn_w1", "ffn_w2"):
        out[name] = w[name].astype(BF16)
    return out


def _even_mixer(hn, w, wb, e, bsz, l, ssd_s, ssd_buf, gdn_s, gdn_buf):
    pm = _matmul([hn], [wb["w_in_even_main"][e]], F32)
    sm = _matmul([hn], [wb["w_in_even_small"][e]], F32)
    xbc_col0, qkv_col0 = SSD_INNER, SSD_INNER + SSD_CONV_DIM
    gate_col0 = qkv_col0 + GDN_CONV_DIM
    xbc = _conv_silu(pm, xbc_col0, SSD_CONV_DIM, ssd_buf, w["ssd_conv_w"][e], w["ssd_conv_b"][e], bsz, l)
    qkv = _conv_silu(pm, qkv_col0, GDN_CONV_DIM, gdn_buf, w["gdn_conv_w"][e],
                     jnp.zeros((GDN_CONV_DIM,), F32), bsz, l)
    y, ssd_new = _ssd(xbc, pm, 0, sm, w["ssd_dt_bias"][e], w["ssd_a_log"][e], w["ssd_d"][e], w["ssd_norm"][e],
                      ssd_s.reshape(bsz, SSD_INNER, SSD_STATE), bsz, l, SSD_HEADS, SSD_HEAD_DIM, SSD_GROUPS,
                      SSD_STATE)
    o, gdn_new = _gdn(qkv, pm, gate_col0, sm, SSD_HEADS, w["gdn_a_log"][e], w["gdn_dt_bias"][e], w["gdn_norm"][e],
                      gdn_s, bsz, l, GDN_HEADS, GDN_DK, GDN_DV)
    w_out = wb["w_out_even"][e]
    mix = _matmul([y, o], [w_out, w_out], F32, b_rows=[0, SSD_INNER])
    pm3 = pm.reshape(bsz, l, pm.shape[1])
    tail = pm3[:, l - (CONV_W - 1):]
    states = (ssd_new.reshape(bsz, SSD_HEADS, SSD_HEAD_DIM, SSD_STATE), tail[:, :, xbc_col0:qkv_col0],
              gdn_new, tail[:, :, qkv_col0:gate_col0])
    return mix, states


def _odd_mixer(hn, w, wb, j, lb, bsz, l, ckv_past, kr_past, hg_s):
    t_past = ckv_past.shape[1]
    pm = _matmul([hn], [wb["w_in_odd_main"][j]], F32)
    sm = _matmul([hn], [wb["w_in_odd_small"][j]], F32)
    posf = (t_past + jnp.arange(l, dtype=jnp.int32)).astype(F32)
    scale = (MLA_NOPE + MLA_ROPE) ** -0.5
    cqn = _rms(pm, w["mla_q_norm"][j], BF16, col_block=4 * HG_W // MLA_Q_RANK, width=MLA_Q_RANK)
    q = _matmul([cqn], [wb["w_uq"][j]], F32)
    q_lat = _head_split_matmul(q, wb["w_uk"][j], scale)
    nope_w = MLA_HEADS * MLA_NOPE
    q_rope = _rope(q[:, nope_w:].reshape(bsz, l, MLA_HEADS, MLA_ROPE), posf) * scale
    q_rope = jnp.transpose(q_rope.reshape(bsz * l, MLA_HEADS, MLA_ROPE), (1, 0, 2)).astype(BF16)
    c_kv = _rms(sm, w["mla_kv_norm"][j], F32, col_block=0, width=MLA_KV_RANK).reshape(bsz, l, MLA_KV_RANK)
    k_rope = _rope(sm[:, MLA_KV_RANK:].reshape(bsz, l, MLA_ROPE), posf)
    tkeys = t_past + l
    pad = (-tkeys) % MLA_TK
    ckv_all = jnp.concatenate([ckv_past.astype(BF16), c_kv.astype(BF16),
                               jnp.zeros((bsz, pad, MLA_KV_RANK), BF16)], axis=1)
    kr_all = jnp.concatenate([kr_past.astype(BF16), k_rope.astype(BF16),
                              jnp.zeros((bsz, pad, MLA_ROPE), BF16)], axis=1)
    tq = MLA_TQ if l % MLA_TQ == 0 else CHUNK
    o_lat = _mla_attention(q_lat, q_rope, ckv_all, kr_all, bsz, l, t_past, tq, MLA_TK)
    o_mla = _head_merge_matmul(o_lat, wb["w_uv"][j])
    o_hg, hg_new = _hgrn(pm, lb, w["hg_norm"][j], hg_s, bsz, l, HG_HEADS, HG_DK)
    w_out = wb["w_out_odd"][j]
    mix = _matmul([o_mla, o_hg], [w_out, w_out], F32, b_rows=[0, MLA_HEADS * MLA_V])
    return mix, (c_kv, k_rope, hg_new)


def _forward(x, mem_k, mem_v, ssd_s, ssd_buf, gdn_s, gdn_buf, ckv_past, kr_past, hg_s, w, wb, lbs):
    bsz, l, d = x.shape
    depth = w["norm_g"].shape[0]
    x = x.reshape(bsz * l, d)
    hn = _rms(x, w["norm_g"][0, 0], BF16)
    ev, od = [], []
    for layer in range(depth):
        ng = w["norm_g"][layer]
        if layer % 2 == 0:
            e = layer // 2
            mix, st = _even_mixer(hn, w, wb, e, bsz, l, ssd_s[e], ssd_buf[e], gdn_s[e], gdn_buf[e])
            ev.append(st)
        else:
            j = layer // 2
            mix, st = _odd_mixer(hn, w, wb, j, lbs[layer], bsz, l, ckv_past[j], kr_past[j], hg_s[j])
            od.append(st)
        x, hn = _add_rms(x, mix, ng[1], ng[2])
        q = _matmul([hn], [wb["xa_w_q"][layer]], BF16)
        xo = _cross_attention(q, mem_k[layer], mem_v[layer], bsz, l, XA_HEADS, XA_HEAD_DIM)
        xa = _matmul([xo], [wb["xa_w_o"][layer]], F32)
        x, hn = _add_rms(x, xa, ng[3], ng[4])
        h1 = _matmul([hn], [wb["ffn_w1"][layer]], BF16, act="relu2")
        f = _matmul([h1], [wb["ffn_w2"][layer]], F32)
        x, hn = _add_rms(x, f, ng[5], w["norm_g"][layer + 1, 0] if layer + 1 < depth else None)
    stack = lambda items, i: jnp.stack([it[i] for it in items])
    return (x.reshape(bsz, l, d), stack(ev, 0), stack(ev, 1), stack(ev, 2), stack(ev, 3),
            stack(od, 0), stack(od, 1), stack(od, 2))


def kernel(x_prompt, x_sample, mem_prompt, state_ssd, state_ssd_conv, state_gdn, state_gdn_conv, cache_mla_ckv, cache_mla_krope, state_hgrn, cache_mem_k, cache_mem_v, norm_g, w_in_even, w_out_even, ssd_conv_w, ssd_conv_b, ssd_dt_bias, ssd_a_log, ssd_d, ssd_norm, gdn_conv_w, gdn_a_log, gdn_dt_bias, gdn_norm, w_in_odd, w_out_odd, mla_q_norm, mla_w_uq, mla_kv_norm, mla_w_uk, mla_w_uv, hg_lower_bound, hg_norm, xa_mem_norm, xa_w_q, xa_w_k, xa_w_v, xa_w_o, ffn_w1, ffn_w2):
    w = dict(norm_g=norm_g, w_in_even=w_in_even, w_out_even=w_out_even, ssd_conv_w=ssd_conv_w, ssd_conv_b=ssd_conv_b,
             ssd_dt_bias=ssd_dt_bias, ssd_a_log=ssd_a_log, ssd_d=ssd_d, ssd_norm=ssd_norm, gdn_conv_w=gdn_conv_w,
             gdn_a_log=gdn_a_log, gdn_dt_bias=gdn_dt_bias, gdn_norm=gdn_norm, w_in_odd=w_in_odd, w_out_odd=w_out_odd,
             mla_q_norm=mla_q_norm, mla_w_uq=mla_w_uq, mla_kv_norm=mla_kv_norm, mla_w_uk=mla_w_uk, mla_w_uv=mla_w_uv,
             hg_norm=hg_norm, xa_w_q=xa_w_q, xa_w_k=xa_w_k, xa_w_v=xa_w_v, xa_w_o=xa_w_o, ffn_w1=ffn_w1,
             ffn_w2=ffn_w2)
    wb = _prep_weights(w)
    depth = norm_g.shape[0]
    n_even, n_odd = (depth + 1) // 2, depth // 2
    lbs = jnp.cumsum(jax.nn.softmax(hg_lower_bound.astype(F32), axis=0), axis=0)
    lbs = lbs - lbs[0]

    b, m, d = mem_prompt.shape
    xa_dim = XA_HEADS * XA_HEAD_DIM
    mk, mv = [], []
    for layer in range(depth):
        mn = _rms(mem_prompt.reshape(b * m, d), xa_mem_norm[layer], BF16)
        mk.append(_matmul([mn], [wb["xa_w_k"][layer]], F32).reshape(b, m, xa_dim))
        mv.append(_matmul([mn], [wb["xa_w_v"][layer]], F32).reshape(b, m, xa_dim))
    p_mem_k = jnp.stack(mk)
    p_mem_v = jnp.stack(mv)
    dt = x_prompt.dtype
    prompt = _forward(
        x_prompt, p_mem_k.astype(BF16), p_mem_v.astype(BF16),
        jnp.zeros((n_even, b, SSD_HEADS, SSD_HEAD_DIM, SSD_STATE), dt),
        jnp.zeros((n_even, b, CONV_W - 1, SSD_CONV_DIM), dt),
        jnp.zeros((n_even, b, GDN_HEADS, GDN_DK, GDN_DV), dt),
        jnp.zeros((n_even, b, CONV_W - 1, GDN_CONV_DIM), dt),
        jnp.zeros((n_odd, b, 0, MLA_KV_RANK), dt),
        jnp.zeros((n_odd, b, 0, MLA_ROPE), dt),
        jnp.zeros((n_odd, b, HG_HEADS, HG_DK, HG_DK), dt),
        w, wb, lbs)
    db = x_sample.shape[0]
    sample = _forward(
        x_sample, cache_mem_k.reshape(depth, db, m, xa_dim).astype(BF16),
        cache_mem_v.reshape(depth, db, m, xa_dim).astype(BF16),
        state_ssd, state_ssd_conv, state_gdn, state_gdn_conv, cache_mla_ckv, cache_mla_krope, state_hgrn,
        w, wb, lbs)
    y_prompt, p_rest = prompt[0], prompt[1:]
    y_sample, s_rest = sample[0], sample[1:]
    return (y_prompt, y_sample, *p_rest,
            p_mem_k.reshape(depth, b, m, XA_HEADS, XA_HEAD_DIM), p_mem_v.reshape(depth, b, m, XA_HEADS, XA_HEAD_DIM),
            *s_rest)
```

```python
import functools

import numpy as np
import jax
import jax.numpy as jnp
from jax import lax
from jax.experimental import pallas as pl
from jax.experimental.pallas import tpu as pltpu

F32 = jnp.float32
BF16 = jnp.bfloat16

VMEM_LIMIT_BYTES = 56 * 1024 * 1024
LANES = 128
NORM_EPS = 1e-6
CHUNK = 64
CHUNK_SHIFT = 6
assert 1 << CHUNK_SHIFT == CHUNK
CONV_W = 4
ROPE_THETA = 10000.0
MASKED = -1e30


def _params(*sem):
    return pltpu.CompilerParams(dimension_semantics=sem, vmem_limit_bytes=VMEM_LIMIT_BYTES)


def _tile(n, cap, mult):
    if n <= cap:
        return n
    for d in range(cap - cap % mult, 0, -mult):
        if n % d == 0:
            return d
    return n


def _dot(a, b):
    return jnp.dot(a, b, preferred_element_type=F32)


def _dot_nt(a, b):
    return lax.dot_general(a, b, (((1,), (1,)), ((), ())), preferred_element_type=F32)


def _dot_tn(a, b):
    return lax.dot_general(a, b, (((0,), (0,)), ((), ())), preferred_element_type=F32)


def _split3(x):
    hi = x.astype(BF16)
    r = x - hi.astype(F32)
    mid = r.astype(BF16)
    lo = (r - mid.astype(F32)).astype(BF16)
    return hi, mid, lo


def _mask_dot(mask3, x):
    return _dot(mask3, jnp.concatenate(_split3(x), axis=0))


def _mask_dot_nt(mask16, x):
    hi, mid, lo = _split3(x)
    return _dot_nt(mask16, hi) + (_dot_nt(mask16, mid) + _dot_nt(mask16, lo))


def _dot_x3(a, b):
    a_hi = a.astype(BF16)
    a_lo = (a - a_hi.astype(F32)).astype(BF16)
    b_hi = b.astype(BF16)
    b_lo = (b - b_hi.astype(F32)).astype(BF16)
    return _dot(a_hi, b_hi) + (_dot(a_hi, b_lo) + _dot(a_lo, b_hi))


def _softplus(x):
    return jnp.maximum(x, 0.0) + jnp.log1p(jnp.exp(-jnp.abs(x)))


def _sigmoid(x):
    return 0.5 * jnp.tanh(0.5 * x) + 0.5


def _silu(x):
    return x * _sigmoid(x)


def _pick_lane(a, idx):
    lane = lax.broadcasted_iota(jnp.int32, a.shape, 1)
    return jnp.sum(jnp.where(lane == idx, a, 0.0), axis=1, keepdims=True)


def _mm_body(*refs, n_ops, nks, act):
    a_refs, b_refs = refs[:n_ops], refs[n_ops:2 * n_ops]
    o_ref, acc_ref = refs[2 * n_ops], refs[2 * n_ops + 1]
    k = pl.program_id(2)
    nk = sum(nks)

    @pl.when(k == 0)
    def _():
        acc_ref[...] = jnp.zeros_like(acc_ref)

    off = 0
    for a_ref, b_ref, n in zip(a_refs, b_refs, nks):
        def step(a_ref=a_ref, b_ref=b_ref):
            acc_ref[...] += _dot(a_ref[...], b_ref[...])
        if n_ops == 1:
            step()
        else:
            pl.when((k >= off) & (k < off + n))(step)
        off += n

    @pl.when(k == nk - 1)
    def _():
        r = acc_ref[...]
        if act == "relu2":
            r = jnp.square(jnp.maximum(r, 0.0))
        o_ref[...] = r.astype(o_ref.dtype)


def _mm_full_k_body(a_ref, b_ref, o_ref, *, act):
    r = _dot(a_ref[...], b_ref[...])
    if act == "relu2":
        r = jnp.square(jnp.maximum(r, 0.0))
    o_ref[...] = r.astype(o_ref.dtype)


FULL_K_MAX = 4096
FULL_K_B_BLOCK_BYTES = 4 * 1024 * 1024


def _matmul(a_list, b_list, out_dtype, act=None, b_rows=None, tm_cap=1024, tn_cap=None, tk_cap=1024):
    m = a_list[0].shape[0]
    n = b_list[0].shape[1]
    b_rows = [0] * len(a_list) if b_rows is None else b_rows
    tm = _tile(m, tm_cap, 8)
    k0 = a_list[0].shape[1]
    if len(a_list) == 1 and k0 <= FULL_K_MAX and b_rows[0] == 0 and b_list[0].shape[0] == k0:
        cap = tn_cap or min(2048, max(512, FULL_K_B_BLOCK_BYTES // (2 * k0)))
        tn = _tile(n, cap, LANES)
        return pl.pallas_call(
            functools.partial(_mm_full_k_body, act=act),
            grid=(m // tm, n // tn),
            in_specs=[pl.BlockSpec((tm, k0), lambda i, j: (i, 0)),
                      pl.BlockSpec((k0, tn), lambda i, j: (0, j))],
            out_specs=pl.BlockSpec((tm, tn), lambda i, j: (i, j)),
            out_shape=jax.ShapeDtypeStruct((m, n), out_dtype),
            compiler_params=_params("parallel", "arbitrary"),
            name="matmul_full_k",
        )(a_list[0], b_list[0])
    tn = _tile(n, tn_cap or (2048 if len(a_list) == 1 else 1024), LANES)
    tk = min(_tile(a.shape[1], tk_cap, LANES) for a in a_list)
    nks, offs = [], []
    for a, b, r0 in zip(a_list, b_list, b_rows):
        assert a.shape[0] == m and b.shape[1] == n and a.shape[1] % tk == 0 and r0 % tk == 0
        assert r0 + a.shape[1] <= b.shape[0]
        offs.append(sum(nks))
        nks.append(a.shape[1] // tk)
    in_specs = []
    for o, nki in zip(offs, nks):
        in_specs.append(pl.BlockSpec((tm, tk), lambda i, j, k, o=o, nki=nki: (i, jnp.clip(k - o, 0, nki - 1))))
    for o, nki, r0 in zip(offs, nks, b_rows):
        in_specs.append(pl.BlockSpec(
            (tk, tn), lambda i, j, k, o=o, nki=nki, rb=r0 // tk: (rb + jnp.clip(k - o, 0, nki - 1), j)))
    return pl.pallas_call(
        functools.partial(_mm_body, n_ops=len(a_list), nks=tuple(nks), act=act),
        grid=(m // tm, n // tn, sum(nks)),
        in_specs=in_specs,
        out_specs=pl.BlockSpec((tm, tn), lambda i, j, k: (i, j)),
        out_shape=jax.ShapeDtypeStruct((m, n), out_dtype),
        scratch_shapes=[pltpu.VMEM((tm, tn), F32)],
        compiler_params=_params("parallel", "parallel", "arbitrary"),
        name="matmul",
    )(*a_list, *b_list)


def _head_mm_body(a_ref, b_ref, o_ref):
    o_ref[...] = _dot(a_ref[...], b_ref[...]).astype(o_ref.dtype)


def _head_split_body(a_ref, b_ref, tail_ref, o_ref, *, scale):
    n = b_ref.shape[1]
    o_ref[:, 0:n] = (_dot(a_ref[...].astype(BF16), b_ref[...]) * scale).astype(o_ref.dtype)
    o_ref[:, n:] = tail_ref[...]


def _head_split_matmul(a, w, scale, tail, tm_cap=1024):
    t = a.shape[0]
    nh, dk, n = w.shape
    p = tail.shape[2]
    assert n % LANES == 0 and p % LANES == 0
    tm = _tile(t, tm_cap, 8)
    return pl.pallas_call(
        functools.partial(_head_split_body, scale=scale),
        grid=(t // tm, nh),
        in_specs=[pl.BlockSpec((tm, dk), lambda i, h: (i, h)),
                  pl.BlockSpec((None, dk, n), lambda i, h: (h, 0, 0)),
                  pl.BlockSpec((None, tm, p), lambda i, h: (h, i, 0))],
        out_specs=pl.BlockSpec((None, tm, n + p), lambda i, h: (h, i, 0)),
        out_shape=jax.ShapeDtypeStruct((nh, t, n + p), BF16),
        compiler_params=_params("parallel", "arbitrary"),
        name="head_split_matmul",
    )(a, w, tail)


def _head_merge_matmul(a, w, tm_cap=1024):
    nh, t, k = a.shape
    dv = w.shape[2]
    tm = _tile(t, tm_cap, 8)
    return pl.pallas_call(
        _head_mm_body,
        grid=(t // tm, nh),
        in_specs=[pl.BlockSpec((None, tm, k), lambda i, h: (h, i, 0)),
                  pl.BlockSpec((None, k, dv), lambda i, h: (h, 0, 0))],
        out_specs=pl.BlockSpec((tm, dv), lambda i, h: (i, h)),
        out_shape=jax.ShapeDtypeStruct((t, nh * dv), BF16),
        compiler_params=_params("parallel", "arbitrary"),
        name="head_merge_matmul",
    )(a, w)


def _rms_rows(x, g):
    return x * lax.rsqrt(jnp.mean(x * x, axis=-1, keepdims=True) + NORM_EPS) * g


def _rms_body(x_ref, g_ref, o_ref):
    o_ref[...] = _rms_rows(x_ref[...], g_ref[...]).astype(o_ref.dtype)


def _rms(x, g, out_dtype, col_block=0, width=None, tr_cap=256):
    t = x.shape[0]
    width = x.shape[1] if width is None else width
    tr = _tile(t, tr_cap, 8)
    return pl.pallas_call(
        _rms_body,
        grid=(t // tr,),
        in_specs=[pl.BlockSpec((tr, width), lambda i: (i, col_block)),
                  pl.BlockSpec((1, width), lambda i: (0, 0))],
        out_specs=pl.BlockSpec((tr, width), lambda i: (i, 0)),
        out_shape=jax.ShapeDtypeStruct((t, width), out_dtype),
        compiler_params=_params("parallel"),
        name="rms",
    )(x, g.reshape(1, width))


def _add_rms_body(x_ref, y_ref, g1_ref, g2_ref, xo_ref, ho_ref):
    xn = x_ref[...] + _rms_rows(y_ref[...], g1_ref[...])
    xo_ref[...] = xn
    ho_ref[...] = _rms_rows(xn, g2_ref[...]).astype(ho_ref.dtype)


def _add_rms_last_body(x_ref, y_ref, g1_ref, xo_ref):
    xo_ref[...] = x_ref[...] + _rms_rows(y_ref[...], g1_ref[...])


def _add_rms(x, y, g1, g2, tr_cap=256):
    t, d = x.shape
    tr = _tile(t, tr_cap, 8)
    row = pl.BlockSpec((tr, d), lambda i: (i, 0))
    gain = pl.BlockSpec((1, d), lambda i: (0, 0))
    if g2 is None:
        out = pl.pallas_call(
            _add_rms_last_body, grid=(t // tr,), in_specs=[row, row, gain], out_specs=row,
            out_shape=jax.ShapeDtypeStruct((t, d), F32), compiler_params=_params("parallel"),
            name="add_rms_last",
        )(x, y, g1.reshape(1, d))
        return out, None
    return pl.pallas_call(
        _add_rms_body, grid=(t // tr,), in_specs=[row, row, gain, gain], out_specs=[row, row],
        out_shape=[jax.ShapeDtypeStruct((t, d), F32), jax.ShapeDtypeStruct((t, d), BF16)],
        compiler_params=_params("parallel"), name="add_rms",
    )(x, y, g1.reshape(1, d), g2.reshape(1, d))


CONV_HALO = 8


def _conv_taps(x, w, b):
    acc = b + x * w[CONV_W - 1:CONV_W, :]
    for j in range(1, CONV_W):
        acc = acc + pltpu.roll(x, j, 0) * w[CONV_W - 1 - j:CONV_W - j, :]
    return acc


def _conv_body(u_ref, buf_ref, w_ref, b_ref, o_ref, hist_ref, *, tl):
    lt = pl.program_id(2)

    @pl.when(lt == 0)
    def _():
        hist_ref[...] = jnp.zeros_like(hist_ref)
        hist_ref[CONV_HALO - (CONV_W - 1):CONV_HALO, :] = buf_ref[0]

    u = u_ref[...]
    w = w_ref[...]
    b = b_ref[...]
    o_ref[...] = _silu(_conv_taps(u, w, b))
    head = jnp.concatenate([hist_ref[...], u[0:CONV_HALO, :]], axis=0)
    o_ref[0:CONV_HALO, :] = _silu(_conv_taps(head, w, b)[CONV_HALO:, :])
    hist_ref[...] = u[tl - CONV_HALO:tl, :]


def _conv_silu(u, col0, c, buf, w, b, bsz, l, tl_cap=512, tc=1024):
    assert l >= CONV_W - 1 and col0 % tc == 0 and c % tc == 0
    tl = _tile(l, tl_cap, 8)
    assert tl >= CONV_HALO
    nl = l // tl
    cb0 = col0 // tc
    return pl.pallas_call(
        functools.partial(_conv_body, tl=tl),
        grid=(bsz, c // tc, nl),
        in_specs=[pl.BlockSpec((tl, tc), lambda bi, ci, li: (bi * nl + li, cb0 + ci)),
                  pl.BlockSpec((1, CONV_W - 1, tc), lambda bi, ci, li: (bi, 0, ci)),
                  pl.BlockSpec((CONV_W, tc), lambda bi, ci, li: (0, ci)),
                  pl.BlockSpec((1, tc), lambda bi, ci, li: (0, ci))],
        out_specs=pl.BlockSpec((tl, tc), lambda bi, ci, li: (bi * nl + li, ci)),
        out_shape=jax.ShapeDtypeStruct((bsz * l, c), F32),
        scratch_shapes=[pltpu.VMEM((CONV_HALO, tc), F32)],
        compiler_params=_params("parallel", "parallel", "arbitrary"),
        name="conv_silu",
    )(u, buf, w, b.reshape(1, c))


def _tri_masks(n):
    r = np.arange(n)
    tril = (r[:, None] >= r[None, :]).astype(np.float32)
    return (jnp.asarray(tril, dtype=BF16), jnp.asarray(np.tile(tril, (1, 3)), dtype=BF16),
            jnp.asarray(np.eye(n, dtype=np.float32), dtype=BF16))


def _ssd_body(x_ref, bm_ref, cm_ref, z_ref, sm_ref, dtb_ref, alog_ref, d_ref, gn_ref, s0_ref, tril_ref, tril3_ref,
              eye_ref, y_ref, so_ref, s_scr, cst_scr, *, nc, hpg, hd, nheads):
    g = pl.program_id(1)
    c = pl.program_id(2)
    lc = x_ref.shape[0]

    @pl.when(c == 0)
    def _():
        s_scr[...] = s0_ref[0]

    dt = _softplus(sm_ref[:, 0:nheads] + dtb_ref[...])
    da = dt * (-jnp.exp(alog_ref[...]))
    cs = _mask_dot(tril3_ref[...], da)
    cst_scr[...] = _mask_dot_nt(eye_ref[...], cs)
    tri = tril_ref[...].astype(F32) > 0.0

    x = x_ref[...]
    bm = bm_ref[...].astype(BF16)
    cm = cm_ref[...].astype(BF16)
    s = s_scr[...]
    cb = _dot_nt(cm, bm)
    y_state = _dot_nt(cm, s.astype(BF16))
    hd_ids = [g * hpg + j for j in range(hpg)]
    js = range(hpg)
    cs_col = [_pick_lane(cs, h) for h in hd_ids]
    dt_col = [_pick_lane(dt, h) for h in hd_ids]
    d_h = [_pick_lane(d_ref[...], h) for h in hd_ids]
    cs_row = [cst_scr[pl.ds(h, 1), :] for h in hd_ids]
    cs_last = [cs_row[j][:, lc - 1:lc] for j in js]
    mix = [(cb * jnp.exp(jnp.where(tri, cs_col[j] - cs_row[j], MASKED))).astype(BF16) for j in js]
    xh = [x[:, j * hd:(j + 1) * hd] for j in js]
    xdt = [xh[j] * dt_col[j] for j in js]
    y_in = [_dot(mix[j], xdt[j].astype(BF16)) for j in js]
    ys = [y_in[j] + y_state[:, j * hd:(j + 1) * hd] * jnp.exp(cs_col[j]) + d_h[j] * xh[j] for j in js]
    xws = [xdt[j] * jnp.exp(cs_last[j] - cs_col[j]) for j in js]
    decs = [jnp.broadcast_to(jnp.exp(cs_last[j]), (hd, s.shape[1])) for j in js]
    y = jnp.concatenate(ys, axis=1)
    xw = jnp.concatenate(xws, axis=1)
    s_new = s * jnp.concatenate(decs, axis=0) + _dot_tn(xw.astype(BF16), bm)
    s_scr[...] = s_new

    @pl.when(c == nc - 1)
    def _():
        so_ref[0] = s_new

    y = y * _silu(z_ref[...])
    y_ref[...] = _rms_rows(y, gn_ref[...]).astype(y_ref.dtype)


def _ssd(xbc, pm, z_col0, sm, dt_bias, a_log, d, gn, s0, bsz, l, nheads, hd, ngroups, nstate):
    t = bsz * l
    lc = CHUNK
    nc = l // lc
    hpg = nheads // ngroups
    gw = hpg * hd
    inner = nheads * hd
    assert l % lc == 0 and nheads == lc and z_col0 % gw == 0
    tril, tril3, eye = _tri_masks(lc)
    rows = lambda bi, gi, ci: bi * nc + ci
    const2 = lambda bi, gi, ci: (0, 0)
    y, s_new = pl.pallas_call(
        functools.partial(_ssd_body, nc=nc, hpg=hpg, hd=hd, nheads=nheads),
        grid=(bsz, ngroups, nc),
        in_specs=[pl.BlockSpec((lc, gw), lambda bi, gi, ci: (rows(bi, gi, ci), gi)),
                  pl.BlockSpec((lc, nstate), lambda bi, gi, ci: (rows(bi, gi, ci), inner // nstate + gi)),
                  pl.BlockSpec((lc, nstate), lambda bi, gi, ci: (rows(bi, gi, ci), inner // nstate + ngroups + gi)),
                  pl.BlockSpec((lc, gw), lambda bi, gi, ci: (rows(bi, gi, ci), z_col0 // gw + gi)),
                  pl.BlockSpec((lc, sm.shape[1]), lambda bi, gi, ci: (rows(bi, gi, ci), 0)),
                  pl.BlockSpec((1, nheads), const2),
                  pl.BlockSpec((1, nheads), const2),
                  pl.BlockSpec((1, nheads), const2),
                  pl.BlockSpec((1, gw), lambda bi, gi, ci: (0, gi)),
                  pl.BlockSpec((1, gw, nstate), lambda bi, gi, ci: (bi, gi, 0)),
                  pl.BlockSpec((lc, lc), const2),
                  pl.BlockSpec((lc, 3 * lc), const2),
                  pl.BlockSpec((lc, lc), const2)],
        out_specs=[pl.BlockSpec((lc, gw), lambda bi, gi, ci: (rows(bi, gi, ci), gi)),
                   pl.BlockSpec((1, gw, nstate), lambda bi, gi, ci: (bi, gi, 0))],
        out_shape=[jax.ShapeDtypeStruct((t, inner), BF16),
                   jax.ShapeDtypeStruct((bsz, inner, nstate), F32)],
        scratch_shapes=[pltpu.VMEM((gw, nstate), F32), pltpu.VMEM((nheads, lc), F32)],
        compiler_params=_params("parallel", "parallel", "arbitrary"),
        name="ssd_scan",
    )(xbc, xbc, xbc, pm, sm, dt_bias.reshape(1, nheads), a_log.reshape(1, nheads), d.reshape(1, nheads),
      gn.reshape(1, inner), s0, tril, tril3, eye)
    return y, s_new


def _gdn_heads(qs, ks, vs, gates, betas, cs_cols, cs_rows, states, tri, eye, gn):
    n = len(qs)
    hd = range(n)
    lc, dk = qs[0].shape
    dv = vs[0].shape[1]
    cs_last = [cs_rows[i][:, lc - 1:lc] for i in hd]
    decay = [jnp.exp(jnp.where(tri, cs_cols[i] - cs_rows[i], MASKED)) for i in hd]
    qn = [qs[i] * (lax.rsqrt(jnp.sum(qs[i] * qs[i], axis=1, keepdims=True) + 1e-6) * dk ** -0.5) for i in hd]
    kn = [ks[i] * lax.rsqrt(jnp.sum(ks[i] * ks[i], axis=1, keepdims=True) + 1e-6) for i in hd]
    kb = [kn[i] * betas[i] for i in hd]
    kn16 = [kn[i].astype(BF16) for i in hd]
    off_diag = 1.0 - eye
    a = [_dot_nt(kb[i].astype(BF16), kn16[i]) * decay[i] * off_diag for i in hd]
    qk = [_dot_nt(qn[i].astype(BF16), kn16[i]) * decay[i] for i in hd]
    s16 = [states[i].astype(BF16) for i in hd]
    o_state = [_dot((qn[i] * jnp.exp(cs_cols[i])).astype(BF16), s16[i]) for i in hd]
    tinv = [eye - a[i] for i in hd]
    p = [_dot_x3(a[i], a[i]) for i in hd]
    steps = max(1, int(np.ceil(np.log2(lc))) - 1)
    for st in range(steps):
        tinv = [tinv[i] + _dot_x3(tinv[i], p[i]) for i in hd]
        if st + 1 < steps:
            p = [_dot_x3(p[i], p[i]) for i in hd]
    rhs = [jnp.concatenate([vs[i] * betas[i], kb[i] * jnp.exp(cs_cols[i])], axis=1).astype(BF16) for i in hd]
    sol = [_dot(tinv[i].astype(BF16), rhs[i]) for i in hd]
    vn16 = [(sol[i][:, :dv] - _dot(sol[i][:, dv:].astype(BF16), s16[i])).astype(BF16) for i in hd]
    o = [o_state[i] + _dot(qk[i].astype(BF16), vn16[i]) for i in hd]
    k_end = [(kn[i] * jnp.exp(cs_last[i] - cs_cols[i])).astype(BF16) for i in hd]
    s_new = [states[i] * jnp.exp(cs_last[i]) + _dot_tn(k_end[i], vn16[i]) for i in hd]
    outs = [_rms_rows(o[i], gn) * _silu(gates[i]) for i in hd]
    return outs, s_new


def _gdn_body(q_ref, k_ref, v_ref, gate_ref, sm_ref, alog_ref, dtb_ref, gn_ref, s0_ref, tril_ref, tril3_ref, eye_ref,
              eyeh_ref, o_ref, so_ref, s_scr, cst_scr, *, nc, nheads, a_col0, hps):
    hb = pl.program_id(1)
    c = pl.program_id(2)
    d = q_ref.shape[1] // hps

    @pl.when(c == 0)
    def _():
        s_scr[...] = s0_ref[0]

    a_raw = sm_ref[:, a_col0:a_col0 + nheads]
    b_raw = sm_ref[:, a_col0 + nheads:a_col0 + 2 * nheads]
    gl = -jnp.exp(alog_ref[...]) * _softplus(a_raw + dtb_ref[...])
    beta_all = _sigmoid(b_raw)
    cs = _mask_dot(tril3_ref[...], gl)
    cst_scr[...] = _mask_dot_nt(eyeh_ref[...], cs)
    tri = tril_ref[...].astype(F32) > 0.0
    eye = eye_ref[...].astype(F32)
    gn = gn_ref[...]
    sls = [slice(j * d, (j + 1) * d) for j in range(hps)]
    hidx = [hb * hps + j for j in range(hps)]
    outs, s_new = _gdn_heads(
        [q_ref[:, sl] for sl in sls], [k_ref[:, sl] for sl in sls], [v_ref[:, sl] for sl in sls],
        [gate_ref[:, sl] for sl in sls], [_pick_lane(beta_all, h) for h in hidx], [_pick_lane(cs, h) for h in hidx],
        [cst_scr[pl.ds(h, 1), :] for h in hidx], [s_scr[j] for j in range(hps)], tri, eye, gn)
    for j in range(hps):
        s_scr[j] = s_new[j]
        o_ref[:, sls[j]] = outs[j].astype(o_ref.dtype)

    @pl.when(c == nc - 1)
    def _():
        so_ref[0] = s_scr[...]


GDN_HEADS_PER_STEP = 8


def _gdn(qkv, pm, gate_col0, sm, a_col0, a_log, dt_bias, gn, s0, bsz, l, nheads, dk, dv):
    hps = GDN_HEADS_PER_STEP
    w = hps * dv
    assert dk == dv and gate_col0 % w == 0 and nheads % hps == 0
    t = bsz * l
    lc = CHUNK
    nc = l // lc
    nhb = nheads // hps
    tril, tril3, eye = _tri_masks(lc)
    eyeh = jnp.asarray(np.eye(nheads, dtype=np.float32), dtype=BF16)
    rows = lambda bi, hi, ci: bi * nc + ci
    const2 = lambda bi, hi, ci: (0, 0)
    col = lambda k: pl.BlockSpec((lc, w), lambda bi, hi, ci: (rows(bi, hi, ci), k * nhb + hi))
    state = pl.BlockSpec((1, hps, dk, dv), lambda bi, hi, ci: (bi, hi, 0, 0))
    o, s_new = pl.pallas_call(
        functools.partial(_gdn_body, nc=nc, nheads=nheads, a_col0=a_col0, hps=hps),
        grid=(bsz, nhb, nc),
        in_specs=[col(0), col(1), col(2),
                  pl.BlockSpec((lc, w), lambda bi, hi, ci: (rows(bi, hi, ci), gate_col0 // w + hi)),
                  pl.BlockSpec((lc, sm.shape[1]), lambda bi, hi, ci: (rows(bi, hi, ci), 0)),
                  pl.BlockSpec((1, nheads), const2),
                  pl.BlockSpec((1, nheads), const2),
                  pl.BlockSpec((1, dv), const2),
                  state,
                  pl.BlockSpec((lc, lc), const2),
                  pl.BlockSpec((lc, 3 * lc), const2),
                  pl.BlockSpec((lc, lc), const2),
                  pl.BlockSpec((nheads, nheads), const2)],
        out_specs=[pl.BlockSpec((lc, w), lambda bi, hi, ci: (rows(bi, hi, ci), hi)), state],
        out_shape=[jax.ShapeDtypeStruct((t, nheads * dv), BF16),
                   jax.ShapeDtypeStruct((bsz, nheads, dk, dv), F32)],
        scratch_shapes=[pltpu.VMEM((hps, dk, dv), F32), pltpu.VMEM((nheads, lc), F32)],
        compiler_params=_params("parallel", "parallel", "arbitrary"),
        name="gdn_scan",
    )(qkv, qkv, qkv, pm, sm, a_log.reshape(1, nheads), dt_bias.reshape(1, nheads), gn.reshape(1, dv), s0,
      tril, tril3, eye, eyeh)
    return o, s_new


def _hgrn_levels(lc):
    hs = []
    h = lc // 2
    while h >= 1:
        hs.append(h)
        h //= 2
    return hs


def _hgrn_masks(lc):
    r = np.arange(lc)
    t, u = r[:, None], r[None, :]
    mats = [u <= t, u > t]
    for h in _hgrn_levels(lc):
        start = (t // h) * h
        mats.append((u >= start) & (u <= t))
        mats.append((u > t) & (u <= start + h - 1))
    return jnp.asarray(np.tile(np.concatenate(mats, axis=0).astype(np.float32), (1, 3)), dtype=BF16)


def _hgrn_heads(qs, hfs, vs, gates, lbs, gn, sts, mst16):
    n = len(qs)
    hd = range(n)
    lc, d = qs[0].shape
    logf = [jnp.log(lbs[i] + (1.0 - lbs[i]) * _sigmoid(hfs[i])) for i in hd]
    kk = [(1.0 - lbs[i]) * _sigmoid(-hfs[i]) for i in hd]
    cums = [_mask_dot(mst16, logf[i]) for i in hd]
    v16 = [vs[i].astype(BF16) for i in hd]
    st16 = [sts[i].astype(BF16) for i in hd]
    o_state = [_dot_nt((qs[i] * jnp.exp(cums[i][0:lc])).astype(BF16), st16[i]) for i in hd]

    row = lax.broadcasted_iota(jnp.int32, (lc, 1), 0)
    ti = lax.broadcasted_iota(jnp.int32, (lc, lc), 0)
    si = lax.broadcasted_iota(jnp.int32, (lc, lc), 1)
    att = [jnp.where(ti == si, jnp.sum(qs[i] * kk[i], axis=1, keepdims=True), 0.0) for i in hd]
    for li, hs in enumerate(_hgrn_levels(lc)):
        sh = int(np.log2(hs))
        upper = ((row >> sh) & 1) == 1
        same_block = (ti >> (sh + 1)) == (si >> (sh + 1))
        for i in hd:
            seg = cums[i][(2 + 2 * li) * lc:(3 + 2 * li) * lc]
            rest = cums[i][(3 + 2 * li) * lc:(4 + 2 * li) * lc]
            ql = jnp.where(upper, qs[i] * jnp.exp(seg), 0.0)
            kl = jnp.where(upper, 0.0, kk[i] * jnp.exp(rest))
            att[i] = att[i] + jnp.where(same_block, _dot_nt(ql.astype(BF16), kl.astype(BF16)), 0.0)

    outs, st_new = [], []
    for i in hd:
        o = o_state[i] + _dot(att[i].astype(BF16), v16[i])
        k_end = kk[i] * jnp.exp(cums[i][lc:2 * lc])
        st_new.append(sts[i] * jnp.exp(cums[i][lc - 1:lc, :]) + _dot_tn(v16[i], k_end.astype(BF16)))
        outs.append(_rms_rows(o, gn) * _silu(gates[i]))
    return outs, st_new


def _hgrn_body(q_ref, f_ref, i_ref, gate_ref, lb_ref, gn_ref, s0_ref, mst_ref, o_ref, so_ref, st_scr, *, nc, hps):
    c = pl.program_id(2)
    d = q_ref.shape[1] // hps

    @pl.when(c == 0)
    def _():
        for j in range(hps):
            st_scr[j] = s0_ref[0, j].T

    mst16 = mst_ref[...]
    gn = gn_ref[...]
    sls = [slice(j * d, (j + 1) * d) for j in range(hps)]
    outs, st_new = _hgrn_heads(
        [q_ref[:, sl] for sl in sls], [f_ref[:, sl] for sl in sls], [i_ref[:, sl] for sl in sls],
        [gate_ref[:, sl] for sl in sls], [lb_ref[:, sl] for sl in sls], gn, [st_scr[j] for j in range(hps)], mst16)
    for j in range(hps):
        st_scr[j] = st_new[j]
        o_ref[:, sls[j]] = outs[j].astype(o_ref.dtype)

    @pl.when(c == nc - 1)
    def _():
        for j in range(hps):
            so_ref[0, j] = st_scr[j].T


HGRN_HEADS_PER_STEP = 8


def _hgrn(pm, lb, gn, s0, bsz, l, nheads, dk):
    hps = HGRN_HEADS_PER_STEP
    assert nheads % hps == 0
    w = hps * dk
    nhb = nheads // hps
    t = bsz * l
    lc = CHUNK
    nc = l // lc
    mst = _hgrn_masks(lc)
    rows = lambda bi, hi, ci: bi * nc + ci
    const2 = lambda bi, hi, ci: (0, 0)
    col = lambda k: pl.BlockSpec((lc, w), lambda bi, hi, ci: (rows(bi, hi, ci), k * nhb + hi))
    state = pl.BlockSpec((1, hps, dk, dk), lambda bi, hi, ci: (bi, hi, 0, 0))
    o, s_new = pl.pallas_call(
        functools.partial(_hgrn_body, nc=nc, hps=hps),
        grid=(bsz, nhb, nc),
        in_specs=[col(0), col(1), col(2), col(3),
                  pl.BlockSpec((1, w), lambda bi, hi, ci: (0, hi)),
                  pl.BlockSpec((1, dk), const2),
                  state,
                  pl.BlockSpec(mst.shape, const2)],
        out_specs=[pl.BlockSpec((lc, w), lambda bi, hi, ci: (rows(bi, hi, ci), hi)), state],
        out_shape=[jax.ShapeDtypeStruct((t, nheads * dk), BF16),
                   jax.ShapeDtypeStruct((bsz, nheads, dk, dk), F32)],
        scratch_shapes=[pltpu.VMEM((hps, dk, dk), F32)],
        compiler_params=_params("parallel", "parallel", "arbitrary"),
        name="hgrn_scan",
    )(pm, pm, pm, pm, lb.reshape(1, nheads * dk), gn.reshape(1, dk), s0, mst)
    return o, s_new


MLA_SUB_ROWS = 512


def _mla_body(qi_ref, ki_ref, last_ref, q_ref, k_ref, o_ref, m_scr, l_scr, acc_scr, *, tq, tk, t_past, hs):
    step = pl.program_id(1)
    iq = qi_ref[step]
    ikv = ki_ref[step]
    nh, _, dq = q_ref.shape
    r = o_ref.shape[2]
    rows = hs * tq
    first_limit = t_past + iq * tq + CHUNK

    @pl.when(ikv == 0)
    def _():
        m_scr[...] = jnp.full_like(m_scr, MASKED)
        l_scr[...] = jnp.zeros_like(l_scr)
        acc_scr[...] = jnp.zeros_like(acc_scr)

    def update(masked):
        kc = k_ref[0]
        vals = kc[:, 0:r]
        if masked:
            tok = lax.broadcasted_iota(jnp.int32, (rows, 1), 0) & (tq - 1)
            limit = t_past + iq * tq + ((tok >> CHUNK_SHIFT) + 1) * CHUNK
            visible = (ikv * tk + lax.broadcasted_iota(jnp.int32, (1, tk), 1)) < limit

        def scores(g):
            return _dot_nt(q_ref[g * hs:(g + 1) * hs].reshape(rows, dq), kc)

        ngroups = nh // hs
        s_next = scores(0)
        for g in range(ngroups):
            rsl = slice(g * rows, (g + 1) * rows)
            s = s_next
            if g + 1 < ngroups:
                s_next = scores(g + 1)
            if masked:
                s = jnp.where(visible, s, MASKED)
            m_old = m_scr[rsl]
            m_new = jnp.maximum(m_old, jnp.max(s, axis=1, keepdims=True))
            alpha = jnp.exp(m_old - m_new)
            p = jnp.exp(s - m_new)
            l_scr[rsl] = alpha * l_scr[rsl] + jnp.sum(p, axis=1, keepdims=True)
            acc_scr[rsl] = alpha * acc_scr[rsl] + _dot(p.astype(BF16), vals)
            m_scr[rsl] = m_new

    fully_visible = (ikv + 1) * tk <= first_limit
    pl.when(fully_visible)(lambda: update(False))
    pl.when(jnp.logical_not(fully_visible))(lambda: update(True))

    @pl.when(last_ref[step] == 1)
    def _():
        o = acc_scr[...] / l_scr[...]
        o_ref[...] = o.reshape(nh, tq, r).astype(o_ref.dtype)


def _mla_attention(q, k, r, bsz, l, t_past, tq, tk):
    nh, t, dq = q.shape
    tkeys = k.shape[1]
    assert l % tq == 0 and tq % CHUNK == 0 and t_past % CHUNK == 0 and tkeys % tk == 0 and tkeys >= t_past + l
    assert tq & (tq - 1) == 0 and k.shape[2] == dq and r % LANES == 0
    nq = l // tq
    hs = min(nh, max(1, MLA_SUB_ROWS // tq))
    assert nh % hs == 0
    qi, ki, last = [], [], []
    for qb in range(nq):
        need = -(-(t_past + (qb + 1) * tq) // tk)
        qi += [qb] * need
        ki += list(range(need))
        last += [0] * (need - 1) + [1]
    tables = [jnp.asarray(np.asarray(a, np.int32)) for a in (qi, ki, last)]
    qmap = lambda bi, si, qi_ref, ki_ref, last_ref: (0, bi * nq + qi_ref[si], 0)
    kmap = lambda bi, si, qi_ref, ki_ref, last_ref: (bi, ki_ref[si], 0)
    return pl.pallas_call(
        functools.partial(_mla_body, tq=tq, tk=tk, t_past=t_past, hs=hs),
        grid_spec=pltpu.PrefetchScalarGridSpec(
            num_scalar_prefetch=3,
            grid=(bsz, len(qi)),
            in_specs=[pl.BlockSpec((nh, tq, dq), qmap),
                      pl.BlockSpec((1, tk, dq), kmap)],
            out_specs=pl.BlockSpec((nh, tq, r), qmap),
            scratch_shapes=[pltpu.VMEM((nh * tq, 1), F32), pltpu.VMEM((nh * tq, 1), F32),
                            pltpu.VMEM((nh * tq, r), F32)]),
        out_shape=jax.ShapeDtypeStruct((nh, t, r), BF16),
        compiler_params=_params("parallel", "arbitrary"),
        name="mla_attention",
    )(*tables, q, k)


def _xattn_body(q_ref, k_ref, v_ref, o_ref, *, nheads, hd):
    q = q_ref[...]
    k = k_ref[0]
    v = v_ref[0]
    outs = []
    for h in range(nheads):
        sl = slice(h * hd, (h + 1) * hd)
        s = _dot_nt(q[:, sl], k[:, sl]) * hd ** -0.5
        s = s - jnp.max(s, axis=1, keepdims=True)
        p = jnp.exp(s)
        p = p / jnp.sum(p, axis=1, keepdims=True)
        outs.append(_dot(p.astype(BF16), v[:, sl]))
    o_ref[...] = jnp.concatenate(outs, axis=1).astype(o_ref.dtype)


def _cross_attention(q, mem_k, mem_v, bsz, l, nheads, hd, tl_cap=512):
    t, d = q.shape
    m = mem_k.shape[1]
    tl = _tile(l, tl_cap, 8)
    nl = l // tl
    return pl.pallas_call(
        functools.partial(_xattn_body, nheads=nheads, hd=hd),
        grid=(bsz, nl),
        in_specs=[pl.BlockSpec((tl, d), lambda bi, li: (bi * nl + li, 0)),
                  pl.BlockSpec((1, m, d), lambda bi, li: (bi, 0, 0)),
                  pl.BlockSpec((1, m, d), lambda bi, li: (bi, 0, 0))],
        out_specs=pl.BlockSpec((tl, d), lambda bi, li: (bi * nl + li, 0)),
        out_shape=jax.ShapeDtypeStruct((t, d), BF16),
        compiler_params=_params("parallel", "parallel"),
        name="cross_attention",
    )(q, mem_k, mem_v)


SSD_HEADS, SSD_HEAD_DIM, SSD_GROUPS, SSD_STATE = 64, 64, 8, 128
SSD_INNER = SSD_HEADS * SSD_HEAD_DIM
SSD_CONV_DIM = SSD_INNER + 2 * SSD_GROUPS * SSD_STATE
GDN_HEADS, GDN_DK, GDN_DV = 32, 128, 128
GDN_CONV_DIM = 2 * GDN_HEADS * GDN_DK + GDN_HEADS * GDN_DV
MLA_HEADS, MLA_Q_RANK, MLA_KV_RANK, MLA_NOPE, MLA_ROPE, MLA_V = 32, 1024, 512, 128, 64, 128
HG_HEADS, HG_DK = 32, 128
HG_W = HG_HEADS * HG_DK
XA_HEADS, XA_HEAD_DIM = 4, 128
MLA_TK = 1024
MLA_TQ = 128


def _rope(x, pos):
    half = x.shape[-1] // 2
    inv = ROPE_THETA ** (-jnp.arange(half, dtype=F32) / half)
    ang = pos[:, None] * inv[None, :]
    ang = ang.reshape((ang.shape[0],) + (1,) * (x.ndim - 3) + (half,))
    cos, sin = jnp.cos(ang), jnp.sin(ang)
    x1, x2 = x[..., :half], x[..., half:]
    return jnp.concatenate([x1 * cos - x2 * sin, x2 * cos + x1 * sin], axis=-1)


def _prep_weights(w):
    out = {}
    e_sizes = [SSD_INNER, SSD_CONV_DIM, SSD_HEADS, GDN_CONV_DIM, GDN_HEADS, GDN_HEADS, GDN_HEADS * GDN_DV]
    e_off = np.concatenate([[0], np.cumsum(e_sizes)])
    wi = w["w_in_even"]
    sec = lambda a, offs, i: a[:, :, offs[i]:offs[i + 1]]
    out["w_in_even_main"] = jnp.concatenate(
        [sec(wi, e_off, 0), sec(wi, e_off, 1), sec(wi, e_off, 3), sec(wi, e_off, 6)], axis=-1).astype(BF16)
    out["w_in_even_small"] = jnp.concatenate(
        [sec(wi, e_off, 2), sec(wi, e_off, 4), sec(wi, e_off, 5)], axis=-1).astype(BF16)
    o_sizes = [MLA_Q_RANK, MLA_KV_RANK, MLA_ROPE, HG_W, HG_W, HG_W, HG_W]
    o_off = np.concatenate([[0], np.cumsum(o_sizes)])
    wo = w["w_in_odd"]
    out["w_in_odd_main"] = jnp.concatenate(
        [sec(wo, o_off, 3), sec(wo, o_off, 4), sec(wo, o_off, 5), sec(wo, o_off, 6), sec(wo, o_off, 0)],
        axis=-1).astype(BF16)
    out["w_in_odd_small"] = jnp.concatenate([sec(wo, o_off, 1), sec(wo, o_off, 2)], axis=-1).astype(BF16)
    n_odd = wo.shape[0]
    uq = w["mla_w_uq"].reshape(n_odd, MLA_Q_RANK, MLA_HEADS, MLA_NOPE + MLA_ROPE)
    out["w_uq"] = jnp.concatenate(
        [uq[..., :MLA_NOPE].reshape(n_odd, MLA_Q_RANK, MLA_HEADS * MLA_NOPE),
         uq[..., MLA_NOPE:].reshape(n_odd, MLA_Q_RANK, MLA_HEADS * MLA_ROPE)], axis=-1).astype(BF16)
    out["w_uk"] = jnp.transpose(w["mla_w_uk"], (0, 2, 3, 1)).astype(BF16)
    out["w_uv"] = jnp.transpose(w["mla_w_uv"], (0, 2, 1, 3)).astype(BF16)
    for name in ("w_out_even", "w_out_odd", "xa_w_q", "xa_w_k", "xa_w_v", "xa_w_o", "ffn_w1", "ffn_w2"):
        out[name] = w[name].astype(BF16)
    return out


def _even_mixer(hn, w, wb, e, bsz, l, ssd_s, ssd_buf, gdn_s, gdn_buf):
    pm = _matmul([hn], [wb["w_in_even_main"][e]], F32)
    sm = _matmul([hn], [wb["w_in_even_small"][e]], F32)
    xbc_col0, qkv_col0 = SSD_INNER, SSD_INNER + SSD_CONV_DIM
    gate_col0 = qkv_col0 + GDN_CONV_DIM
    xbc = _conv_silu(pm, xbc_col0, SSD_CONV_DIM, ssd_buf, w["ssd_conv_w"][e], w["ssd_conv_b"][e], bsz, l)
    qkv = _conv_silu(pm, qkv_col0, GDN_CONV_DIM, gdn_buf, w["gdn_conv_w"][e],
                     jnp.zeros((GDN_CONV_DIM,), F32), bsz, l)
    y, ssd_new = _ssd(xbc, pm, 0, sm, w["ssd_dt_bias"][e], w["ssd_a_log"][e], w["ssd_d"][e], w["ssd_norm"][e],
                      ssd_s.reshape(bsz, SSD_INNER, SSD_STATE), bsz, l, SSD_HEADS, SSD_HEAD_DIM, SSD_GROUPS,
                      SSD_STATE)
    o, gdn_new = _gdn(qkv, pm, gate_col0, sm, SSD_HEADS, w["gdn_a_log"][e], w["gdn_dt_bias"][e], w["gdn_norm"][e],
                      gdn_s, bsz, l, GDN_HEADS, GDN_DK, GDN_DV)
    w_out = wb["w_out_even"][e]
    mix = _matmul([y, o], [w_out, w_out], F32, b_rows=[0, SSD_INNER])
    pm3 = pm.reshape(bsz, l, pm.shape[1])
    tail = pm3[:, l - (CONV_W - 1):]
    states = (ssd_new.reshape(bsz, SSD_HEADS, SSD_HEAD_DIM, SSD_STATE), tail[:, :, xbc_col0:qkv_col0],
              gdn_new, tail[:, :, qkv_col0:gate_col0])
    return mix, states


def _odd_mixer(hn, w, wb, j, lb, bsz, l, ckv_past, kr_past, hg_s):
    t_past = ckv_past.shape[1]
    pm = _matmul([hn], [wb["w_in_odd_main"][j]], F32)
    sm = _matmul([hn], [wb["w_in_odd_small"][j]], F32)
    posf = (t_past + jnp.arange(l, dtype=jnp.int32)).astype(F32)
    scale = (MLA_NOPE + MLA_ROPE) ** -0.5
    cqn = _rms(pm, w["mla_q_norm"][j], BF16, col_block=4 * HG_W // MLA_Q_RANK, width=MLA_Q_RANK)
    q = _matmul([cqn], [wb["w_uq"][j]], F32)
    nope_w = MLA_HEADS * MLA_NOPE
    lane_pad = (-MLA_ROPE) % LANES
    q_rope = _rope(q[:, nope_w:].reshape(bsz, l, MLA_HEADS, MLA_ROPE), posf) * scale
    q_rope = jnp.transpose(q_rope.reshape(bsz * l, MLA_HEADS, MLA_ROPE), (1, 0, 2)).astype(BF16)
    q_rope = jnp.pad(q_rope, ((0, 0), (0, 0), (0, lane_pad)))
    q_att = _head_split_matmul(q, wb["w_uk"][j], scale, q_rope)
    c_kv = _rms(sm, w["mla_kv_norm"][j], F32, col_block=0, width=MLA_KV_RANK).reshape(bsz, l, MLA_KV_RANK)
    k_rope = _rope(sm[:, MLA_KV_RANK:].reshape(bsz, l, MLA_ROPE), posf)
    tkeys = t_past + l
    pad = (-tkeys) % MLA_TK
    k_att = jnp.concatenate([jnp.concatenate([ckv_past, c_kv], axis=1).astype(BF16),
                             jnp.concatenate([kr_past, k_rope], axis=1).astype(BF16)], axis=2)
    k_att = jnp.pad(k_att, ((0, 0), (0, pad), (0, lane_pad)))
    tq = MLA_TQ if l % MLA_TQ == 0 else CHUNK
    o_lat = _mla_attention(q_att, k_att, MLA_KV_RANK, bsz, l, t_past, tq, MLA_TK)
    o_mla = _head_merge_matmul(o_lat, wb["w_uv"][j])
    o_hg, hg_new = _hgrn(pm, lb, w["hg_norm"][j], hg_s, bsz, l, HG_HEADS, HG_DK)
    w_out = wb["w_out_odd"][j]
    mix = _matmul([o_mla, o_hg], [w_out, w_out], F32, b_rows=[0, MLA_HEADS * MLA_V])
    return mix, (c_kv, k_rope, hg_new)


def _forward(x, mem_k, mem_v, ssd_s, ssd_buf, gdn_s, gdn_buf, ckv_past, kr_past, hg_s, w, wb, lbs):
    bsz, l, d = x.shape
    depth = w["norm_g"].shape[0]
    x = x.reshape(bsz * l, d)
    hn = _rms(x, w["norm_g"][0, 0], BF16)
    ev, od = [], []
    for layer in range(depth):
        ng = w["norm_g"][layer]
        if layer % 2 == 0:
            e = layer // 2
            mix, st = _even_mixer(hn, w, wb, e, bsz, l, ssd_s[e], ssd_buf[e], gdn_s[e], gdn_buf[e])
            ev.append(st)
        else:
            j = layer // 2
            mix, st = _odd_mixer(hn, w, wb, j, lbs[layer], bsz, l, ckv_past[j], kr_past[j], hg_s[j])
            od.append(st)
        x, hn = _add_rms(x, mix, ng[1], ng[2])
        q = _matmul([hn], [wb["xa_w_q"][layer]], BF16)
        xo = _cross_attention(q, mem_k[layer], mem_v[layer], bsz, l, XA_HEADS, XA_HEAD_DIM)
        xa = _matmul([xo], [wb["xa_w_o"][layer]], F32)
        x, hn = _add_rms(x, xa, ng[3], ng[4])
        h1 = _matmul([hn], [wb["ffn_w1"][layer]], BF16, act="relu2")
        f = _matmul([h1], [wb["ffn_w2"][layer]], F32)
        x, hn = _add_rms(x, f, ng[5], w["norm_g"][layer + 1, 0] if layer + 1 < depth else None)
    stack = lambda items, i: jnp.stack([it[i] for it in items])
    return (x.reshape(bsz, l, d), stack(ev, 0), stack(ev, 1), stack(ev, 2), stack(ev, 3),
            stack(od, 0), stack(od, 1), stack(od, 2))


def kernel(x_prompt, x_sample, mem_prompt, state_ssd, state_ssd_conv, state_gdn, state_gdn_conv, cache_mla_ckv, cache_mla_krope, state_hgrn, cache_mem_k, cache_mem_v, norm_g, w_in_even, w_out_even, ssd_conv_w, ssd_conv_b, ssd_dt_bias, ssd_a_log, ssd_d, ssd_norm, gdn_conv_w, gdn_a_log, gdn_dt_bias, gdn_norm, w_in_odd, w_out_odd, mla_q_norm, mla_w_uq, mla_kv_norm, mla_w_uk, mla_w_uv, hg_lower_bound, hg_norm, xa_mem_norm, xa_w_q, xa_w_k, xa_w_v, xa_w_o, ffn_w1, ffn_w2):
    w = dict(norm_g=norm_g, w_in_even=w_in_even, w_out_even=w_out_even, ssd_conv_w=ssd_conv_w, ssd_conv_b=ssd_conv_b,
             ssd_dt_bias=ssd_dt_bias, ssd_a_log=ssd_a_log, ssd_d=ssd_d, ssd_norm=ssd_norm, gdn_conv_w=gdn_conv_w,
             gdn_a_log=gdn_a_log, gdn_dt_bias=gdn_dt_bias, gdn_norm=gdn_norm, w_in_odd=w_in_odd, w_out_odd=w_out_odd,
             mla_q_norm=mla_q_norm, mla_w_uq=mla_w_uq, mla_kv_norm=mla_kv_norm, mla_w_uk=mla_w_uk, mla_w_uv=mla_w_uv,
             hg_norm=hg_norm, xa_w_q=xa_w_q, xa_w_k=xa_w_k, xa_w_v=xa_w_v, xa_w_o=xa_w_o, ffn_w1=ffn_w1,
             ffn_w2=ffn_w2)
    wb = _prep_weights(w)
    depth = norm_g.shape[0]
    n_even, n_odd = (depth + 1) // 2, depth // 2
    lbs = jnp.cumsum(jax.nn.softmax(hg_lower_bound.astype(F32), axis=0), axis=0)
    lbs = lbs - lbs[0]

    b, m, d = mem_prompt.shape
    xa_dim = XA_HEADS * XA_HEAD_DIM
    mk, mv = [], []
    for layer in range(depth):
        mn = _rms(mem_prompt.reshape(b * m, d), xa_mem_norm[layer], BF16)
        mk.append(_matmul([mn], [wb["xa_w_k"][layer]], F32).reshape(b, m, xa_dim))
        mv.append(_matmul([mn], [wb["xa_w_v"][layer]], F32).reshape(b, m, xa_dim))
    p_mem_k = jnp.stack(mk)
    p_mem_v = jnp.stack(mv)
    dt = x_prompt.dtype
    prompt = _forward(
        x_prompt, p_mem_k.astype(BF16), p_mem_v.astype(BF16),
        jnp.zeros((n_even, b, SSD_HEADS, SSD_HEAD_DIM, SSD_STATE), dt),
        jnp.zeros((n_even, b, CONV_W - 1, SSD_CONV_DIM), dt),
        jnp.zeros((n_even, b, GDN_HEADS, GDN_DK, GDN_DV), dt),
        jnp.zeros((n_even, b, CONV_W - 1, GDN_CONV_DIM), dt),
        jnp.zeros((n_odd, b, 0, MLA_KV_RANK), dt),
        jnp.zeros((n_odd, b, 0, MLA_ROPE), dt),
        jnp.zeros((n_odd, b, HG_HEADS, HG_DK, HG_DK), dt),
        w, wb, lbs)
    db = x_sample.shape[0]
    sample = _forward(
        x_sample, cache_mem_k.reshape(depth, db, m, xa_dim).astype(BF16),
        cache_mem_v.reshape(depth, db, m, xa_dim).astype(BF16),
        state_ssd, state_ssd_conv, state_gdn, state_gdn_conv, cache_mla_ckv, cache_mla_krope, state_hgrn,
        w, wb, lbs)
    y_prompt, p_rest = prompt[0], prompt[1:]
    y_sample, s_rest = sample[0], sample[1:]
    return (y_prompt, y_sample, *p_rest,
            p_mem_k.reshape(depth, b, m, XA_HEADS, XA_HEAD_DIM), p_mem_v.reshape(depth, b, m, XA_HEADS, XA_HEAD_DIM),
            *s_rest)
```

```python
import functools

import numpy as np
import jax
import jax.numpy as jnp
from jax import lax
from jax.experimental import pallas as pl
from jax.experimental.pallas import tpu as pltpu

F32 = jnp.float32
BF16 = jnp.bfloat16

VMEM_LIMIT_BYTES = 56 * 1024 * 1024
LANES = 128
NORM_EPS = 1e-6
CHUNK = 64
CHUNK_SHIFT = 6
assert 1 << CHUNK_SHIFT == CHUNK
CONV_W = 4
ROPE_THETA = 10000.0
MASKED = -1e30


def _params(*sem):
    return pltpu.CompilerParams(dimension_semantics=sem, vmem_limit_bytes=VMEM_LIMIT_BYTES)


def _tile(n, cap, mult):
    if n <= cap:
        return n
    for d in range(cap - cap % mult, 0, -mult):
        if n % d == 0:
            return d
    return n


def _dot(a, b):
    return jnp.dot(a, b, preferred_element_type=F32)


def _dot_nt(a, b):
    return lax.dot_general(a, b, (((1,), (1,)), ((), ())), preferred_element_type=F32)


def _dot_tn(a, b):
    return lax.dot_general(a, b, (((0,), (0,)), ((), ())), preferred_element_type=F32)


def _split3(x):
    hi = x.astype(BF16)
    r = x - hi.astype(F32)
    mid = r.astype(BF16)
    lo = (r - mid.astype(F32)).astype(BF16)
    return hi, mid, lo


def _mask_dot(mask3, x):
    return _dot(mask3, jnp.concatenate(_split3(x), axis=0))


def _mask_dot_nt(mask16, x):
    hi, mid, lo = _split3(x)
    return _dot_nt(mask16, hi) + (_dot_nt(mask16, mid) + _dot_nt(mask16, lo))


def _dot_x3(a, b):
    a_hi = a.astype(BF16)
    a_lo = (a - a_hi.astype(F32)).astype(BF16)
    b_hi = b.astype(BF16)
    b_lo = (b - b_hi.astype(F32)).astype(BF16)
    return _dot(a_hi, b_hi) + (_dot(a_hi, b_lo) + _dot(a_lo, b_hi))


def _softplus(x):
    return jnp.maximum(x, 0.0) + jnp.log1p(jnp.exp(-jnp.abs(x)))


def _sigmoid(x):
    return 0.5 * jnp.tanh(0.5 * x) + 0.5


def _silu(x):
    return x * _sigmoid(x)


def _pick_lane(a, idx):
    lane = lax.broadcasted_iota(jnp.int32, a.shape, 1)
    return jnp.sum(jnp.where(lane == idx, a, 0.0), axis=1, keepdims=True)


def _mm_body(*refs, n_ops, nks, act):
    a_refs, b_refs = refs[:n_ops], refs[n_ops:2 * n_ops]
    o_ref, acc_ref = refs[2 * n_ops], refs[2 * n_ops + 1]
    k = pl.program_id(2)
    nk = sum(nks)

    @pl.when(k == 0)
    def _():
        acc_ref[...] = jnp.zeros_like(acc_ref)

    off = 0
    for a_ref, b_ref, n in zip(a_refs, b_refs, nks):
        def step(a_ref=a_ref, b_ref=b_ref):
            acc_ref[...] += _dot(a_ref[...], b_ref[...])
        if n_ops == 1:
            step()
        else:
            pl.when((k >= off) & (k < off + n))(step)
        off += n

    @pl.when(k == nk - 1)
    def _():
        r = acc_ref[...]
        if act == "relu2":
            r = jnp.square(jnp.maximum(r, 0.0))
        o_ref[...] = r.astype(o_ref.dtype)


def _mm_full_k_body(a_ref, b_ref, o_ref, *, act):
    r = _dot(a_ref[...], b_ref[...])
    if act == "relu2":
        r = jnp.square(jnp.maximum(r, 0.0))
    o_ref[...] = r.astype(o_ref.dtype)


FULL_K_MAX = 4096
FULL_K_B_BLOCK_BYTES = 4 * 1024 * 1024


def _matmul(a_list, b_list, out_dtype, act=None, b_rows=None, tm_cap=1024, tn_cap=None, tk_cap=1024):
    m = a_list[0].shape[0]
    n = b_list[0].shape[1]
    b_rows = [0] * len(a_list) if b_rows is None else b_rows
    tm = _tile(m, tm_cap, 8)
    k0 = a_list[0].shape[1]
    if len(a_list) == 1 and k0 <= FULL_K_MAX and b_rows[0] == 0 and b_list[0].shape[0] == k0:
        cap = tn_cap or min(2048, max(512, FULL_K_B_BLOCK_BYTES // (2 * k0)))
        tn = _tile(n, cap, LANES)
        return pl.pallas_call(
            functools.partial(_mm_full_k_body, act=act),
            grid=(m // tm, n // tn),
            in_specs=[pl.BlockSpec((tm, k0), lambda i, j: (i, 0)),
                      pl.BlockSpec((k0, tn), lambda i, j: (0, j))],
            out_specs=pl.BlockSpec((tm, tn), lambda i, j: (i, j)),
            out_shape=jax.ShapeDtypeStruct((m, n), out_dtype),
            compiler_params=_params("parallel", "arbitrary"),
            name="matmul_full_k",
        )(a_list[0], b_list[0])
    tn = _tile(n, tn_cap or (2048 if len(a_list) == 1 else 1024), LANES)
    tk = min(_tile(a.shape[1], tk_cap, LANES) for a in a_list)
    nks, offs = [], []
    for a, b, r0 in zip(a_list, b_list, b_rows):
        assert a.shape[0] == m and b.shape[1] == n and a.shape[1] % tk == 0 and r0 % tk == 0
        assert r0 + a.shape[1] <= b.shape[0]
        offs.append(sum(nks))
        nks.append(a.shape[1] // tk)
    in_specs = []
    for o, nki in zip(offs, nks):
        in_specs.append(pl.BlockSpec((tm, tk), lambda i, j, k, o=o, nki=nki: (i, jnp.clip(k - o, 0, nki - 1))))
    for o, nki, r0 in zip(offs, nks, b_rows):
        in_specs.append(pl.BlockSpec(
            (tk, tn), lambda i, j, k, o=o, nki=nki, rb=r0 // tk: (rb + jnp.clip(k - o, 0, nki - 1), j)))
    return pl.pallas_call(
        functools.partial(_mm_body, n_ops=len(a_list), nks=tuple(nks), act=act),
        grid=(m // tm, n // tn, sum(nks)),
        in_specs=in_specs,
        out_specs=pl.BlockSpec((tm, tn), lambda i, j, k: (i, j)),
        out_shape=jax.ShapeDtypeStruct((m, n), out_dtype),
        scratch_shapes=[pltpu.VMEM((tm, tn), F32)],
        compiler_params=_params("parallel", "parallel", "arbitrary"),
        name="matmul",
    )(*a_list, *b_list)


def _head_mm_body(a_ref, b_ref, o_ref):
    o_ref[...] = _dot(a_ref[...], b_ref[...]).astype(o_ref.dtype)


def _head_split_body(a_ref, b_ref, rot_ref, cos_ref, sin_ref, o_ref, *, scale):
    h = pl.program_id(1)
    n = b_ref.shape[1]
    p = cos_ref.shape[1]
    o_ref[:, 0:n] = (_dot(a_ref[...].astype(BF16), b_ref[...]) * scale).astype(o_ref.dtype)
    pair = rot_ref[...]
    x = jnp.where(h % 2 == 0, pair[:, 0:p], pair[:, p:2 * p])
    swapped = jnp.concatenate([x[:, p // 2:], x[:, :p // 2]], axis=1)
    o_ref[:, n:n + p] = ((x * cos_ref[...] + swapped * sin_ref[...]) * scale).astype(o_ref.dtype)
    o_ref[:, n + p:] = jnp.zeros((o_ref.shape[0], o_ref.shape[1] - n - p), o_ref.dtype)


def _head_split_matmul(a, rot_col0, w, scale, cos2, sin2, tm_cap=1024):
    t = a.shape[0]
    nh, dk, n = w.shape
    p = cos2.shape[1]
    assert n % LANES == 0 and 2 * p == LANES and nh % 2 == 0 and rot_col0 % LANES == 0
    tm = _tile(t, tm_cap, 8)
    return pl.pallas_call(
        functools.partial(_head_split_body, scale=scale),
        grid=(t // tm, nh),
        in_specs=[pl.BlockSpec((tm, dk), lambda i, h: (i, h)),
                  pl.BlockSpec((None, dk, n), lambda i, h: (h, 0, 0)),
                  pl.BlockSpec((tm, LANES), lambda i, h: (i, rot_col0 // LANES + h // 2)),
                  pl.BlockSpec((tm, p), lambda i, h: (i, 0)),
                  pl.BlockSpec((tm, p), lambda i, h: (i, 0))],
        out_specs=pl.BlockSpec((None, tm, n + LANES), lambda i, h: (h, i, 0)),
        out_shape=jax.ShapeDtypeStruct((nh, t, n + LANES), BF16),
        compiler_params=_params("parallel", "arbitrary"),
        name="head_split_matmul",
    )(a, w, a, cos2, sin2)


def _head_merge_matmul(a, w, tm_cap=1024):
    nh, t, k = a.shape
    dv = w.shape[2]
    tm = _tile(t, tm_cap, 8)
    return pl.pallas_call(
        _head_mm_body,
        grid=(t // tm, nh),
        in_specs=[pl.BlockSpec((None, tm, k), lambda i, h: (h, i, 0)),
                  pl.BlockSpec((None, k, dv), lambda i, h: (h, 0, 0))],
        out_specs=pl.BlockSpec((tm, dv), lambda i, h: (i, h)),
        out_shape=jax.ShapeDtypeStruct((t, nh * dv), BF16),
        compiler_params=_params("parallel", "arbitrary"),
        name="head_merge_matmul",
    )(a, w)


def _rms_rows(x, g):
    return x * lax.rsqrt(jnp.mean(x * x, axis=-1, keepdims=True) + NORM_EPS) * g


def _rms_body(x_ref, g_ref, o_ref):
    o_ref[...] = _rms_rows(x_ref[...], g_ref[...]).astype(o_ref.dtype)


def _rms(x, g, out_dtype, col_block=0, width=None, tr_cap=256):
    t = x.shape[0]
    width = x.shape[1] if width is None else width
    tr = _tile(t, tr_cap, 8)
    return pl.pallas_call(
        _rms_body,
        grid=(t // tr,),
        in_specs=[pl.BlockSpec((tr, width), lambda i: (i, col_block)),
                  pl.BlockSpec((1, width), lambda i: (0, 0))],
        out_specs=pl.BlockSpec((tr, width), lambda i: (i, 0)),
        out_shape=jax.ShapeDtypeStruct((t, width), out_dtype),
        compiler_params=_params("parallel"),
        name="rms",
    )(x, g.reshape(1, width))


def _add_rms_body(x_ref, y_ref, g1_ref, g2_ref, xo_ref, ho_ref):
    xn = x_ref[...] + _rms_rows(y_ref[...].astype(F32), g1_ref[...])
    xo_ref[...] = xn
    ho_ref[...] = _rms_rows(xn, g2_ref[...]).astype(ho_ref.dtype)


def _add_rms_last_body(x_ref, y_ref, g1_ref, xo_ref):
    xo_ref[...] = x_ref[...] + _rms_rows(y_ref[...].astype(F32), g1_ref[...])


def _add_rms(x, y, g1, g2, tr_cap=256):
    t, d = x.shape
    tr = _tile(t, tr_cap, 8)
    row = pl.BlockSpec((tr, d), lambda i: (i, 0))
    gain = pl.BlockSpec((1, d), lambda i: (0, 0))
    if g2 is None:
        out = pl.pallas_call(
            _add_rms_last_body, grid=(t // tr,), in_specs=[row, row, gain], out_specs=row,
            out_shape=jax.ShapeDtypeStruct((t, d), F32), compiler_params=_params("parallel"),
            name="add_rms_last",
        )(x, y, g1.reshape(1, d))
        return out, None
    return pl.pallas_call(
        _add_rms_body, grid=(t // tr,), in_specs=[row, row, gain, gain], out_specs=[row, row],
        out_shape=[jax.ShapeDtypeStruct((t, d), F32), jax.ShapeDtypeStruct((t, d), BF16)],
        compiler_params=_params("parallel"), name="add_rms",
    )(x, y, g1.reshape(1, d), g2.reshape(1, d))


CONV_HALO = 8


def _conv_taps(x, w, b):
    acc = b + x * w[CONV_W - 1:CONV_W, :]
    for j in range(1, CONV_W):
        acc = acc + pltpu.roll(x, j, 0) * w[CONV_W - 1 - j:CONV_W - j, :]
    return acc


def _conv_body(u_ref, buf_ref, w_ref, b_ref, o_ref, hist_ref, *, tl):
    lt = pl.program_id(2)

    @pl.when(lt == 0)
    def _():
        hist_ref[...] = jnp.zeros_like(hist_ref)
        hist_ref[CONV_HALO - (CONV_W - 1):CONV_HALO, :] = buf_ref[0]

    u = u_ref[...]
    w = w_ref[...]
    b = b_ref[...]
    o_ref[...] = _silu(_conv_taps(u, w, b))
    head = jnp.concatenate([hist_ref[...], u[0:CONV_HALO, :]], axis=0)
    o_ref[0:CONV_HALO, :] = _silu(_conv_taps(head, w, b)[CONV_HALO:, :])
    hist_ref[...] = u[tl - CONV_HALO:tl, :]


def _conv_silu(u, col0, c, buf, w, b, bsz, l, tl_cap=512, tc=1024):
    assert l >= CONV_W - 1 and col0 % tc == 0 and c % tc == 0
    tl = _tile(l, tl_cap, 8)
    assert tl >= CONV_HALO
    nl = l // tl
    cb0 = col0 // tc
    return pl.pallas_call(
        functools.partial(_conv_body, tl=tl),
        grid=(bsz, c // tc, nl),
        in_specs=[pl.BlockSpec((tl, tc), lambda bi, ci, li: (bi * nl + li, cb0 + ci)),
                  pl.BlockSpec((1, CONV_W - 1, tc), lambda bi, ci, li: (bi, 0, ci)),
                  pl.BlockSpec((CONV_W, tc), lambda bi, ci, li: (0, ci)),
                  pl.BlockSpec((1, tc), lambda bi, ci, li: (0, ci))],
        out_specs=pl.BlockSpec((tl, tc), lambda bi, ci, li: (bi * nl + li, ci)),
        out_shape=jax.ShapeDtypeStruct((bsz * l, c), F32),
        scratch_shapes=[pltpu.VMEM((CONV_HALO, tc), F32)],
        compiler_params=_params("parallel", "parallel", "arbitrary"),
        name="conv_silu",
    )(u, buf, w, b.reshape(1, c))


def _tri_masks(n):
    r = np.arange(n)
    tril = (r[:, None] >= r[None, :]).astype(np.float32)
    return (jnp.asarray(tril, dtype=BF16), jnp.asarray(np.tile(tril, (1, 3)), dtype=BF16),
            jnp.asarray(np.eye(n, dtype=np.float32), dtype=BF16))


def _ssd_body(x_ref, bm_ref, cm_ref, z_ref, sm_ref, dtb_ref, alog_ref, d_ref, gn_ref, s0_ref, tril_ref, tril3_ref,
              eye_ref, y_ref, so_ref, s_scr, cst_scr, *, nc, hpg, hd, nheads, gps):
    g0 = pl.program_id(1) * gps
    c = pl.program_id(2)
    lc = x_ref.shape[0]
    gw = hpg * hd
    nstate = bm_ref.shape[1] // gps

    @pl.when(c == 0)
    def _():
        s_scr[...] = s0_ref[0]

    dt = _softplus(sm_ref[:, 0:nheads] + dtb_ref[...])
    da = dt * (-jnp.exp(alog_ref[...]))
    cs = _mask_dot(tril3_ref[...], da)
    cst_scr[...] = _mask_dot_nt(eye_ref[...], cs)
    tri = tril_ref[...].astype(F32) > 0.0

    x = x_ref[...]
    gs = range(gps)
    bm = [bm_ref[:, gi * nstate:(gi + 1) * nstate].astype(BF16) for gi in gs]
    cm = [cm_ref[:, gi * nstate:(gi + 1) * nstate].astype(BF16) for gi in gs]
    s = [s_scr[gi * gw:(gi + 1) * gw, :] for gi in gs]
    cb = [_dot_nt(cm[gi], bm[gi]) for gi in gs]
    y_state = [_dot_nt(cm[gi], s[gi].astype(BF16)) for gi in gs]
    js = range(gps * hpg)
    hd_ids = [g0 * hpg + j for j in js]
    cs_col = [_pick_lane(cs, h) for h in hd_ids]
    dt_col = [_pick_lane(dt, h) for h in hd_ids]
    d_h = [_pick_lane(d_ref[...], h) for h in hd_ids]
    cs_row = [cst_scr[pl.ds(h, 1), :] for h in hd_ids]
    cs_last = [cs_row[j][:, lc - 1:lc] for j in js]
    mix = [(cb[j // hpg] * jnp.exp(jnp.where(tri, cs_col[j] - cs_row[j], MASKED))).astype(BF16) for j in js]
    xh = [x[:, j * hd:(j + 1) * hd] for j in js]
    xdt = [xh[j] * dt_col[j] for j in js]
    y_in = [_dot(mix[j], xdt[j].astype(BF16)) for j in js]
    ys = [y_in[j] + y_state[j // hpg][:, (j % hpg) * hd:(j % hpg + 1) * hd] * jnp.exp(cs_col[j]) + d_h[j] * xh[j]
          for j in js]
    xws = [xdt[j] * jnp.exp(cs_last[j] - cs_col[j]) for j in js]
    decs = [jnp.broadcast_to(jnp.exp(cs_last[j]), (hd, nstate)) for j in js]
    for gi in gs:
        hsl = slice(gi * hpg, (gi + 1) * hpg)
        xw = jnp.concatenate(xws[hsl], axis=1)
        s_new = s[gi] * jnp.concatenate(decs[hsl], axis=0) + _dot_tn(xw.astype(BF16), bm[gi])
        s_scr[gi * gw:(gi + 1) * gw, :] = s_new
        y = jnp.concatenate(ys[hsl], axis=1) * _silu(z_ref[:, gi * gw:(gi + 1) * gw])
        y_ref[:, gi * gw:(gi + 1) * gw] = _rms_rows(y, gn_ref[:, gi * gw:(gi + 1) * gw]).astype(y_ref.dtype)

    @pl.when(c == nc - 1)
    def _():
        so_ref[0] = s_scr[...]


SSD_GROUPS_PER_STEP = 2


def _ssd(xbc, pm, z_col0, sm, dt_bias, a_log, d, gn, s0, bsz, l, nheads, hd, ngroups, nstate):
    t = bsz * l
    lc = CHUNK
    nc = l // lc
    hpg = nheads // ngroups
    gps = SSD_GROUPS_PER_STEP
    gw = gps * hpg * hd
    sw = gps * nstate
    inner = nheads * hd
    assert l % lc == 0 and nheads == lc and z_col0 % gw == 0 and ngroups % gps == 0 and inner % sw == 0
    tril, tril3, eye = _tri_masks(lc)
    rows = lambda bi, gi, ci: bi * nc + ci
    const2 = lambda bi, gi, ci: (0, 0)
    y, s_new = pl.pallas_call(
        functools.partial(_ssd_body, nc=nc, hpg=hpg, hd=hd, nheads=nheads, gps=gps),
        grid=(bsz, ngroups // gps, nc),
        in_specs=[pl.BlockSpec((lc, gw), lambda bi, gi, ci: (rows(bi, gi, ci), gi)),
                  pl.BlockSpec((lc, sw), lambda bi, gi, ci: (rows(bi, gi, ci), inner // sw + gi)),
                  pl.BlockSpec((lc, sw), lambda bi, gi, ci: (rows(bi, gi, ci), (inner + ngroups * nstate) // sw + gi)),
                  pl.BlockSpec((lc, gw), lambda bi, gi, ci: (rows(bi, gi, ci), z_col0 // gw + gi)),
                  pl.BlockSpec((lc, sm.shape[1]), lambda bi, gi, ci: (rows(bi, gi, ci), 0)),
                  pl.BlockSpec((1, nheads), const2),
                  pl.BlockSpec((1, nheads), const2),
                  pl.BlockSpec((1, nheads), const2),
                  pl.BlockSpec((1, gw), lambda bi, gi, ci: (0, gi)),
                  pl.BlockSpec((1, gw, nstate), lambda bi, gi, ci: (bi, gi, 0)),
                  pl.BlockSpec((lc, lc), const2),
                  pl.BlockSpec((lc, 3 * lc), const2),
                  pl.BlockSpec((lc, lc), const2)],
        out_specs=[pl.BlockSpec((lc, gw), lambda bi, gi, ci: (rows(bi, gi, ci), gi)),
                   pl.BlockSpec((1, gw, nstate), lambda bi, gi, ci: (bi, gi, 0))],
        out_shape=[jax.ShapeDtypeStruct((t, inner), BF16),
                   jax.ShapeDtypeStruct((bsz, inner, nstate), F32)],
        scratch_shapes=[pltpu.VMEM((gw, nstate), F32), pltpu.VMEM((nheads, lc), F32)],
        compiler_params=_params("parallel", "parallel", "arbitrary"),
        name="ssd_scan",
    )(xbc, xbc, xbc, pm, sm, dt_bias.reshape(1, nheads), a_log.reshape(1, nheads), d.reshape(1, nheads),
      gn.reshape(1, inner), s0, tril, tril3, eye)
    return y, s_new


def _gdn_heads(qs, ks, vs, gates, betas, cs_cols, cs_rows, states, tri, eye, gn):
    n = len(qs)
    hd = range(n)
    lc, dk = qs[0].shape
    dv = vs[0].shape[1]
    cs_last = [cs_rows[i][:, lc - 1:lc] for i in hd]
    decay = [jnp.exp(jnp.where(tri, cs_cols[i] - cs_rows[i], MASKED)) for i in hd]
    qn = [qs[i] * (lax.rsqrt(jnp.sum(qs[i] * qs[i], axis=1, keepdims=True) + 1e-6) * dk ** -0.5) for i in hd]
    kn = [ks[i] * lax.rsqrt(jnp.sum(ks[i] * ks[i], axis=1, keepdims=True) + 1e-6) for i in hd]
    kb = [kn[i] * betas[i] for i in hd]
    kn16 = [kn[i].astype(BF16) for i in hd]
    off_diag = 1.0 - eye
    a = [_dot_nt(kb[i].astype(BF16), kn16[i]) * decay[i] * off_diag for i in hd]
    qk = [_dot_nt(qn[i].astype(BF16), kn16[i]) * decay[i] for i in hd]
    s16 = [states[i].astype(BF16) for i in hd]
    o_state = [_dot((qn[i] * jnp.exp(cs_cols[i])).astype(BF16), s16[i]) for i in hd]
    tinv = [eye - a[i] for i in hd]
    p = [_dot_x3(a[i], a[i]) for i in hd]
    steps = max(1, int(np.ceil(np.log2(lc))) - 1)
    for st in range(steps):
        tinv = [tinv[i] + _dot_x3(tinv[i], p[i]) for i in hd]
        if st + 1 < steps:
            p = [_dot_x3(p[i], p[i]) for i in hd]
    rhs = [jnp.concatenate([vs[i] * betas[i], kb[i] * jnp.exp(cs_cols[i])], axis=1).astype(BF16) for i in hd]
    sol = [_dot(tinv[i].astype(BF16), rhs[i]) for i in hd]
    vn16 = [(sol[i][:, :dv] - _dot(sol[i][:, dv:].astype(BF16), s16[i])).astype(BF16) for i in hd]
    o = [o_state[i] + _dot(qk[i].astype(BF16), vn16[i]) for i in hd]
    k_end = [(kn[i] * jnp.exp(cs_last[i] - cs_cols[i])).astype(BF16) for i in hd]
    s_new = [states[i] * jnp.exp(cs_last[i]) + _dot_tn(k_end[i], vn16[i]) for i in hd]
    outs = [_rms_rows(o[i], gn) * _silu(gates[i]) for i in hd]
    return outs, s_new


def _gdn_body(q_ref, k_ref, v_ref, gate_ref, sm_ref, alog_ref, dtb_ref, gn_ref, s0_ref, tril_ref, tril3_ref, eye_ref,
              eyeh_ref, o_ref, so_ref, s_scr, cst_scr, *, nc, nheads, a_col0, hps):
    hb = pl.program_id(1)
    c = pl.program_id(2)
    d = q_ref.shape[1] // hps

    @pl.when(c == 0)
    def _():
        s_scr[...] = s0_ref[0]

    a_raw = sm_ref[:, a_col0:a_col0 + nheads]
    b_raw = sm_ref[:, a_col0 + nheads:a_col0 + 2 * nheads]
    gl = -jnp.exp(alog_ref[...]) * _softplus(a_raw + dtb_ref[...])
    beta_all = _sigmoid(b_raw)
    cs = _mask_dot(tril3_ref[...], gl)
    cst_scr[...] = _mask_dot_nt(eyeh_ref[...], cs)
    tri = tril_ref[...].astype(F32) > 0.0
    eye = eye_ref[...].astype(F32)
    gn = gn_ref[...]
    sls = [slice(j * d, (j + 1) * d) for j in range(hps)]
    hidx = [hb * hps + j for j in range(hps)]
    outs, s_new = _gdn_heads(
        [q_ref[:, sl] for sl in sls], [k_ref[:, sl] for sl in sls], [v_ref[:, sl] for sl in sls],
        [gate_ref[:, sl] for sl in sls], [_pick_lane(beta_all, h) for h in hidx], [_pick_lane(cs, h) for h in hidx],
        [cst_scr[pl.ds(h, 1), :] for h in hidx], [s_scr[j] for j in range(hps)], tri, eye, gn)
    for j in range(hps):
        s_scr[j] = s_new[j]
        o_ref[:, sls[j]] = outs[j].astype(o_ref.dtype)

    @pl.when(c == nc - 1)
    def _():
        so_ref[0] = s_scr[...]


GDN_HEADS_PER_STEP = 8


def _gdn(qkv, pm, gate_col0, sm, a_col0, a_log, dt_bias, gn, s0, bsz, l, nheads, dk, dv):
    hps = GDN_HEADS_PER_STEP
    w = hps * dv
    assert dk == dv and gate_col0 % w == 0 and nheads % hps == 0
    t = bsz * l
    lc = CHUNK
    nc = l // lc
    nhb = nheads // hps
    tril, tril3, eye = _tri_masks(lc)
    eyeh = jnp.asarray(np.eye(nheads, dtype=np.float32), dtype=BF16)
    rows = lambda bi, hi, ci: bi * nc + ci
    const2 = lambda bi, hi, ci: (0, 0)
    col = lambda k: pl.BlockSpec((lc, w), lambda bi, hi, ci: (rows(bi, hi, ci), k * nhb + hi))
    state = pl.BlockSpec((1, hps, dk, dv), lambda bi, hi, ci: (bi, hi, 0, 0))
    o, s_new = pl.pallas_call(
        functools.partial(_gdn_body, nc=nc, nheads=nheads, a_col0=a_col0, hps=hps),
        grid=(bsz, nhb, nc),
        in_specs=[col(0), col(1), col(2),
                  pl.BlockSpec((lc, w), lambda bi, hi, ci: (rows(bi, hi, ci), gate_col0 // w + hi)),
                  pl.BlockSpec((lc, sm.shape[1]), lambda bi, hi, ci: (rows(bi, hi, ci), 0)),
                  pl.BlockSpec((1, nheads), const2),
                  pl.BlockSpec((1, nheads), const2),
                  pl.BlockSpec((1, dv), const2),
                  state,
                  pl.BlockSpec((lc, lc), const2),
                  pl.BlockSpec((lc, 3 * lc), const2),
                  pl.BlockSpec((lc, lc), const2),
                  pl.BlockSpec((nheads, nheads), const2)],
        out_specs=[pl.BlockSpec((lc, w), lambda bi, hi, ci: (rows(bi, hi, ci), hi)), state],
        out_shape=[jax.ShapeDtypeStruct((t, nheads * dv), BF16),
                   jax.ShapeDtypeStruct((bsz, nheads, dk, dv), F32)],
        scratch_shapes=[pltpu.VMEM((hps, dk, dv), F32), pltpu.VMEM((nheads, lc), F32)],
        compiler_params=_params("parallel", "parallel", "arbitrary"),
        name="gdn_scan",
    )(qkv, qkv, qkv, pm, sm, a_log.reshape(1, nheads), dt_bias.reshape(1, nheads), gn.reshape(1, dv), s0,
      tril, tril3, eye, eyeh)
    return o, s_new


def _hgrn_levels(lc):
    hs = []
    h = lc // 2
    while h >= 1:
        hs.append(h)
        h //= 2
    return hs


def _hgrn_masks(lc):
    r = np.arange(lc)
    t, u = r[:, None], r[None, :]
    mats = [u <= t, u > t]
    for h in _hgrn_levels(lc):
        start = (t // h) * h
        mats.append((u >= start) & (u <= t))
        mats.append((u > t) & (u <= start + h - 1))
    return jnp.asarray(np.tile(np.concatenate(mats, axis=0).astype(np.float32), (1, 3)), dtype=BF16)


def _hgrn_heads(qs, hfs, vs, gates, lbs, gn, sts, mst16):
    n = len(qs)
    hd = range(n)
    lc, d = qs[0].shape
    logf = [jnp.log(lbs[i] + (1.0 - lbs[i]) * _sigmoid(hfs[i])) for i in hd]
    kk = [(1.0 - lbs[i]) * _sigmoid(-hfs[i]) for i in hd]
    cums = [_mask_dot(mst16, logf[i]) for i in hd]
    v16 = [vs[i].astype(BF16) for i in hd]
    st16 = [sts[i].astype(BF16) for i in hd]
    o_state = [_dot_nt((qs[i] * jnp.exp(cums[i][0:lc])).astype(BF16), st16[i]) for i in hd]

    row = lax.broadcasted_iota(jnp.int32, (lc, 1), 0)
    ti = lax.broadcasted_iota(jnp.int32, (lc, lc), 0)
    si = lax.broadcasted_iota(jnp.int32, (lc, lc), 1)
    att = [jnp.where(ti == si, jnp.sum(qs[i] * kk[i], axis=1, keepdims=True), 0.0) for i in hd]
    for li, hs in enumerate(_hgrn_levels(lc)):
        sh = int(np.log2(hs))
        upper = ((row >> sh) & 1) == 1
        same_block = (ti >> (sh + 1)) == (si >> (sh + 1))
        for i in hd:
            seg = cums[i][(2 + 2 * li) * lc:(3 + 2 * li) * lc]
            rest = cums[i][(3 + 2 * li) * lc:(4 + 2 * li) * lc]
            ql = jnp.where(upper, qs[i] * jnp.exp(seg), 0.0)
            kl = jnp.where(upper, 0.0, kk[i] * jnp.exp(rest))
            att[i] = att[i] + jnp.where(same_block, _dot_nt(ql.astype(BF16), kl.astype(BF16)), 0.0)

    outs, st_new = [], []
    for i in hd:
        o = o_state[i] + _dot(att[i].astype(BF16), v16[i])
        k_end = kk[i] * jnp.exp(cums[i][lc:2 * lc])
        st_new.append(sts[i] * jnp.exp(cums[i][lc - 1:lc, :]) + _dot_tn(v16[i], k_end.astype(BF16)))
        outs.append(_rms_rows(o, gn) * _silu(gates[i]))
    return outs, st_new


def _hgrn_body(q_ref, f_ref, i_ref, gate_ref, lb_ref, gn_ref, s0_ref, mst_ref, o_ref, so_ref, st_scr, *, nc, hps):
    c = pl.program_id(2)
    d = q_ref.shape[1] // hps

    @pl.when(c == 0)
    def _():
        for j in range(hps):
            st_scr[j] = s0_ref[0, j].T

    mst16 = mst_ref[...]
    gn = gn_ref[...]
    sls = [slice(j * d, (j + 1) * d) for j in range(hps)]
    outs, st_new = _hgrn_heads(
        [q_ref[:, sl] for sl in sls], [f_ref[:, sl] for sl in sls], [i_ref[:, sl] for sl in sls],
        [gate_ref[:, sl] for sl in sls], [lb_ref[:, sl] for sl in sls], gn, [st_scr[j] for j in range(hps)], mst16)
    for j in range(hps):
        st_scr[j] = st_new[j]
        o_ref[:, sls[j]] = outs[j].astype(o_ref.dtype)

    @pl.when(c == nc - 1)
    def _():
        for j in range(hps):
            so_ref[0, j] = st_scr[j].T


HGRN_HEADS_PER_STEP = 8


def _hgrn(pm, lb, gn, s0, bsz, l, nheads, dk):
    hps = HGRN_HEADS_PER_STEP
    assert nheads % hps == 0
    w = hps * dk
    nhb = nheads // hps
    t = bsz * l
    lc = CHUNK
    nc = l // lc
    mst = _hgrn_masks(lc)
    rows = lambda bi, hi, ci: bi * nc + ci
    const2 = lambda bi, hi, ci: (0, 0)
    col = lambda k: pl.BlockSpec((lc, w), lambda bi, hi, ci: (rows(bi, hi, ci), k * nhb + hi))
    state = pl.BlockSpec((1, hps, dk, dk), lambda bi, hi, ci: (bi, hi, 0, 0))
    o, s_new = pl.pallas_call(
        functools.partial(_hgrn_body, nc=nc, hps=hps),
        grid=(bsz, nhb, nc),
        in_specs=[col(0), col(1), col(2), col(3),
                  pl.BlockSpec((1, w), lambda bi, hi, ci: (0, hi)),
                  pl.BlockSpec((1, dk), const2),
                  state,
                  pl.BlockSpec(mst.shape, const2)],
        out_specs=[pl.BlockSpec((lc, w), lambda bi, hi, ci: (rows(bi, hi, ci), hi)), state],
        out_shape=[jax.ShapeDtypeStruct((t, nheads * dk), BF16),
                   jax.ShapeDtypeStruct((bsz, nheads, dk, dk), F32)],
        scratch_shapes=[pltpu.VMEM((hps, dk, dk), F32)],
        compiler_params=_params("parallel", "parallel", "arbitrary"),
        name="hgrn_scan",
    )(pm, pm, pm, pm, lb.reshape(1, nheads * dk), gn.reshape(1, dk), s0, mst)
    return o, s_new


MLA_SUB_ROWS = 512


def _mla_body(qi_ref, ki_ref, last_ref, q_ref, k_ref, o_ref, m_scr, l_scr, acc_scr, *, tq, tk, t_past, hs):
    step = pl.program_id(1)
    iq = qi_ref[step]
    ikv = ki_ref[step]
    nh, _, dq = q_ref.shape
    r = o_ref.shape[2]
    rows = hs * tq
    first_limit = t_past + iq * tq + CHUNK

    @pl.when(ikv == 0)
    def _():
        m_scr[...] = jnp.full_like(m_scr, MASKED)
        l_scr[...] = jnp.zeros_like(l_scr)
        acc_scr[...] = jnp.zeros_like(acc_scr)

    def update(masked):
        kc = k_ref[0]
        vals = kc[:, 0:r]
        if masked:
            tok = lax.broadcasted_iota(jnp.int32, (rows, 1), 0) & (tq - 1)
            limit = t_past + iq * tq + ((tok >> CHUNK_SHIFT) + 1) * CHUNK
            visible = (ikv * tk + lax.broadcasted_iota(jnp.int32, (1, tk), 1)) < limit

        def scores(g):
            return _dot_nt(q_ref[g * hs:(g + 1) * hs].reshape(rows, dq), kc)

        ngroups = nh // hs
        s_next = scores(0)
        for g in range(ngroups):
            rsl = slice(g * rows, (g + 1) * rows)
            s = s_next
            if g + 1 < ngroups:
                s_next = scores(g + 1)
            if masked:
                s = jnp.where(visible, s, MASKED)
            m_old = m_scr[rsl]
            m_new = jnp.maximum(m_old, jnp.max(s, axis=1, keepdims=True))
            alpha = jnp.exp(m_old - m_new)
            p = jnp.exp(s - m_new)
            l_scr[rsl] = alpha * l_scr[rsl] + jnp.sum(p, axis=1, keepdims=True)
            acc_scr[rsl] = alpha * acc_scr[rsl] + _dot(p.astype(BF16), vals)
            m_scr[rsl] = m_new

    fully_visible = (ikv + 1) * tk <= first_limit
    pl.when(fully_visible)(lambda: update(False))
    pl.when(jnp.logical_not(fully_visible))(lambda: update(True))

    @pl.when(last_ref[step] == 1)
    def _():
        o = acc_scr[...] / l_scr[...]
        o_ref[...] = o.reshape(nh, tq, r).astype(o_ref.dtype)


def _mla_attention(q, k, r, bsz, l, t_past, tq, tk):
    nh, t, dq = q.shape
    tkeys = k.shape[1]
    assert l % tq == 0 and tq % CHUNK == 0 and t_past % CHUNK == 0 and tkeys % tk == 0 and tkeys >= t_past + l
    assert tq & (tq - 1) == 0 and k.shape[2] == dq and r % LANES == 0
    nq = l // tq
    hs = min(nh, max(1, MLA_SUB_ROWS // tq))
    assert nh % hs == 0
    qi, ki, last = [], [], []
    for qb in range(nq):
        need = -(-(t_past + (qb + 1) * tq) // tk)
        qi += [qb] * need
        ki += list(range(need))
        last += [0] * (need - 1) + [1]
    tables = [jnp.asarray(np.asarray(a, np.int32)) for a in (qi, ki, last)]
    qmap = lambda bi, si, qi_ref, ki_ref, last_ref: (0, bi * nq + qi_ref[si], 0)
    kmap = lambda bi, si, qi_ref, ki_ref, last_ref: (bi, ki_ref[si], 0)
    return pl.pallas_call(
        functools.partial(_mla_body, tq=tq, tk=tk, t_past=t_past, hs=hs),
        grid_spec=pltpu.PrefetchScalarGridSpec(
            num_scalar_prefetch=3,
            grid=(bsz, len(qi)),
            in_specs=[pl.BlockSpec((nh, tq, dq), qmap),
                      pl.BlockSpec((1, tk, dq), kmap)],
            out_specs=pl.BlockSpec((nh, tq, r), qmap),
            scratch_shapes=[pltpu.VMEM((nh * tq, 1), F32), pltpu.VMEM((nh * tq, 1), F32),
                            pltpu.VMEM((nh * tq, r), F32)]),
        out_shape=jax.ShapeDtypeStruct((nh, t, r), BF16),
        compiler_params=_params("parallel", "arbitrary"),
        name="mla_attention",
    )(*tables, q, k)


def _xattn_body(q_ref, k_ref, v_ref, o_ref, *, nheads, hd):
    q = q_ref[...]
    k = k_ref[0]
    v = v_ref[0]
    outs = []
    for h in range(nheads):
        sl = slice(h * hd, (h + 1) * hd)
        s = _dot_nt(q[:, sl], k[:, sl]) * hd ** -0.5
        s = s - jnp.max(s, axis=1, keepdims=True)
        p = jnp.exp(s)
        p = p / jnp.sum(p, axis=1, keepdims=True)
        outs.append(_dot(p.astype(BF16), v[:, sl]))
    o_ref[...] = jnp.concatenate(outs, axis=1).astype(o_ref.dtype)


def _cross_attention(q, mem_k, mem_v, bsz, l, nheads, hd, tl_cap=512):
    t, d = q.shape
    m = mem_k.shape[1]
    tl = _tile(l, tl_cap, 8)
    nl = l // tl
    return pl.pallas_call(
        functools.partial(_xattn_body, nheads=nheads, hd=hd),
        grid=(bsz, nl),
        in_specs=[pl.BlockSpec((tl, d), lambda bi, li: (bi * nl + li, 0)),
                  pl.BlockSpec((1, m, d), lambda bi, li: (bi, 0, 0)),
                  pl.BlockSpec((1, m, d), lambda bi, li: (bi, 0, 0))],
        out_specs=pl.BlockSpec((tl, d), lambda bi, li: (bi * nl + li, 0)),
        out_shape=jax.ShapeDtypeStruct((t, d), BF16),
        compiler_params=_params("parallel", "parallel"),
        name="cross_attention",
    )(q, mem_k, mem_v)


SSD_HEADS, SSD_HEAD_DIM, SSD_GROUPS, SSD_STATE = 64, 64, 8, 128
SSD_INNER = SSD_HEADS * SSD_HEAD_DIM
SSD_CONV_DIM = SSD_INNER + 2 * SSD_GROUPS * SSD_STATE
GDN_HEADS, GDN_DK, GDN_DV = 32, 128, 128
GDN_CONV_DIM = 2 * GDN_HEADS * GDN_DK + GDN_HEADS * GDN_DV
MLA_HEADS, MLA_Q_RANK, MLA_KV_RANK, MLA_NOPE, MLA_ROPE, MLA_V = 32, 1024, 512, 128, 64, 128
HG_HEADS, HG_DK = 32, 128
HG_W = HG_HEADS * HG_DK
XA_HEADS, XA_HEAD_DIM = 4, 128
MLA_TK = 1024
MLA_TQ = 128


def _rope_tables(pos, p):
    half = p // 2
    inv = ROPE_THETA ** (-jnp.arange(half, dtype=F32) / half)
    ang = pos[:, None] * inv[None, :]
    return jnp.cos(ang), jnp.sin(ang)


def _prep_weights(w):
    out = {}
    e_sizes = [SSD_INNER, SSD_CONV_DIM, SSD_HEADS, GDN_CONV_DIM, GDN_HEADS, GDN_HEADS, GDN_HEADS * GDN_DV]
    e_off = np.concatenate([[0], np.cumsum(e_sizes)])
    wi = w["w_in_even"]
    sec = lambda a, offs, i: a[:, :, offs[i]:offs[i + 1]]
    out["w_in_even_main"] = jnp.concatenate(
        [sec(wi, e_off, 0), sec(wi, e_off, 1), sec(wi, e_off, 3), sec(wi, e_off, 6)], axis=-1).astype(BF16)
    out["w_in_even_small"] = jnp.concatenate(
        [sec(wi, e_off, 2), sec(wi, e_off, 4), sec(wi, e_off, 5)], axis=-1).astype(BF16)
    o_sizes = [MLA_Q_RANK, MLA_KV_RANK, MLA_ROPE, HG_W, HG_W, HG_W, HG_W]
    o_off = np.concatenate([[0], np.cumsum(o_sizes)])
    wo = w["w_in_odd"]
    out["w_in_odd_main"] = jnp.concatenate(
        [sec(wo, o_off, 3), sec(wo, o_off, 4), sec(wo, o_off, 5), sec(wo, o_off, 6), sec(wo, o_off, 0)],
        axis=-1).astype(BF16)
    out["w_in_odd_small"] = jnp.concatenate([sec(wo, o_off, 1), sec(wo, o_off, 2)], axis=-1).astype(BF16)
    n_odd = wo.shape[0]
    uq = w["mla_w_uq"].reshape(n_odd, MLA_Q_RANK, MLA_HEADS, MLA_NOPE + MLA_ROPE)
    out["w_uq"] = jnp.concatenate(
        [uq[..., :MLA_NOPE].reshape(n_odd, MLA_Q_RANK, MLA_HEADS * MLA_NOPE),
         uq[..., MLA_NOPE:].reshape(n_odd, MLA_Q_RANK, MLA_HEADS * MLA_ROPE)], axis=-1).astype(BF16)
    out["w_uk"] = jnp.transpose(w["mla_w_uk"], (0, 2, 3, 1)).astype(BF16)
    out["w_uv"] = jnp.transpose(w["mla_w_uv"], (0, 2, 1, 3)).astype(BF16)
    for name in ("w_out_even", "w_out_odd", "xa_w_q", "xa_w_k", "xa_w_v", "xa_w_o", "ffn_w1", "ffn_w2"):
        out[name] = w[name].astype(BF16)
    return out


def _even_mixer(hn, w, wb, e, bsz, l, ssd_s, ssd_buf, gdn_s, gdn_buf):
    pm = _matmul([hn], [wb["w_in_even_main"][e]], F32)
    sm = _matmul([hn], [wb["w_in_even_small"][e]], F32)
    xbc_col0, qkv_col0 = SSD_INNER, SSD_INNER + SSD_CONV_DIM
    gate_col0 = qkv_col0 + GDN_CONV_DIM
    xbc = _conv_silu(pm, xbc_col0, SSD_CONV_DIM, ssd_buf, w["ssd_conv_w"][e], w["ssd_conv_b"][e], bsz, l)
    qkv = _conv_silu(pm, qkv_col0, GDN_CONV_DIM, gdn_buf, w["gdn_conv_w"][e],
                     jnp.zeros((GDN_CONV_DIM,), F32), bsz, l)
    y, ssd_new = _ssd(xbc, pm, 0, sm, w["ssd_dt_bias"][e], w["ssd_a_log"][e], w["ssd_d"][e], w["ssd_norm"][e],
                      ssd_s.reshape(bsz, SSD_INNER, SSD_STATE), bsz, l, SSD_HEADS, SSD_HEAD_DIM, SSD_GROUPS,
                      SSD_STATE)
    o, gdn_new = _gdn(qkv, pm, gate_col0, sm, SSD_HEADS, w["gdn_a_log"][e], w["gdn_dt_bias"][e], w["gdn_norm"][e],
                      gdn_s, bsz, l, GDN_HEADS, GDN_DK, GDN_DV)
    w_out = wb["w_out_even"][e]
    mix = _matmul([y, o], [w_out, w_out], BF16, b_rows=[0, SSD_INNER])
    pm3 = pm.reshape(bsz, l, pm.shape[1])
    tail = pm3[:, l - (CONV_W - 1):]
    states = (ssd_new.reshape(bsz, SSD_HEADS, SSD_HEAD_DIM, SSD_STATE), tail[:, :, xbc_col0:qkv_col0],
              gdn_new, tail[:, :, qkv_col0:gate_col0])
    return mix, states


def _odd_mixer(hn, w, wb, j, lb, bsz, l, ckv_past, kr_past, hg_s):
    t_past = ckv_past.shape[1]
    pm = _matmul([hn], [wb["w_in_odd_main"][j]], F32)
    sm = _matmul([hn], [wb["w_in_odd_small"][j]], F32)
    posf = (t_past + jnp.arange(l, dtype=jnp.int32)).astype(F32)
    scale = (MLA_NOPE + MLA_ROPE) ** -0.5
    cqn = _rms(pm, w["mla_q_norm"][j], BF16, col_block=4 * HG_W // MLA_Q_RANK, width=MLA_Q_RANK)
    q = _matmul([cqn], [wb["w_uq"][j]], F32)
    lane_pad = (-MLA_ROPE) % LANES
    cos, sin = _rope_tables(posf, MLA_ROPE)
    cos2 = jnp.tile(jnp.concatenate([cos, cos], axis=1), (bsz, 1))
    sin2 = jnp.tile(jnp.concatenate([-sin, sin], axis=1), (bsz, 1))
    q_att = _head_split_matmul(q, MLA_HEADS * MLA_NOPE, wb["w_uk"][j], scale, cos2, sin2)
    c_kv = _rms(sm, w["mla_kv_norm"][j], F32, col_block=0, width=MLA_KV_RANK).reshape(bsz, l, MLA_KV_RANK)
    kr = sm[:, MLA_KV_RANK:].reshape(bsz, l, MLA_ROPE)
    kr1, kr2 = kr[..., :MLA_ROPE // 2], kr[..., MLA_ROPE // 2:]
    k_rope = jnp.concatenate([kr1 * cos - kr2 * sin, kr2 * cos + kr1 * sin], axis=-1)
    tkeys = t_past + l
    pad = (-tkeys) % MLA_TK
    k_att = jnp.concatenate([jnp.concatenate([ckv_past, c_kv], axis=1).astype(BF16),
                             jnp.concatenate([kr_past, k_rope], axis=1).astype(BF16)], axis=2)
    k_att = jnp.pad(k_att, ((0, 0), (0, pad), (0, lane_pad)))
    tq = MLA_TQ if l % MLA_TQ == 0 else CHUNK
    o_lat = _mla_attention(q_att, k_att, MLA_KV_RANK, bsz, l, t_past, tq, MLA_TK)
    o_mla = _head_merge_matmul(o_lat, wb["w_uv"][j])
    o_hg, hg_new = _hgrn(pm, lb, w["hg_norm"][j], hg_s, bsz, l, HG_HEADS, HG_DK)
    w_out = wb["w_out_odd"][j]
    mix = _matmul([o_mla, o_hg], [w_out, w_out], BF16, b_rows=[0, MLA_HEADS * MLA_V])
    return mix, (c_kv, k_rope, hg_new)


def _forward(x, mem_k, mem_v, ssd_s, ssd_buf, gdn_s, gdn_buf, ckv_past, kr_past, hg_s, w, wb, lbs):
    bsz, l, d = x.shape
    depth = w["norm_g"].shape[0]
    x = x.reshape(bsz * l, d)
    hn = _rms(x, w["norm_g"][0, 0], BF16)
    ev, od = [], []
    for layer in range(depth):
        ng = w["norm_g"][layer]
        if layer % 2 == 0:
            e = layer // 2
            mix, st = _even_mixer(hn, w, wb, e, bsz, l, ssd_s[e], ssd_buf[e], gdn_s[e], gdn_buf[e])
            ev.append(st)
        else:
            j = layer // 2
            mix, st = _odd_mixer(hn, w, wb, j, lbs[layer], bsz, l, ckv_past[j], kr_past[j], hg_s[j])
            od.append(st)
        x, hn = _add_rms(x, mix, ng[1], ng[2])
        q = _matmul([hn], [wb["xa_w_q"][layer]], BF16)
        xo = _cross_attention(q, mem_k[layer], mem_v[layer], bsz, l, XA_HEADS, XA_HEAD_DIM)
        xa = _matmul([xo], [wb["xa_w_o"][layer]], BF16)
        x, hn = _add_rms(x, xa, ng[3], ng[4])
        h1 = _matmul([hn], [wb["ffn_w1"][layer]], BF16, act="relu2")
        f = _matmul([h1], [wb["ffn_w2"][layer]], BF16)
        x, hn = _add_rms(x, f, ng[5], w["norm_g"][layer + 1, 0] if layer + 1 < depth else None)
    stack = lambda items, i: jnp.stack([it[i] for it in items])
    return (x.reshape(bsz, l, d), stack(ev, 0), stack(ev, 1), stack(ev, 2), stack(ev, 3),
            stack(od, 0), stack(od, 1), stack(od, 2))


def kernel(x_prompt, x_sample, mem_prompt, state_ssd, state_ssd_conv, state_gdn, state_gdn_conv, cache_mla_ckv, cache_mla_krope, state_hgrn, cache_mem_k, cache_mem_v, norm_g, w_in_even, w_out_even, ssd_conv_w, ssd_conv_b, ssd_dt_bias, ssd_a_log, ssd_d, ssd_norm, gdn_conv_w, gdn_a_log, gdn_dt_bias, gdn_norm, w_in_odd, w_out_odd, mla_q_norm, mla_w_uq, mla_kv_norm, mla_w_uk, mla_w_uv, hg_lower_bound, hg_norm, xa_mem_norm, xa_w_q, xa_w_k, xa_w_v, xa_w_o, ffn_w1, ffn_w2):
    w = dict(norm_g=norm_g, w_in_even=w_in_even, w_out_even=w_out_even, ssd_conv_w=ssd_conv_w, ssd_conv_b=ssd_conv_b,
             ssd_dt_bias=ssd_dt_bias, ssd_a_log=ssd_a_log, ssd_d=ssd_d, ssd_norm=ssd_norm, gdn_conv_w=gdn_conv_w,
             gdn_a_log=gdn_a_log, gdn_dt_bias=gdn_dt_bias, gdn_norm=gdn_norm, w_in_odd=w_in_odd, w_out_odd=w_out_odd,
             mla_q_norm=mla_q_norm, mla_w_uq=mla_w_uq, mla_kv_norm=mla_kv_norm, mla_w_uk=mla_w_uk, mla_w_uv=mla_w_uv,
             hg_norm=hg_norm, xa_w_q=xa_w_q, xa_w_k=xa_w_k, xa_w_v=xa_w_v, xa_w_o=xa_w_o, ffn_w1=ffn_w1,
             ffn_w2=ffn_w2)
    wb = _prep_weights(w)
    depth = norm_g.shape[0]
    n_even, n_odd = (depth + 1) // 2, depth // 2
    lbs = jnp.cumsum(jax.nn.softmax(hg_lower_bound.astype(F32), axis=0), axis=0)
    lbs = lbs - lbs[0]

    b, m, d = mem_prompt.shape
    xa_dim = XA_HEADS * XA_HEAD_DIM
    mk, mv = [], []
    for layer in range(depth):
        mn = _rms(mem_prompt.reshape(b * m, d), xa_mem_norm[layer], BF16)
        mk.append(_matmul([mn], [wb["xa_w_k"][layer]], F32).reshape(b, m, xa_dim))
        mv.append(_matmul([mn], [wb["xa_w_v"][layer]], F32).reshape(b, m, xa_dim))
    p_mem_k = jnp.stack(mk)
    p_mem_v = jnp.stack(mv)
    dt = x_prompt.dtype
    prompt = _forward(
        x_prompt, p_mem_k.astype(BF16), p_mem_v.astype(BF16),
        jnp.zeros((n_even, b, SSD_HEADS, SSD_HEAD_DIM, SSD_STATE), dt),
        jnp.zeros((n_even, b, CONV_W - 1, SSD_CONV_DIM), dt),
        jnp.zeros((n_even, b, GDN_HEADS, GDN_DK, GDN_DV), dt),
        jnp.zeros((n_even, b, CONV_W - 1, GDN_CONV_DIM), dt),
        jnp.zeros((n_odd, b, 0, MLA_KV_RANK), dt),
        jnp.zeros((n_odd, b, 0, MLA_ROPE), dt),
        jnp.zeros((n_odd, b, HG_HEADS, HG_DK, HG_DK), dt),
        w, wb, lbs)
    db = x_sample.shape[0]
    sample = _forward(
        x_sample, cache_mem_k.reshape(depth, db, m, xa_dim).astype(BF16),
        cache_mem_v.reshape(depth, db, m, xa_dim).astype(BF16),
        state_ssd, state_ssd_conv, state_gdn, state_gdn_conv, cache_mla_ckv, cache_mla_krope, state_hgrn,
        w, wb, lbs)
    y_prompt, p_rest = prompt[0], prompt[1:]
    y_sample, s_rest = sample[0], sample[1:]
    return (y_prompt, y_sample, *p_rest,
            p_mem_k.reshape(depth, b, m, XA_HEADS, XA_HEAD_DIM), p_mem_v.reshape(depth, b, m, XA_HEADS, XA_HEAD_DIM),
            *s_rest)
```

```python
import functools

import numpy as np
import jax
import jax.numpy as jnp
from jax import lax
from jax.experimental import pallas as pl
from jax.experimental.pallas import tpu as pltpu

F32 = jnp.float32
BF16 = jnp.bfloat16

VMEM_LIMIT_BYTES = 56 * 1024 * 1024
LANES = 128
NORM_EPS = 1e-6
CHUNK = 64
CHUNK_SHIFT = 6
assert 1 << CHUNK_SHIFT == CHUNK
CONV_W = 4
ROPE_THETA = 10000.0
MASKED = -1e30


def _params(*sem):
    return pltpu.CompilerParams(dimension_semantics=sem, vmem_limit_bytes=VMEM_LIMIT_BYTES)


def _tile(n, cap, mult):
    if n <= cap:
        return n
    for d in range(cap - cap % mult, 0, -mult):
        if n % d == 0:
            return d
    return n


def _dot(a, b):
    return jnp.dot(a, b, preferred_element_type=F32)


def _dot_nt(a, b):
    return lax.dot_general(a, b, (((1,), (1,)), ((), ())), preferred_element_type=F32)


def _dot_tn(a, b):
    return lax.dot_general(a, b, (((0,), (0,)), ((), ())), preferred_element_type=F32)


def _split3(x):
    hi = x.astype(BF16)
    r = x - hi.astype(F32)
    mid = r.astype(BF16)
    lo = (r - mid.astype(F32)).astype(BF16)
    return hi, mid, lo


def _mask_dot(mask3, x):
    return _dot(mask3, jnp.concatenate(_split3(x), axis=0))


def _mask_dot_nt(mask16, x):
    hi, mid, lo = _split3(x)
    return _dot_nt(mask16, hi) + (_dot_nt(mask16, mid) + _dot_nt(mask16, lo))


def _dot_x3(a, b):
    a_hi = a.astype(BF16)
    a_lo = (a - a_hi.astype(F32)).astype(BF16)
    b_hi = b.astype(BF16)
    b_lo = (b - b_hi.astype(F32)).astype(BF16)
    return _dot(a_hi, b_hi) + (_dot(a_hi, b_lo) + _dot(a_lo, b_hi))


def _softplus(x):
    return jnp.maximum(x, 0.0) + jnp.log1p(jnp.exp(-jnp.abs(x)))


def _sigmoid(x):
    return 0.5 * jnp.tanh(0.5 * x) + 0.5


def _silu(x):
    return x * _sigmoid(x)


def _pick_lane(a, idx):
    lane = lax.broadcasted_iota(jnp.int32, a.shape, 1)
    return jnp.sum(jnp.where(lane == idx, a, 0.0), axis=1, keepdims=True)


def _mm_body(*refs, n_ops, nks, act):
    a_refs, b_refs = refs[:n_ops], refs[n_ops:2 * n_ops]
    o_ref, acc_ref = refs[2 * n_ops], refs[2 * n_ops + 1]
    k = pl.program_id(2)
    nk = sum(nks)

    @pl.when(k == 0)
    def _():
        acc_ref[...] = jnp.zeros_like(acc_ref)

    off = 0
    for a_ref, b_ref, n in zip(a_refs, b_refs, nks):
        def step(a_ref=a_ref, b_ref=b_ref):
            acc_ref[...] += _dot(a_ref[...], b_ref[...])
        if n_ops == 1:
            step()
        else:
            pl.when((k >= off) & (k < off + n))(step)
        off += n

    @pl.when(k == nk - 1)
    def _():
        r = acc_ref[...]
        if act == "relu2":
            r = jnp.square(jnp.maximum(r, 0.0))
        o_ref[...] = r.astype(o_ref.dtype)


def _mm_full_k_body(a_ref, b_ref, o_ref, *, act):
    r = _dot(a_ref[...], b_ref[...])
    if act == "relu2":
        r = jnp.square(jnp.maximum(r, 0.0))
    o_ref[...] = r.astype(o_ref.dtype)


FULL_K_MAX = 4096
FULL_K_B_BLOCK_BYTES = 4 * 1024 * 1024


def _matmul(a_list, b_list, out_dtype, act=None, b_rows=None, tm_cap=1024, tn_cap=None, tk_cap=1024):
    m = a_list[0].shape[0]
    n = b_list[0].shape[1]
    b_rows = [0] * len(a_list) if b_rows is None else b_rows
    tm = _tile(m, tm_cap, 8)
    k0 = a_list[0].shape[1]
    if len(a_list) == 1 and k0 <= FULL_K_MAX and b_rows[0] == 0 and b_list[0].shape[0] == k0:
        cap = tn_cap or min(2048, max(512, FULL_K_B_BLOCK_BYTES // (2 * k0)))
        tn = _tile(n, cap, LANES)
        return pl.pallas_call(
            functools.partial(_mm_full_k_body, act=act),
            grid=(m // tm, n // tn),
            in_specs=[pl.BlockSpec((tm, k0), lambda i, j: (i, 0)),
                      pl.BlockSpec((k0, tn), lambda i, j: (0, j))],
            out_specs=pl.BlockSpec((tm, tn), lambda i, j: (i, j)),
            out_shape=jax.ShapeDtypeStruct((m, n), out_dtype),
            compiler_params=_params("parallel", "arbitrary"),
            name="matmul_full_k",
        )(a_list[0], b_list[0])
    tn = _tile(n, tn_cap or (2048 if len(a_list) == 1 else 1024), LANES)
    tk = min(_tile(a.shape[1], tk_cap, LANES) for a in a_list)
    nks, offs = [], []
    for a, b, r0 in zip(a_list, b_list, b_rows):
        assert a.shape[0] == m and b.shape[1] == n and a.shape[1] % tk == 0 and r0 % tk == 0
        assert r0 + a.shape[1] <= b.shape[0]
        offs.append(sum(nks))
        nks.append(a.shape[1] // tk)
    in_specs = []
    for o, nki in zip(offs, nks):
        in_specs.append(pl.BlockSpec((tm, tk), lambda i, j, k, o=o, nki=nki: (i, jnp.clip(k - o, 0, nki - 1))))
    for o, nki, r0 in zip(offs, nks, b_rows):
        in_specs.append(pl.BlockSpec(
            (tk, tn), lambda i, j, k, o=o, nki=nki, rb=r0 // tk: (rb + jnp.clip(k - o, 0, nki - 1), j)))
    return pl.pallas_call(
        functools.partial(_mm_body, n_ops=len(a_list), nks=tuple(nks), act=act),
        grid=(m // tm, n // tn, sum(nks)),
        in_specs=in_specs,
        out_specs=pl.BlockSpec((tm, tn), lambda i, j, k: (i, j)),
        out_shape=jax.ShapeDtypeStruct((m, n), out_dtype),
        scratch_shapes=[pltpu.VMEM((tm, tn), F32)],
        compiler_params=_params("parallel", "parallel", "arbitrary"),
        name="matmul",
    )(*a_list, *b_list)


def _head_mm_body(a_ref, b_ref, o_ref):
    o_ref[...] = _dot(a_ref[...], b_ref[...]).astype(o_ref.dtype)


def _head_split_body(a_ref, b_ref, rot_ref, cos_ref, sin_ref, o_ref, *, scale):
    h = pl.program_id(1)
    n = b_ref.shape[1]
    p = cos_ref.shape[1]
    o_ref[:, 0:n] = (_dot(a_ref[...].astype(BF16), b_ref[...]) * scale).astype(o_ref.dtype)
    pair = rot_ref[...]
    x = jnp.where(h % 2 == 0, pair[:, 0:p], pair[:, p:2 * p])
    swapped = jnp.concatenate([x[:, p // 2:], x[:, :p // 2]], axis=1)
    o_ref[:, n:n + p] = ((x * cos_ref[...] + swapped * sin_ref[...]) * scale).astype(o_ref.dtype)
    o_ref[:, n + p:] = jnp.zeros((o_ref.shape[0], o_ref.shape[1] - n - p), o_ref.dtype)


def _head_split_matmul(a, rot_col0, w, scale, cos2, sin2, tm_cap=1024):
    t = a.shape[0]
    nh, dk, n = w.shape
    p = cos2.shape[1]
    assert n % LANES == 0 and 2 * p == LANES and nh % 2 == 0 and rot_col0 % LANES == 0
    tm = _tile(t, tm_cap, 8)
    return pl.pallas_call(
        functools.partial(_head_split_body, scale=scale),
        grid=(t // tm, nh),
        in_specs=[pl.BlockSpec((tm, dk), lambda i, h: (i, h)),
                  pl.BlockSpec((None, dk, n), lambda i, h: (h, 0, 0)),
                  pl.BlockSpec((tm, LANES), lambda i, h: (i, rot_col0 // LANES + h // 2)),
                  pl.BlockSpec((tm, p), lambda i, h: (i, 0)),
                  pl.BlockSpec((tm, p), lambda i, h: (i, 0))],
        out_specs=pl.BlockSpec((None, tm, n + LANES), lambda i, h: (h, i, 0)),
        out_shape=jax.ShapeDtypeStruct((nh, t, n + LANES), BF16),
        compiler_params=_params("parallel", "arbitrary"),
        name="head_split_matmul",
    )(a, w, a, cos2, sin2)


def _head_merge_matmul(a, w, tm_cap=1024):
    nh, t, k = a.shape
    dv = w.shape[2]
    tm = _tile(t, tm_cap, 8)
    return pl.pallas_call(
        _head_mm_body,
        grid=(t // tm, nh),
        in_specs=[pl.BlockSpec((None, tm, k), lambda i, h: (h, i, 0)),
                  pl.BlockSpec((None, k, dv), lambda i, h: (h, 0, 0))],
        out_specs=pl.BlockSpec((tm, dv), lambda i, h: (i, h)),
        out_shape=jax.ShapeDtypeStruct((t, nh * dv), BF16),
        compiler_params=_params("parallel", "arbitrary"),
        name="head_merge_matmul",
    )(a, w)


def _rms_rows(x, g):
    return x * lax.rsqrt(jnp.mean(x * x, axis=-1, keepdims=True) + NORM_EPS) * g


def _rms_body(x_ref, g_ref, o_ref):
    o_ref[...] = _rms_rows(x_ref[...], g_ref[...]).astype(o_ref.dtype)


def _rms(x, g, out_dtype, col_block=0, width=None, tr_cap=256):
    t = x.shape[0]
    width = x.shape[1] if width is None else width
    tr = _tile(t, tr_cap, 8)
    return pl.pallas_call(
        _rms_body,
        grid=(t // tr,),
        in_specs=[pl.BlockSpec((tr, width), lambda i: (i, col_block)),
                  pl.BlockSpec((1, width), lambda i: (0, 0))],
        out_specs=pl.BlockSpec((tr, width), lambda i: (i, 0)),
        out_shape=jax.ShapeDtypeStruct((t, width), out_dtype),
        compiler_params=_params("parallel"),
        name="rms",
    )(x, g.reshape(1, width))


def _add_rms_body(x_ref, y_ref, g1_ref, g2_ref, xo_ref, ho_ref):
    xn = x_ref[...] + _rms_rows(y_ref[...].astype(F32), g1_ref[...])
    xo_ref[...] = xn
    ho_ref[...] = _rms_rows(xn, g2_ref[...]).astype(ho_ref.dtype)


def _add_rms_last_body(x_ref, y_ref, g1_ref, xo_ref):
    xo_ref[...] = x_ref[...] + _rms_rows(y_ref[...].astype(F32), g1_ref[...])


def _add_rms(x, y, g1, g2, tr_cap=256):
    t, d = x.shape
    tr = _tile(t, tr_cap, 8)
    row = pl.BlockSpec((tr, d), lambda i: (i, 0))
    gain = pl.BlockSpec((1, d), lambda i: (0, 0))
    if g2 is None:
        out = pl.pallas_call(
            _add_rms_last_body, grid=(t // tr,), in_specs=[row, row, gain], out_specs=row,
            out_shape=jax.ShapeDtypeStruct((t, d), F32), compiler_params=_params("parallel"),
            name="add_rms_last",
        )(x, y, g1.reshape(1, d))
        return out, None
    return pl.pallas_call(
        _add_rms_body, grid=(t // tr,), in_specs=[row, row, gain, gain], out_specs=[row, row],
        out_shape=[jax.ShapeDtypeStruct((t, d), F32), jax.ShapeDtypeStruct((t, d), BF16)],
        compiler_params=_params("parallel"), name="add_rms",
    )(x, y, g1.reshape(1, d), g2.reshape(1, d))


CONV_HALO = 8


def _conv_taps(x, w, b):
    acc = b + x * w[CONV_W - 1:CONV_W, :]
    for j in range(1, CONV_W):
        acc = acc + pltpu.roll(x, j, 0) * w[CONV_W - 1 - j:CONV_W - j, :]
    return acc


def _conv_body(u_ref, buf_ref, w_ref, b_ref, o_ref, hist_ref, *, tl):
    lt = pl.program_id(2)

    @pl.when(lt == 0)
    def _():
        hist_ref[...] = jnp.zeros_like(hist_ref)
        hist_ref[CONV_HALO - (CONV_W - 1):CONV_HALO, :] = buf_ref[0]

    u = u_ref[...]
    w = w_ref[...]
    b = b_ref[...]
    o_ref[...] = _silu(_conv_taps(u, w, b))
    head = jnp.concatenate([hist_ref[...], u[0:CONV_HALO, :]], axis=0)
    o_ref[0:CONV_HALO, :] = _silu(_conv_taps(head, w, b)[CONV_HALO:, :])
    hist_ref[...] = u[tl - CONV_HALO:tl, :]


def _conv_silu(u, col0, c, buf, w, b, bsz, l, tl_cap=512, tc=1024):
    assert l >= CONV_W - 1 and col0 % tc == 0 and c % tc == 0
    tl = _tile(l, tl_cap, 8)
    assert tl >= CONV_HALO
    nl = l // tl
    cb0 = col0 // tc
    return pl.pallas_call(
        functools.partial(_conv_body, tl=tl),
        grid=(bsz, c // tc, nl),
        in_specs=[pl.BlockSpec((tl, tc), lambda bi, ci, li: (bi * nl + li, cb0 + ci)),
                  pl.BlockSpec((1, CONV_W - 1, tc), lambda bi, ci, li: (bi, 0, ci)),
                  pl.BlockSpec((CONV_W, tc), lambda bi, ci, li: (0, ci)),
                  pl.BlockSpec((1, tc), lambda bi, ci, li: (0, ci))],
        out_specs=pl.BlockSpec((tl, tc), lambda bi, ci, li: (bi * nl + li, ci)),
        out_shape=jax.ShapeDtypeStruct((bsz * l, c), F32),
        scratch_shapes=[pltpu.VMEM((CONV_HALO, tc), F32)],
        compiler_params=_params("parallel", "parallel", "arbitrary"),
        name="conv_silu",
    )(u, buf, w, b.reshape(1, c))


def _tri_masks(n):
    r = np.arange(n)
    tril = (r[:, None] >= r[None, :]).astype(np.float32)
    return (jnp.asarray(tril, dtype=BF16), jnp.asarray(np.tile(tril, (1, 3)), dtype=BF16),
            jnp.asarray(np.eye(n, dtype=np.float32), dtype=BF16))


def _ssd_body(x_ref, bm_ref, cm_ref, z_ref, sm_ref, dtb_ref, alog_ref, d_ref, gn_ref, s0_ref, tril_ref, tril3_ref,
              eye_ref, y_ref, so_ref, s_scr, cst_scr, *, nc, hpg, hd, nheads, gps):
    g0 = pl.program_id(1) * gps
    c = pl.program_id(2)
    lc = x_ref.shape[0]
    gw = hpg * hd
    nstate = bm_ref.shape[1] // gps

    @pl.when(c == 0)
    def _():
        s_scr[...] = s0_ref[0]

    dt = _softplus(sm_ref[:, 0:nheads] + dtb_ref[...])
    da = dt * (-jnp.exp(alog_ref[...]))
    cs = _mask_dot(tril3_ref[...], da)
    cst_scr[...] = _mask_dot_nt(eye_ref[...], cs)
    tri = tril_ref[...].astype(F32) > 0.0

    x = x_ref[...]
    gs = range(gps)
    bm = [bm_ref[:, gi * nstate:(gi + 1) * nstate].astype(BF16) for gi in gs]
    cm = [cm_ref[:, gi * nstate:(gi + 1) * nstate].astype(BF16) for gi in gs]
    s = [s_scr[gi * gw:(gi + 1) * gw, :] for gi in gs]
    cb = [_dot_nt(cm[gi], bm[gi]) for gi in gs]
    y_state = [_dot_nt(cm[gi], s[gi].astype(BF16)) for gi in gs]
    js = range(gps * hpg)
    hd_ids = [g0 * hpg + j for j in js]
    cs_col = [_pick_lane(cs, h) for h in hd_ids]
    dt_col = [_pick_lane(dt, h) for h in hd_ids]
    d_h = [_pick_lane(d_ref[...], h) for h in hd_ids]
    cs_row = [cst_scr[pl.ds(h, 1), :] for h in hd_ids]
    cs_last = [cs_row[j][:, lc - 1:lc] for j in js]
    mix = [(cb[j // hpg] * jnp.exp(jnp.where(tri, cs_col[j] - cs_row[j], MASKED))).astype(BF16) for j in js]
    xh = [x[:, j * hd:(j + 1) * hd] for j in js]
    xdt = [xh[j] * dt_col[j] for j in js]
    y_in = [_dot(mix[j], xdt[j].astype(BF16)) for j in js]
    ys = [y_in[j] + y_state[j // hpg][:, (j % hpg) * hd:(j % hpg + 1) * hd] * jnp.exp(cs_col[j]) + d_h[j] * xh[j]
          for j in js]
    xws = [xdt[j] * jnp.exp(cs_last[j] - cs_col[j]) for j in js]
    decs = [jnp.broadcast_to(jnp.exp(cs_last[j]), (hd, nstate)) for j in js]
    for gi in gs:
        hsl = slice(gi * hpg, (gi + 1) * hpg)
        xw = jnp.concatenate(xws[hsl], axis=1)
        s_new = s[gi] * jnp.concatenate(decs[hsl], axis=0) + _dot_tn(xw.astype(BF16), bm[gi])
        s_scr[gi * gw:(gi + 1) * gw, :] = s_new
        y = jnp.concatenate(ys[hsl], axis=1) * _silu(z_ref[:, gi * gw:(gi + 1) * gw])
        y_ref[:, gi * gw:(gi + 1) * gw] = _rms_rows(y, gn_ref[:, gi * gw:(gi + 1) * gw]).astype(y_ref.dtype)

    @pl.when(c == nc - 1)
    def _():
        so_ref[0] = s_scr[...]


SSD_GROUPS_PER_STEP = 4


def _ssd(xbc, pm, z_col0, sm, dt_bias, a_log, d, gn, s0, bsz, l, nheads, hd, ngroups, nstate):
    t = bsz * l
    lc = CHUNK
    nc = l // lc
    hpg = nheads // ngroups
    gps = SSD_GROUPS_PER_STEP
    gw = gps * hpg * hd
    sw = gps * nstate
    inner = nheads * hd
    assert l % lc == 0 and nheads == lc and z_col0 % gw == 0 and ngroups % gps == 0 and inner % sw == 0
    tril, tril3, eye = _tri_masks(lc)
    rows = lambda bi, gi, ci: bi * nc + ci
    const2 = lambda bi, gi, ci: (0, 0)
    y, s_new = pl.pallas_call(
        functools.partial(_ssd_body, nc=nc, hpg=hpg, hd=hd, nheads=nheads, gps=gps),
        grid=(bsz, ngroups // gps, nc),
        in_specs=[pl.BlockSpec((lc, gw), lambda bi, gi, ci: (rows(bi, gi, ci), gi)),
                  pl.BlockSpec((lc, sw), lambda bi, gi, ci: (rows(bi, gi, ci), inner // sw + gi)),
                  pl.BlockSpec((lc, sw), lambda bi, gi, ci: (rows(bi, gi, ci), (inner + ngroups * nstate) // sw + gi)),
                  pl.BlockSpec((lc, gw), lambda bi, gi, ci: (rows(bi, gi, ci), z_col0 // gw + gi)),
                  pl.BlockSpec((lc, sm.shape[1]), lambda bi, gi, ci: (rows(bi, gi, ci), 0)),
                  pl.BlockSpec((1, nheads), const2),
                  pl.BlockSpec((1, nheads), const2),
                  pl.BlockSpec((1, nheads), const2),
                  pl.BlockSpec((1, gw), lambda bi, gi, ci: (0, gi)),
                  pl.BlockSpec((1, gw, nstate), lambda bi, gi, ci: (bi, gi, 0)),
                  pl.BlockSpec((lc, lc), const2),
                  pl.BlockSpec((lc, 3 * lc), const2),
                  pl.BlockSpec((lc, lc), const2)],
        out_specs=[pl.BlockSpec((lc, gw), lambda bi, gi, ci: (rows(bi, gi, ci), gi)),
                   pl.BlockSpec((1, gw, nstate), lambda bi, gi, ci: (bi, gi, 0))],
        out_shape=[jax.ShapeDtypeStruct((t, inner), BF16),
                   jax.ShapeDtypeStruct((bsz, inner, nstate), F32)],
        scratch_shapes=[pltpu.VMEM((gw, nstate), F32), pltpu.VMEM((nheads, lc), F32)],
        compiler_params=_params("parallel", "parallel", "arbitrary"),
        name="ssd_scan",
    )(xbc, xbc, xbc, pm, sm, dt_bias.reshape(1, nheads), a_log.reshape(1, nheads), d.reshape(1, nheads),
      gn.reshape(1, inner), s0, tril, tril3, eye)
    return y, s_new


def _gdn_heads(qs, ks, vs, gates, betas, cs_cols, cs_rows, states, tri, eye, gn):
    n = len(qs)
    hd = range(n)
    lc, dk = qs[0].shape
    dv = vs[0].shape[1]
    cs_last = [cs_rows[i][:, lc - 1:lc] for i in hd]
    decay = [jnp.exp(jnp.where(tri, cs_cols[i] - cs_rows[i], MASKED)) for i in hd]
    qn = [qs[i] * (lax.rsqrt(jnp.sum(qs[i] * qs[i], axis=1, keepdims=True) + 1e-6) * dk ** -0.5) for i in hd]
    kn = [ks[i] * lax.rsqrt(jnp.sum(ks[i] * ks[i], axis=1, keepdims=True) + 1e-6) for i in hd]
    kb = [kn[i] * betas[i] for i in hd]
    kn16 = [kn[i].astype(BF16) for i in hd]
    off_diag = 1.0 - eye
    a = [_dot_nt(kb[i].astype(BF16), kn16[i]) * decay[i] * off_diag for i in hd]
    qk = [_dot_nt(qn[i].astype(BF16), kn16[i]) * decay[i] for i in hd]
    s16 = [states[i].astype(BF16) for i in hd]
    o_state = [_dot((qn[i] * jnp.exp(cs_cols[i])).astype(BF16), s16[i]) for i in hd]
    tinv = [eye - a[i] for i in hd]
    p = [_dot_x3(a[i], a[i]) for i in hd]
    steps = max(1, int(np.ceil(np.log2(lc))) - 1)
    for st in range(steps):
        tinv = [tinv[i] + _dot_x3(tinv[i], p[i]) for i in hd]
        if st + 1 < steps:
            p = [_dot_x3(p[i], p[i]) for i in hd]
    rhs = [jnp.concatenate([vs[i] * betas[i], kb[i] * jnp.exp(cs_cols[i])], axis=1).astype(BF16) for i in hd]
    sol = [_dot(tinv[i].astype(BF16), rhs[i]) for i in hd]
    vn16 = [(sol[i][:, :dv] - _dot(sol[i][:, dv:].astype(BF16), s16[i])).astype(BF16) for i in hd]
    o = [o_state[i] + _dot(qk[i].astype(BF16), vn16[i]) for i in hd]
    k_end = [(kn[i] * jnp.exp(cs_last[i] - cs_cols[i])).astype(BF16) for i in hd]
    s_new = [states[i] * jnp.exp(cs_last[i]) + _dot_tn(k_end[i], vn16[i]) for i in hd]
    outs = [_rms_rows(o[i], gn) * _silu(gates[i]) for i in hd]
    return outs, s_new


def _gdn_body(q_ref, k_ref, v_ref, gate_ref, sm_ref, alog_ref, dtb_ref, gn_ref, s0_ref, tril_ref, tril3_ref, eye_ref,
              eyeh_ref, o_ref, so_ref, s_scr, cst_scr, *, nc, nheads, a_col0, hps):
    hb = pl.program_id(1)
    c = pl.program_id(2)
    d = q_ref.shape[1] // hps

    @pl.when(c == 0)
    def _():
        s_scr[...] = s0_ref[0]

    a_raw = sm_ref[:, a_col0:a_col0 + nheads]
    b_raw = sm_ref[:, a_col0 + nheads:a_col0 + 2 * nheads]
    gl = -jnp.exp(alog_ref[...]) * _softplus(a_raw + dtb_ref[...])
    beta_all = _sigmoid(b_raw)
    cs = _mask_dot(tril3_ref[...], gl)
    cst_scr[...] = _mask_dot_nt(eyeh_ref[...], cs)
    tri = tril_ref[...].astype(F32) > 0.0
    eye = eye_ref[...].astype(F32)
    gn = gn_ref[...]
    sls = [slice(j * d, (j + 1) * d) for j in range(hps)]
    hidx = [hb * hps + j for j in range(hps)]
    outs, s_new = _gdn_heads(
        [q_ref[:, sl] for sl in sls], [k_ref[:, sl] for sl in sls], [v_ref[:, sl] for sl in sls],
        [gate_ref[:, sl] for sl in sls], [_pick_lane(beta_all, h) for h in hidx], [_pick_lane(cs, h) for h in hidx],
        [cst_scr[pl.ds(h, 1), :] for h in hidx], [s_scr[j] for j in range(hps)], tri, eye, gn)
    for j in range(hps):
        s_scr[j] = s_new[j]
        o_ref[:, sls[j]] = outs[j].astype(o_ref.dtype)

    @pl.when(c == nc - 1)
    def _():
        so_ref[0] = s_scr[...]


GDN_HEADS_PER_STEP = 16


def _gdn(qkv, pm, gate_col0, sm, a_col0, a_log, dt_bias, gn, s0, bsz, l, nheads, dk, dv):
    hps = GDN_HEADS_PER_STEP
    w = hps * dv
    assert dk == dv and gate_col0 % w == 0 and nheads % hps == 0
    t = bsz * l
    lc = CHUNK
    nc = l // lc
    nhb = nheads // hps
    tril, tril3, eye = _tri_masks(lc)
    eyeh = jnp.asarray(np.eye(nheads, dtype=np.float32), dtype=BF16)
    rows = lambda bi, hi, ci: bi * nc + ci
    const2 = lambda bi, hi, ci: (0, 0)
    col = lambda k: pl.BlockSpec((lc, w), lambda bi, hi, ci: (rows(bi, hi, ci), k * nhb + hi))
    state = pl.BlockSpec((1, hps, dk, dv), lambda bi, hi, ci: (bi, hi, 0, 0))
    o, s_new = pl.pallas_call(
        functools.partial(_gdn_body, nc=nc, nheads=nheads, a_col0=a_col0, hps=hps),
        grid=(bsz, nhb, nc),
        in_specs=[col(0), col(1), col(2),
                  pl.BlockSpec((lc, w), lambda bi, hi, ci: (rows(bi, hi, ci), gate_col0 // w + hi)),
                  pl.BlockSpec((lc, sm.shape[1]), lambda bi, hi, ci: (rows(bi, hi, ci), 0)),
                  pl.BlockSpec((1, nheads), const2),
                  pl.BlockSpec((1, nheads), const2),
                  pl.BlockSpec((1, dv), const2),
                  state,
                  pl.BlockSpec((lc, lc), const2),
                  pl.BlockSpec((lc, 3 * lc), const2),
                  pl.BlockSpec((lc, lc), const2),
                  pl.BlockSpec((nheads, nheads), const2)],
        out_specs=[pl.BlockSpec((lc, w), lambda bi, hi, ci: (rows(bi, hi, ci), hi)), state],
        out_shape=[jax.ShapeDtypeStruct((t, nheads * dv), BF16),
                   jax.ShapeDtypeStruct((bsz, nheads, dk, dv), F32)],
        scratch_shapes=[pltpu.VMEM((hps, dk, dv), F32), pltpu.VMEM((nheads, lc), F32)],
        compiler_params=_params("parallel", "parallel", "arbitrary"),
        name="gdn_scan",
    )(qkv, qkv, qkv, pm, sm, a_log.reshape(1, nheads), dt_bias.reshape(1, nheads), gn.reshape(1, dv), s0,
      tril, tril3, eye, eyeh)
    return o, s_new


def _hgrn_levels(lc):
    hs = []
    h = lc // 2
    while h >= 1:
        hs.append(h)
        h //= 2
    return hs


def _hgrn_masks(lc):
    r = np.arange(lc)
    t, u = r[:, None], r[None, :]
    mats = [u <= t, u > t]
    for h in _hgrn_levels(lc):
        start = (t // h) * h
        mats.append((u >= start) & (u <= t))
        mats.append((u > t) & (u <= start + h - 1))
    return jnp.asarray(np.tile(np.concatenate(mats, axis=0).astype(np.float32), (1, 3)), dtype=BF16)


def _hgrn_heads(qs, hfs, vs, gates, lbs, gn, sts, mst16):
    n = len(qs)
    hd = range(n)
    lc, d = qs[0].shape
    logf = [jnp.log(lbs[i] + (1.0 - lbs[i]) * _sigmoid(hfs[i])) for i in hd]
    kk = [(1.0 - lbs[i]) * _sigmoid(-hfs[i]) for i in hd]
    cums = [_mask_dot(mst16, logf[i]) for i in hd]
    v16 = [vs[i].astype(BF16) for i in hd]
    st16 = [sts[i].astype(BF16) for i in hd]
    o_state = [_dot_nt((qs[i] * jnp.exp(cums[i][0:lc])).astype(BF16), st16[i]) for i in hd]

    row = lax.broadcasted_iota(jnp.int32, (lc, 1), 0)
    ti = lax.broadcasted_iota(jnp.int32, (lc, lc), 0)
    si = lax.broadcasted_iota(jnp.int32, (lc, lc), 1)
    att = [jnp.where(ti == si, jnp.sum(qs[i] * kk[i], axis=1, keepdims=True), 0.0) for i in hd]
    for li, hs in enumerate(_hgrn_levels(lc)):
        sh = int(np.log2(hs))
        upper = ((row >> sh) & 1) == 1
        same_block = (ti >> (sh + 1)) == (si >> (sh + 1))
        for i in hd:
            seg = cums[i][(2 + 2 * li) * lc:(3 + 2 * li) * lc]
            rest = cums[i][(3 + 2 * li) * lc:(4 + 2 * li) * lc]
            ql = jnp.where(upper, qs[i] * jnp.exp(seg), 0.0)
            kl = jnp.where(upper, 0.0, kk[i] * jnp.exp(rest))
            att[i] = att[i] + jnp.where(same_block, _dot_nt(ql.astype(BF16), kl.astype(BF16)), 0.0)

    outs, st_new = [], []
    for i in hd:
        o = o_state[i] + _dot(att[i].astype(BF16), v16[i])
        k_end = kk[i] * jnp.exp(cums[i][lc:2 * lc])
        st_new.append(sts[i] * jnp.exp(cums[i][lc - 1:lc, :]) + _dot_tn(v16[i], k_end.astype(BF16)))
        outs.append(_rms_rows(o, gn) * _silu(gates[i]))
    return outs, st_new


def _hgrn_body(q_ref, f_ref, i_ref, gate_ref, lb_ref, gn_ref, s0_ref, mst_ref, o_ref, so_ref, st_scr, *, nc, hps):
    c = pl.program_id(2)
    d = q_ref.shape[1] // hps

    @pl.when(c == 0)
    def _():
        for j in range(hps):
            st_scr[j] = s0_ref[0, j].T

    mst16 = mst_ref[...]
    gn = gn_ref[...]
    sls = [slice(j * d, (j + 1) * d) for j in range(hps)]
    outs, st_new = _hgrn_heads(
        [q_ref[:, sl] for sl in sls], [f_ref[:, sl] for sl in sls], [i_ref[:, sl] for sl in sls],
        [gate_ref[:, sl] for sl in sls], [lb_ref[:, sl] for sl in sls], gn, [st_scr[j] for j in range(hps)], mst16)
    for j in range(hps):
        st_scr[j] = st_new[j]
        o_ref[:, sls[j]] = outs[j].astype(o_ref.dtype)

    @pl.when(c == nc - 1)
    def _():
        for j in range(hps):
            so_ref[0, j] = st_scr[j].T


HGRN_HEADS_PER_STEP = 16


def _hgrn(pm, lb, gn, s0, bsz, l, nheads, dk):
    hps = HGRN_HEADS_PER_STEP
    assert nheads % hps == 0
    w = hps * dk
    nhb = nheads // hps
    t = bsz * l
    lc = CHUNK
    nc = l // lc
    mst = _hgrn_masks(lc)
    rows = lambda bi, hi, ci: bi * nc + ci
    const2 = lambda bi, hi, ci: (0, 0)
    col = lambda k: pl.BlockSpec((lc, w), lambda bi, hi, ci: (rows(bi, hi, ci), k * nhb + hi))
    state = pl.BlockSpec((1, hps, dk, dk), lambda bi, hi, ci: (bi, hi, 0, 0))
    o, s_new = pl.pallas_call(
        functools.partial(_hgrn_body, nc=nc, hps=hps),
        grid=(bsz, nhb, nc),
        in_specs=[col(0), col(1), col(2), col(3),
                  pl.BlockSpec((1, w), lambda bi, hi, ci: (0, hi)),
                  pl.BlockSpec((1, dk), const2),
                  state,
                  pl.BlockSpec(mst.shape, const2)],
        out_specs=[pl.BlockSpec((lc, w), lambda bi, hi, ci: (rows(bi, hi, ci), hi)), state],
        out_shape=[jax.ShapeDtypeStruct((t, nheads * dk), BF16),
                   jax.ShapeDtypeStruct((bsz, nheads, dk, dk), F32)],
        scratch_shapes=[pltpu.VMEM((hps, dk, dk), F32)],
        compiler_params=_params("parallel", "parallel", "arbitrary"),
        name="hgrn_scan",
    )(pm, pm, pm, pm, lb.reshape(1, nheads * dk), gn.reshape(1, dk), s0, mst)
    return o, s_new


MLA_SUB_ROWS = 1024


def _mla_body(qi_ref, ki_ref, last_ref, q_ref, k_ref, o_ref, m_scr, l_scr, acc_scr, *, tq, tk, t_past, hs):
    step = pl.program_id(1)
    iq = qi_ref[step]
    ikv = ki_ref[step]
    nh, _, dq = q_ref.shape
    r = o_ref.shape[2]
    rows = hs * tq
    first_limit = t_past + iq * tq + CHUNK

    @pl.when(ikv == 0)
    def _():
        m_scr[...] = jnp.full_like(m_scr, MASKED)
        l_scr[...] = jnp.zeros_like(l_scr)
        acc_scr[...] = jnp.zeros_like(acc_scr)

    def update(masked):
        kc = k_ref[0]
        vals = kc[:, 0:r]
        if masked:
            tok = lax.broadcasted_iota(jnp.int32, (rows, 1), 0) & (tq - 1)
            limit = t_past + iq * tq + ((tok >> CHUNK_SHIFT) + 1) * CHUNK
            visible = (ikv * tk + lax.broadcasted_iota(jnp.int32, (1, tk), 1)) < limit

        def scores(g):
            return _dot_nt(q_ref[g * hs:(g + 1) * hs].reshape(rows, dq), kc)

        ngroups = nh // hs
        s_next = scores(0)
        for g in range(ngroups):
            rsl = slice(g * rows, (g + 1) * rows)
            s = s_next
            if g + 1 < ngroups:
                s_next = scores(g + 1)
            if masked:
                s = jnp.where(visible, s, MASKED)
            m_old = m_scr[rsl]
            m_new = jnp.maximum(m_old, jnp.max(s, axis=1, keepdims=True))
            alpha = jnp.exp(m_old - m_new)
            p = jnp.exp(s - m_new)
            l_scr[rsl] = alpha * l_scr[rsl] + jnp.sum(p, axis=1, keepdims=True)
            acc_scr[rsl] = alpha * acc_scr[rsl] + _dot(p.astype(BF16), vals)
            m_scr[rsl] = m_new

    fully_visible = (ikv + 1) * tk <= first_limit
    pl.when(fully_visible)(lambda: update(False))
    pl.when(jnp.logical_not(fully_visible))(lambda: update(True))

    @pl.when(last_ref[step] == 1)
    def _():
        o = acc_scr[...] / l_scr[...]
        o_ref[...] = o.reshape(nh, tq, r).astype(o_ref.dtype)


def _mla_attention(q, k, r, bsz, l, t_past, tq, tk):
    nh, t, dq = q.shape
    tkeys = k.shape[1]
    assert l % tq == 0 and tq % CHUNK == 0 and t_past % CHUNK == 0 and tkeys % tk == 0 and tkeys >= t_past + l
    assert tq & (tq - 1) == 0 and k.shape[2] == dq and r % LANES == 0
    nq = l // tq
    hs = min(nh, max(1, MLA_SUB_ROWS // tq))
    assert nh % hs == 0
    qi, ki, last = [], [], []
    for qb in range(nq):
        need = -(-(t_past + (qb + 1) * tq) // tk)
        qi += [qb] * need
        ki += list(range(need))
        last += [0] * (need - 1) + [1]
    tables = [jnp.asarray(np.asarray(a, np.int32)) for a in (qi, ki, last)]
    qmap = lambda bi, si, qi_ref, ki_ref, last_ref: (0, bi * nq + qi_ref[si], 0)
    kmap = lambda bi, si, qi_ref, ki_ref, last_ref: (bi, ki_ref[si], 0)
    return pl.pallas_call(
        functools.partial(_mla_body, tq=tq, tk=tk, t_past=t_past, hs=hs),
        grid_spec=pltpu.PrefetchScalarGridSpec(
            num_scalar_prefetch=3,
            grid=(bsz, len(qi)),
            in_specs=[pl.BlockSpec((nh, tq, dq), qmap),
                      pl.BlockSpec((1, tk, dq), kmap)],
            out_specs=pl.BlockSpec((nh, tq, r), qmap),
            scratch_shapes=[pltpu.VMEM((nh * tq, 1), F32), pltpu.VMEM((nh * tq, 1), F32),
                            pltpu.VMEM((nh * tq, r), F32)]),
        out_shape=jax.ShapeDtypeStruct((nh, t, r), BF16),
        compiler_params=_params("parallel", "arbitrary"),
        name="mla_attention",
    )(*tables, q, k)


def _xattn_body(q_ref, k_ref, v_ref, o_ref, *, nheads, hd):
    q = q_ref[...]
    k = k_ref[0]
    v = v_ref[0]
    outs = []
    for h in range(nheads):
        sl = slice(h * hd, (h + 1) * hd)
        s = _dot_nt(q[:, sl], k[:, sl]) * hd ** -0.5
        s = s - jnp.max(s, axis=1, keepdims=True)
        p = jnp.exp(s)
        p = p / jnp.sum(p, axis=1, keepdims=True)
        outs.append(_dot(p.astype(BF16), v[:, sl]))
    o_ref[...] = jnp.concatenate(outs, axis=1).astype(o_ref.dtype)


def _cross_attention(q, mem_k, mem_v, bsz, l, nheads, hd, tl_cap=512):
    t, d = q.shape
    m = mem_k.shape[1]
    tl = _tile(l, tl_cap, 8)
    nl = l // tl
    return pl.pallas_call(
        functools.partial(_xattn_body, nheads=nheads, hd=hd),
        grid=(bsz, nl),
        in_specs=[pl.BlockSpec((tl, d), lambda bi, li: (bi * nl + li, 0)),
                  pl.BlockSpec((1, m, d), lambda bi, li: (bi, 0, 0)),
                  pl.BlockSpec((1, m, d), lambda bi, li: (bi, 0, 0))],
        out_specs=pl.BlockSpec((tl, d), lambda bi, li: (bi * nl + li, 0)),
        out_shape=jax.ShapeDtypeStruct((t, d), BF16),
        compiler_params=_params("parallel", "parallel"),
        name="cross_attention",
    )(q, mem_k, mem_v)


SSD_HEADS, SSD_HEAD_DIM, SSD_GROUPS, SSD_STATE = 64, 64, 8, 128
SSD_INNER = SSD_HEADS * SSD_HEAD_DIM
SSD_CONV_DIM = SSD_INNER + 2 * SSD_GROUPS * SSD_STATE
GDN_HEADS, GDN_DK, GDN_DV = 32, 128, 128
GDN_CONV_DIM = 2 * GDN_HEADS * GDN_DK + GDN_HEADS * GDN_DV
MLA_HEADS, MLA_Q_RANK, MLA_KV_RANK, MLA_NOPE, MLA_ROPE, MLA_V = 32, 1024, 512, 128, 64, 128
HG_HEADS, HG_DK = 32, 128
HG_W = HG_HEADS * HG_DK
XA_HEADS, XA_HEAD_DIM = 4, 128
MLA_TK = 1024
MLA_TQ = 128


def _rope_tables(pos, p):
    half = p // 2
    inv = ROPE_THETA ** (-jnp.arange(half, dtype=F32) / half)
    ang = pos[:, None] * inv[None, :]
    return jnp.cos(ang), jnp.sin(ang)


def _prep_weights(w):
    out = {}
    e_sizes = [SSD_INNER, SSD_CONV_DIM, SSD_HEADS, GDN_CONV_DIM, GDN_HEADS, GDN_HEADS, GDN_HEADS * GDN_DV]
    e_off = np.concatenate([[0], np.cumsum(e_sizes)])
    wi = w["w_in_even"]
    sec = lambda a, offs, i: a[:, :, offs[i]:offs[i + 1]]
    out["w_in_even_main"] = jnp.concatenate(
        [sec(wi, e_off, 0), sec(wi, e_off, 1), sec(wi, e_off, 3), sec(wi, e_off, 6)], axis=-1).astype(BF16)
    out["w_in_even_small"] = jnp.concatenate(
        [sec(wi, e_off, 2), sec(wi, e_off, 4), sec(wi, e_off, 5)], axis=-1).astype(BF16)
    o_sizes = [MLA_Q_RANK, MLA_KV_RANK, MLA_ROPE, HG_W, HG_W, HG_W, HG_W]
    o_off = np.concatenate([[0], np.cumsum(o_sizes)])
    wo = w["w_in_odd"]
    out["w_in_odd_main"] = jnp.concatenate(
        [sec(wo, o_off, 3), sec(wo, o_off, 4), sec(wo, o_off, 5), sec(wo, o_off, 6), sec(wo, o_off, 0)],
        axis=-1).astype(BF16)
    out["w_in_odd_small"] = jnp.concatenate([sec(wo, o_off, 1), sec(wo, o_off, 2)], axis=-1).astype(BF16)
    n_odd = wo.shape[0]
    uq = w["mla_w_uq"].reshape(n_odd, MLA_Q_RANK, MLA_HEADS, MLA_NOPE + MLA_ROPE)
    out["w_uq"] = jnp.concatenate(
        [uq[..., :MLA_NOPE].reshape(n_odd, MLA_Q_RANK, MLA_HEADS * MLA_NOPE),
         uq[..., MLA_NOPE:].reshape(n_odd, MLA_Q_RANK, MLA_HEADS * MLA_ROPE)], axis=-1).astype(BF16)
    out["w_uk"] = jnp.transpose(w["mla_w_uk"], (0, 2, 3, 1)).astype(BF16)
    out["w_uv"] = jnp.transpose(w["mla_w_uv"], (0, 2, 1, 3)).astype(BF16)
    for name in ("w_out_even", "w_out_odd", "xa_w_q", "xa_w_k", "xa_w_v", "xa_w_o", "ffn_w1", "ffn_w2"):
        out[name] = w[name].astype(BF16)
    return out


def _even_mixer(hn, w, wb, e, bsz, l, ssd_s, ssd_buf, gdn_s, gdn_buf):
    pm = _matmul([hn], [wb["w_in_even_main"][e]], F32)
    sm = _matmul([hn], [wb["w_in_even_small"][e]], F32)
    xbc_col0, qkv_col0 = SSD_INNER, SSD_INNER + SSD_CONV_DIM
    gate_col0 = qkv_col0 + GDN_CONV_DIM
    xbc = _conv_silu(pm, xbc_col0, SSD_CONV_DIM, ssd_buf, w["ssd_conv_w"][e], w["ssd_conv_b"][e], bsz, l)
    qkv = _conv_silu(pm, qkv_col0, GDN_CONV_DIM, gdn_buf, w["gdn_conv_w"][e],
                     jnp.zeros((GDN_CONV_DIM,), F32), bsz, l)
    y, ssd_new = _ssd(xbc, pm, 0, sm, w["ssd_dt_bias"][e], w["ssd_a_log"][e], w["ssd_d"][e], w["ssd_norm"][e],
                      ssd_s.reshape(bsz, SSD_INNER, SSD_STATE), bsz, l, SSD_HEADS, SSD_HEAD_DIM, SSD_GROUPS,
                      SSD_STATE)
    o, gdn_new = _gdn(qkv, pm, gate_col0, sm, SSD_HEADS, w["gdn_a_log"][e], w["gdn_dt_bias"][e], w["gdn_norm"][e],
                      gdn_s, bsz, l, GDN_HEADS, GDN_DK, GDN_DV)
    w_out = wb["w_out_even"][e]
    mix = _matmul([y, o], [w_out, w_out], BF16, b_rows=[0, SSD_INNER])
    pm3 = pm.reshape(bsz, l, pm.shape[1])
    tail = pm3[:, l - (CONV_W - 1):]
    states = (ssd_new.reshape(bsz, SSD_HEADS, SSD_HEAD_DIM, SSD_STATE), tail[:, :, xbc_col0:qkv_col0],
              gdn_new, tail[:, :, qkv_col0:gate_col0])
    return mix, states


def _odd_mixer(hn, w, wb, j, lb, bsz, l, ckv_past, kr_past, hg_s):
    t_past = ckv_past.shape[1]
    pm = _matmul([hn], [wb["w_in_odd_main"][j]], F32)
    sm = _matmul([hn], [wb["w_in_odd_small"][j]], F32)
    posf = (t_past + jnp.arange(l, dtype=jnp.int32)).astype(F32)
    scale = (MLA_NOPE + MLA_ROPE) ** -0.5
    cqn = _rms(pm, w["mla_q_norm"][j], BF16, col_block=4 * HG_W // MLA_Q_RANK, width=MLA_Q_RANK)
    q = _matmul([cqn], [wb["w_uq"][j]], F32)
    lane_pad = (-MLA_ROPE) % LANES
    cos, sin = _rope_tables(posf, MLA_ROPE)
    cos2 = jnp.tile(jnp.concatenate([cos, cos], axis=1), (bsz, 1))
    sin2 = jnp.tile(jnp.concatenate([-sin, sin], axis=1), (bsz, 1))
    q_att = _head_split_matmul(q, MLA_HEADS * MLA_NOPE, wb["w_uk"][j], scale, cos2, sin2)
    c_kv = _rms(sm, w["mla_kv_norm"][j], F32, col_block=0, width=MLA_KV_RANK).reshape(bsz, l, MLA_KV_RANK)
    kr = sm[:, MLA_KV_RANK:].reshape(bsz, l, MLA_ROPE)
    kr1, kr2 = kr[..., :MLA_ROPE // 2], kr[..., MLA_ROPE // 2:]
    k_rope = jnp.concatenate([kr1 * cos - kr2 * sin, kr2 * cos + kr1 * sin], axis=-1)
    tkeys = t_past + l
    tk = _tile(tkeys, MLA_TK, CHUNK)
    k_att = jnp.concatenate([jnp.concatenate([ckv_past, c_kv], axis=1).astype(BF16),
                             jnp.concatenate([kr_past, k_rope], axis=1).astype(BF16)], axis=2)
    k_att = jnp.pad(k_att, ((0, 0), (0, 0), (0, lane_pad)))
    tq = MLA_TQ if l % MLA_TQ == 0 else CHUNK
    o_lat = _mla_attention(q_att, k_att, MLA_KV_RANK, bsz, l, t_past, tq, tk)
    o_mla = _head_merge_matmul(o_lat, wb["w_uv"][j])
    o_hg, hg_new = _hgrn(pm, lb, w["hg_norm"][j], hg_s, bsz, l, HG_HEADS, HG_DK)
    w_out = wb["w_out_odd"][j]
    mix = _matmul([o_mla, o_hg], [w_out, w_out], BF16, b_rows=[0, MLA_HEADS * MLA_V])
    return mix, (c_kv, k_rope, hg_new)


def _forward(x, mem_k, mem_v, ssd_s, ssd_buf, gdn_s, gdn_buf, ckv_past, kr_past, hg_s, w, wb, lbs):
    bsz, l, d = x.shape
    depth = w["norm_g"].shape[0]
    x = x.reshape(bsz * l, d)
    hn = _rms(x, w["norm_g"][0, 0], BF16)
    ev, od = [], []
    for layer in range(depth):
        ng = w["norm_g"][layer]
        if layer % 2 == 0:
            e = layer // 2
            mix, st = _even_mixer(hn, w, wb, e, bsz, l, ssd_s[e], ssd_buf[e], gdn_s[e], gdn_buf[e])
            ev.append(st)
        else:
            j = layer // 2
            mix, st = _odd_mixer(hn, w, wb, j, lbs[layer], bsz, l, ckv_past[j], kr_past[j], hg_s[j])
            od.append(st)
        x, hn = _add_rms(x, mix, ng[1], ng[2])
        q = _matmul([hn], [wb["xa_w_q"][layer]], BF16)
        xo = _cross_attention(q, mem_k[layer], mem_v[layer], bsz, l, XA_HEADS, XA_HEAD_DIM)
        xa = _matmul([xo], [wb["xa_w_o"][layer]], BF16)
        x, hn = _add_rms(x, xa, ng[3], ng[4])
        h1 = _matmul([hn], [wb["ffn_w1"][layer]], BF16, act="relu2")
        f = _matmul([h1], [wb["ffn_w2"][layer]], BF16)
        x, hn = _add_rms(x, f, ng[5], w["norm_g"][layer + 1, 0] if layer + 1 < depth else None)
    stack = lambda items, i: jnp.stack([it[i] for it in items])
    return (x.reshape(bsz, l, d), stack(ev, 0), stack(ev, 1), stack(ev, 2), stack(ev, 3),
            stack(od, 0), stack(od, 1), stack(od, 2))


def kernel(x_prompt, x_sample, mem_prompt, state_ssd, state_ssd_conv, state_gdn, state_gdn_conv, cache_mla_ckv, cache_mla_krope, state_hgrn, cache_mem_k, cache_mem_v, norm_g, w_in_even, w_out_even, ssd_conv_w, ssd_conv_b, ssd_dt_bias, ssd_a_log, ssd_d, ssd_norm, gdn_conv_w, gdn_a_log, gdn_dt_bias, gdn_norm, w_in_odd, w_out_odd, mla_q_norm, mla_w_uq, mla_kv_norm, mla_w_uk, mla_w_uv, hg_lower_bound, hg_norm, xa_mem_norm, xa_w_q, xa_w_k, xa_w_v, xa_w_o, ffn_w1, ffn_w2):
    w = dict(norm_g=norm_g, w_in_even=w_in_even, w_out_even=w_out_even, ssd_conv_w=ssd_conv_w, ssd_conv_b=ssd_conv_b,
             ssd_dt_bias=ssd_dt_bias, ssd_a_log=ssd_a_log, ssd_d=ssd_d, ssd_norm=ssd_norm, gdn_conv_w=gdn_conv_w,
             gdn_a_log=gdn_a_log, gdn_dt_bias=gdn_dt_bias, gdn_norm=gdn_norm, w_in_odd=w_in_odd, w_out_odd=w_out_odd,
             mla_q_norm=mla_q_norm, mla_w_uq=mla_w_uq, mla_kv_norm=mla_kv_norm, mla_w_uk=mla_w_uk, mla_w_uv=mla_w_uv,
             hg_norm=hg_norm, xa_w_q=xa_w_q, xa_w_k=xa_w_k, xa_w_v=xa_w_v, xa_w_o=xa_w_o, ffn_w1=ffn_w1,
             ffn_w2=ffn_w2)
    wb = _prep_weights(w)
    depth = norm_g.shape[0]
    n_even, n_odd = (depth + 1) // 2, depth // 2
    lbs = jnp.cumsum(jax.nn.softmax(hg_lower_bound.astype(F32), axis=0), axis=0)
    lbs = lbs - lbs[0]

    b, m, d = mem_prompt.shape
    xa_dim = XA_HEADS * XA_HEAD_DIM
    mk, mv = [], []
    for layer in range(depth):
        mn = _rms(mem_prompt.reshape(b * m, d), xa_mem_norm[layer], BF16)
        mk.append(_matmul([mn], [wb["xa_w_k"][layer]], F32).reshape(b, m, xa_dim))
        mv.append(_matmul([mn], [wb["xa_w_v"][layer]], F32).reshape(b, m, xa_dim))
    p_mem_k = jnp.stack(mk)
    p_mem_v = jnp.stack(mv)
    dt = x_prompt.dtype
    prompt = _forward(
        x_prompt, p_mem_k.astype(BF16), p_mem_v.astype(BF16),
        jnp.zeros((n_even, b, SSD_HEADS, SSD_HEAD_DIM, SSD_STATE), dt),
        jnp.zeros((n_even, b, CONV_W - 1, SSD_CONV_DIM), dt),
        jnp.zeros((n_even, b, GDN_HEADS, GDN_DK, GDN_DV), dt),
        jnp.zeros((n_even, b, CONV_W - 1, GDN_CONV_DIM), dt),
        jnp.zeros((n_odd, b, 0, MLA_KV_RANK), dt),
        jnp.zeros((n_odd, b, 0, MLA_ROPE), dt),
        jnp.zeros((n_odd, b, HG_HEADS, HG_DK, HG_DK), dt),
        w, wb, lbs)
    db = x_sample.shape[0]
    sample = _forward(
        x_sample, cache_mem_k.reshape(depth, db, m, xa_dim).astype(BF16),
        cache_mem_v.reshape(depth, db, m, xa_dim).astype(BF16),
        state_ssd, state_ssd_conv, state_gdn, state_gdn_conv, cache_mla_ckv, cache_mla_krope, state_hgrn,
        w, wb, lbs)
    y_prompt, p_rest = prompt[0], prompt[1:]
    y_sample, s_rest = sample[0], sample[1:]
    return (y_prompt, y_sample, *p_rest,
            p_mem_k.reshape(depth, b, m, XA_HEADS, XA_HEAD_DIM), p_mem_v.reshape(depth, b, m, XA_HEADS, XA_HEAD_DIM),
            *s_rest)
```

```python
import functools

import numpy as np
import jax
import jax.numpy as jnp
from jax import lax
from jax.experimental import pallas as pl
from jax.experimental.pallas import tpu as pltpu

F32 = jnp.float32
BF16 = jnp.bfloat16

VMEM_LIMIT_BYTES = 56 * 1024 * 1024
LANES = 128
NORM_EPS = 1e-6
CHUNK = 64
CHUNK_SHIFT = 6
assert 1 << CHUNK_SHIFT == CHUNK
CONV_W = 4
ROPE_THETA = 10000.0
MASKED = -1e30


def _params(*sem):
    return pltpu.CompilerParams(dimension_semantics=sem, vmem_limit_bytes=VMEM_LIMIT_BYTES)


def _tile(n, cap, mult):
    if n <= cap:
        return n
    for d in range(cap - cap % mult, 0, -mult):
        if n % d == 0:
            return d
    return n


def _dot(a, b):
    return jnp.dot(a, b, preferred_element_type=F32)


def _dot_nt(a, b):
    return lax.dot_general(a, b, (((1,), (1,)), ((), ())), preferred_element_type=F32)


def _dot_tn(a, b):
    return lax.dot_general(a, b, (((0,), (0,)), ((), ())), preferred_element_type=F32)


def _split3(x):
    hi = x.astype(BF16)
    r = x - hi.astype(F32)
    mid = r.astype(BF16)
    lo = (r - mid.astype(F32)).astype(BF16)
    return hi, mid, lo


def _mask_dot(mask3, x):
    return _dot(mask3, jnp.concatenate(_split3(x), axis=0))


def _mask_dot_nt(mask16, x):
    hi, mid, lo = _split3(x)
    return _dot_nt(mask16, hi) + (_dot_nt(mask16, mid) + _dot_nt(mask16, lo))


def _dot_x3(a, b):
    a_hi = a.astype(BF16)
    a_lo = (a - a_hi.astype(F32)).astype(BF16)
    b_hi = b.astype(BF16)
    b_lo = (b - b_hi.astype(F32)).astype(BF16)
    return _dot(a_hi, b_hi) + (_dot(a_hi, b_lo) + _dot(a_lo, b_hi))


def _softplus(x):
    return jnp.maximum(x, 0.0) + jnp.log1p(jnp.exp(-jnp.abs(x)))


def _sigmoid(x):
    return 0.5 * jnp.tanh(0.5 * x) + 0.5


def _silu(x):
    return x * _sigmoid(x)


def _pick_lane(a, idx):
    lane = lax.broadcasted_iota(jnp.int32, a.shape, 1)
    return jnp.sum(jnp.where(lane == idx, a, 0.0), axis=1, keepdims=True)


def _mm_body(*refs, n_ops, nks, act):
    a_refs, b_refs = refs[:n_ops], refs[n_ops:2 * n_ops]
    o_ref, acc_ref = refs[2 * n_ops], refs[2 * n_ops + 1]
    k = pl.program_id(2)
    nk = sum(nks)

    @pl.when(k == 0)
    def _():
        acc_ref[...] = jnp.zeros_like(acc_ref)

    off = 0
    for a_ref, b_ref, n in zip(a_refs, b_refs, nks):
        def step(a_ref=a_ref, b_ref=b_ref):
            acc_ref[...] += _dot(a_ref[...], b_ref[...])
        if n_ops == 1:
            step()
        else:
            pl.when((k >= off) & (k < off + n))(step)
        off += n

    @pl.when(k == nk - 1)
    def _():
        r = acc_ref[...]
        if act == "relu2":
            r = jnp.square(jnp.maximum(r, 0.0))
        o_ref[...] = r.astype(o_ref.dtype)


def _mm_full_k_body(a_ref, b_ref, o_ref, *, act):
    r = _dot(a_ref[...], b_ref[...])
    if act == "relu2":
        r = jnp.square(jnp.maximum(r, 0.0))
    o_ref[...] = r.astype(o_ref.dtype)


FULL_K_MAX = 4096
FULL_K_B_BLOCK_BYTES = 4 * 1024 * 1024


def _matmul(a_list, b_list, out_dtype, act=None, b_rows=None, tm_cap=1024, tn_cap=None, tk_cap=1024):
    m = a_list[0].shape[0]
    n = b_list[0].shape[1]
    b_rows = [0] * len(a_list) if b_rows is None else b_rows
    tm = _tile(m, tm_cap, 8)
    k0 = a_list[0].shape[1]
    if len(a_list) == 1 and k0 <= FULL_K_MAX and b_rows[0] == 0 and b_list[0].shape[0] == k0:
        cap = tn_cap or min(2048, max(512, FULL_K_B_BLOCK_BYTES // (2 * k0)))
        tn = _tile(n, cap, LANES)
        return pl.pallas_call(
            functools.partial(_mm_full_k_body, act=act),
            grid=(m // tm, n // tn),
            in_specs=[pl.BlockSpec((tm, k0), lambda i, j: (i, 0)),
                      pl.BlockSpec((k0, tn), lambda i, j: (0, j))],
            out_specs=pl.BlockSpec((tm, tn), lambda i, j: (i, j)),
            out_shape=jax.ShapeDtypeStruct((m, n), out_dtype),
            compiler_params=_params("parallel", "arbitrary"),
            name="matmul_full_k",
        )(a_list[0], b_list[0])
    tn = _tile(n, tn_cap or 2048, LANES)
    tk = min(_tile(a.shape[1], tk_cap * (2 if len(a_list) == 1 else 1), LANES) for a in a_list)
    nks, offs = [], []
    for a, b, r0 in zip(a_list, b_list, b_rows):
        assert a.shape[0] == m and b.shape[1] == n and a.shape[1] % tk == 0 and r0 % tk == 0
        assert r0 + a.shape[1] <= b.shape[0]
        offs.append(sum(nks))
        nks.append(a.shape[1] // tk)
    in_specs = []
    for o, nki in zip(offs, nks):
        in_specs.append(pl.BlockSpec((tm, tk), lambda i, j, k, o=o, nki=nki: (i, jnp.clip(k - o, 0, nki - 1))))
    for o, nki, r0 in zip(offs, nks, b_rows):
        in_specs.append(pl.BlockSpec(
            (tk, tn), lambda i, j, k, o=o, nki=nki, rb=r0 // tk: (rb + jnp.clip(k - o, 0, nki - 1), j)))
    return pl.pallas_call(
        functools.partial(_mm_body, n_ops=len(a_list), nks=tuple(nks), act=act),
        grid=(m // tm, n // tn, sum(nks)),
        in_specs=in_specs,
        out_specs=pl.BlockSpec((tm, tn), lambda i, j, k: (i, j)),
        out_shape=jax.ShapeDtypeStruct((m, n), out_dtype),
        scratch_shapes=[pltpu.VMEM((tm, tn), F32)],
        compiler_params=_params("parallel", "parallel", "arbitrary"),
        name="matmul",
    )(*a_list, *b_list)


def _head_mm_body(a_ref, b_ref, o_ref):
    o_ref[...] = _dot(a_ref[...], b_ref[...]).astype(o_ref.dtype)


def _head_split_body(a_ref, b_ref, rot_ref, cos_ref, sin_ref, o_ref, *, scale):
    _, dk, n = b_ref.shape
    p = cos_ref.shape[1]
    tm = o_ref.shape[1]
    pair = rot_ref[...]
    for j in range(2):
        a = a_ref[:, j * dk:(j + 1) * dk].astype(BF16)
        o_ref[j, :, 0:n] = (_dot(a, b_ref[j]) * scale).astype(o_ref.dtype)
        x = pair[:, j * p:(j + 1) * p]
        swapped = jnp.concatenate([x[:, p // 2:], x[:, :p // 2]], axis=1)
        o_ref[j, :, n:n + p] = ((x * cos_ref[...] + swapped * sin_ref[...]) * scale).astype(o_ref.dtype)
        o_ref[j, :, n + p:] = jnp.zeros((tm, o_ref.shape[2] - n - p), o_ref.dtype)


def _head_split_matmul(a, rot_col0, w, scale, cos2, sin2, tm_cap=1024):
    t = a.shape[0]
    nh, dk, n = w.shape
    p = cos2.shape[1]
    assert n % LANES == 0 and 2 * p == LANES and nh % 2 == 0 and rot_col0 % LANES == 0
    tm = _tile(t, tm_cap, 8)
    return pl.pallas_call(
        functools.partial(_head_split_body, scale=scale),
        grid=(t // tm, nh // 2),
        in_specs=[pl.BlockSpec((tm, 2 * dk), lambda i, h: (i, h)),
                  pl.BlockSpec((2, dk, n), lambda i, h: (h, 0, 0)),
                  pl.BlockSpec((tm, LANES), lambda i, h: (i, rot_col0 // LANES + h)),
                  pl.BlockSpec((tm, p), lambda i, h: (i, 0)),
                  pl.BlockSpec((tm, p), lambda i, h: (i, 0))],
        out_specs=pl.BlockSpec((2, tm, n + LANES), lambda i, h: (h, i, 0)),
        out_shape=jax.ShapeDtypeStruct((nh, t, n + LANES), BF16),
        compiler_params=_params("parallel", "arbitrary"),
        name="head_split_matmul",
    )(a, w, a, cos2, sin2)


def _head_merge_matmul(a, w, tm_cap=1024):
    nh, t, k = a.shape
    dv = w.shape[2]
    tm = _tile(t, tm_cap, 8)
    return pl.pallas_call(
        _head_mm_body,
        grid=(t // tm, nh),
        in_specs=[pl.BlockSpec((None, tm, k), lambda i, h: (h, i, 0)),
                  pl.BlockSpec((None, k, dv), lambda i, h: (h, 0, 0))],
        out_specs=pl.BlockSpec((tm, dv), lambda i, h: (i, h)),
        out_shape=jax.ShapeDtypeStruct((t, nh * dv), BF16),
        compiler_params=_params("parallel", "arbitrary"),
        name="head_merge_matmul",
    )(a, w)


def _rms_rows(x, g):
    return x * lax.rsqrt(jnp.mean(x * x, axis=-1, keepdims=True) + NORM_EPS) * g


def _rms_body(x_ref, g_ref, o_ref):
    o_ref[...] = _rms_rows(x_ref[...], g_ref[...]).astype(o_ref.dtype)


def _rms(x, g, out_dtype, col_block=0, width=None, tr_cap=256):
    t = x.shape[0]
    width = x.shape[1] if width is None else width
    tr = _tile(t, tr_cap, 8)
    return pl.pallas_call(
        _rms_body,
        grid=(t // tr,),
        in_specs=[pl.BlockSpec((tr, width), lambda i: (i, col_block)),
                  pl.BlockSpec((1, width), lambda i: (0, 0))],
        out_specs=pl.BlockSpec((tr, width), lambda i: (i, 0)),
        out_shape=jax.ShapeDtypeStruct((t, width), out_dtype),
        compiler_params=_params("parallel"),
        name="rms",
    )(x, g.reshape(1, width))


def _add_rms_body(x_ref, y_ref, g1_ref, g2_ref, xo_ref, ho_ref):
    xn = x_ref[...] + _rms_rows(y_ref[...].astype(F32), g1_ref[...])
    xo_ref[...] = xn
    ho_ref[...] = _rms_rows(xn, g2_ref[...]).astype(ho_ref.dtype)


def _add_rms_last_body(x_ref, y_ref, g1_ref, xo_ref):
    xo_ref[...] = x_ref[...] + _rms_rows(y_ref[...].astype(F32), g1_ref[...])


def _add_rms(x, y, g1, g2, tr_cap=256):
    t, d = x.shape
    tr = _tile(t, tr_cap, 8)
    row = pl.BlockSpec((tr, d), lambda i: (i, 0))
    gain = pl.BlockSpec((1, d), lambda i: (0, 0))
    if g2 is None:
        out = pl.pallas_call(
            _add_rms_last_body, grid=(t // tr,), in_specs=[row, row, gain], out_specs=row,
            out_shape=jax.ShapeDtypeStruct((t, d), F32), compiler_params=_params("parallel"),
            name="add_rms_last",
        )(x, y, g1.reshape(1, d))
        return out, None
    return pl.pallas_call(
        _add_rms_body, grid=(t // tr,), in_specs=[row, row, gain, gain], out_specs=[row, row],
        out_shape=[jax.ShapeDtypeStruct((t, d), F32), jax.ShapeDtypeStruct((t, d), BF16)],
        compiler_params=_params("parallel"), name="add_rms",
    )(x, y, g1.reshape(1, d), g2.reshape(1, d))


CONV_HALO = 8


def _conv_taps(x, w, b):
    acc = b + x * w[CONV_W - 1:CONV_W, :]
    for j in range(1, CONV_W):
        acc = acc + pltpu.roll(x, j, 0) * w[CONV_W - 1 - j:CONV_W - j, :]
    return acc


def _conv_body(u_ref, buf_ref, w_ref, b_ref, o_ref, hist_ref, *, tl):
    lt = pl.program_id(2)

    @pl.when(lt == 0)
    def _():
        hist_ref[...] = jnp.zeros_like(hist_ref)
        hist_ref[CONV_HALO - (CONV_W - 1):CONV_HALO, :] = buf_ref[0]

    u = u_ref[...]
    w = w_ref[...]
    b = b_ref[...]
    o_ref[...] = _silu(_conv_taps(u, w, b))
    head = jnp.concatenate([hist_ref[...], u[0:CONV_HALO, :]], axis=0)
    o_ref[0:CONV_HALO, :] = _silu(_conv_taps(head, w, b)[CONV_HALO:, :])
    hist_ref[...] = u[tl - CONV_HALO:tl, :]


def _conv_silu(u, col0, c, buf, w, b, bsz, l, tl_cap=512, tc=1024):
    assert l >= CONV_W - 1 and col0 % tc == 0 and c % tc == 0
    tl = _tile(l, tl_cap, 8)
    assert tl >= CONV_HALO
    nl = l // tl
    cb0 = col0 // tc
    return pl.pallas_call(
        functools.partial(_conv_body, tl=tl),
        grid=(bsz, c // tc, nl),
        in_specs=[pl.BlockSpec((tl, tc), lambda bi, ci, li: (bi * nl + li, cb0 + ci)),
                  pl.BlockSpec((1, CONV_W - 1, tc), lambda bi, ci, li: (bi, 0, ci)),
                  pl.BlockSpec((CONV_W, tc), lambda bi, ci, li: (0, ci)),
                  pl.BlockSpec((1, tc), lambda bi, ci, li: (0, ci))],
        out_specs=pl.BlockSpec((tl, tc), lambda bi, ci, li: (bi * nl + li, ci)),
        out_shape=jax.ShapeDtypeStruct((bsz * l, c), F32),
        scratch_shapes=[pltpu.VMEM((CONV_HALO, tc), F32)],
        compiler_params=_params("parallel", "parallel", "arbitrary"),
        name="conv_silu",
    )(u, buf, w, b.reshape(1, c))


def _tri_masks(n):
    r = np.arange(n)
    tril = (r[:, None] >= r[None, :]).astype(np.float32)
    return (jnp.asarray(tril, dtype=BF16), jnp.asarray(np.tile(tril, (1, 3)), dtype=BF16),
            jnp.asarray(np.eye(n, dtype=np.float32), dtype=BF16))


def _ssd_body(x_ref, bm_ref, cm_ref, z_ref, sm_ref, dtb_ref, alog_ref, d_ref, gn_ref, s0_ref, tril_ref, tril3_ref,
              eye_ref, y_ref, so_ref, s_scr, cst_scr, *, nc, hpg, hd, nheads, gps):
    g0 = pl.program_id(1) * gps
    c = pl.program_id(2)
    lc = x_ref.shape[0]
    gw = hpg * hd
    nstate = bm_ref.shape[1] // gps

    @pl.when(c == 0)
    def _():
        s_scr[...] = s0_ref[0]

    dt = _softplus(sm_ref[:, 0:nheads] + dtb_ref[...])
    da = dt * (-jnp.exp(alog_ref[...]))
    cs = _mask_dot(tril3_ref[...], da)
    cst_scr[...] = _mask_dot_nt(eye_ref[...], cs)
    tri = tril_ref[...].astype(F32) > 0.0

    x = x_ref[...]
    gs = range(gps)
    bm = [bm_ref[:, gi * nstate:(gi + 1) * nstate].astype(BF16) for gi in gs]
    cm = [cm_ref[:, gi * nstate:(gi + 1) * nstate].astype(BF16) for gi in gs]
    s = [s_scr[gi * gw:(gi + 1) * gw, :] for gi in gs]
    cb = [_dot_nt(cm[gi], bm[gi]) for gi in gs]
    y_state = [_dot_nt(cm[gi], s[gi].astype(BF16)) for gi in gs]
    js = range(gps * hpg)
    hd_ids = [g0 * hpg + j for j in js]
    cs_col = [_pick_lane(cs, h) for h in hd_ids]
    dt_col = [_pick_lane(dt, h) for h in hd_ids]
    d_h = [_pick_lane(d_ref[...], h) for h in hd_ids]
    cs_row = [cst_scr[pl.ds(h, 1), :] for h in hd_ids]
    cs_last = [cs_row[j][:, lc - 1:lc] for j in js]
    mix = [(cb[j // hpg] * jnp.exp(jnp.where(tri, cs_col[j] - cs_row[j], MASKED))).astype(BF16) for j in js]
    xh = [x[:, j * hd:(j + 1) * hd] for j in js]
    xdt = [xh[j] * dt_col[j] for j in js]
    y_in = [_dot(mix[j], xdt[j].astype(BF16)) for j in js]
    ys = [y_in[j] + y_state[j // hpg][:, (j % hpg) * hd:(j % hpg + 1) * hd] * jnp.exp(cs_col[j]) + d_h[j] * xh[j]
          for j in js]
    xws = [xdt[j] * jnp.exp(cs_last[j] - cs_col[j]) for j in js]
    decs = [jnp.broadcast_to(jnp.exp(cs_last[j]), (hd, nstate)) for j in js]
    for gi in gs:
        hsl = slice(gi * hpg, (gi + 1) * hpg)
        xw = jnp.concatenate(xws[hsl], axis=1)
        s_new = s[gi] * jnp.concatenate(decs[hsl], axis=0) + _dot_tn(xw.astype(BF16), bm[gi])
        s_scr[gi * gw:(gi + 1) * gw, :] = s_new
        y = jnp.concatenate(ys[hsl], axis=1) * _silu(z_ref[:, gi * gw:(gi + 1) * gw])
        y_ref[:, gi * gw:(gi + 1) * gw] = _rms_rows(y, gn_ref[:, gi * gw:(gi + 1) * gw]).astype(y_ref.dtype)

    @pl.when(c == nc - 1)
    def _():
        so_ref[0] = s_scr[...]


SSD_GROUPS_PER_STEP = 4


def _ssd(xbc, pm, z_col0, sm, dt_bias, a_log, d, gn, s0, bsz, l, nheads, hd, ngroups, nstate):
    t = bsz * l
    lc = CHUNK
    nc = l // lc
    hpg = nheads // ngroups
    gps = SSD_GROUPS_PER_STEP
    gw = gps * hpg * hd
    sw = gps * nstate
    inner = nheads * hd
    assert l % lc == 0 and nheads == lc and z_col0 % gw == 0 and ngroups % gps == 0 and inner % sw == 0
    tril, tril3, eye = _tri_masks(lc)
    rows = lambda bi, gi, ci: bi * nc + ci
    const2 = lambda bi, gi, ci: (0, 0)
    y, s_new = pl.pallas_call(
        functools.partial(_ssd_body, nc=nc, hpg=hpg, hd=hd, nheads=nheads, gps=gps),
        grid=(bsz, ngroups // gps, nc),
        in_specs=[pl.BlockSpec((lc, gw), lambda bi, gi, ci: (rows(bi, gi, ci), gi)),
                  pl.BlockSpec((lc, sw), lambda bi, gi, ci: (rows(bi, gi, ci), inner // sw + gi)),
                  pl.BlockSpec((lc, sw), lambda bi, gi, ci: (rows(bi, gi, ci), (inner + ngroups * nstate) // sw + gi)),
                  pl.BlockSpec((lc, gw), lambda bi, gi, ci: (rows(bi, gi, ci), z_col0 // gw + gi)),
                  pl.BlockSpec((lc, sm.shape[1]), lambda bi, gi, ci: (rows(bi, gi, ci), 0)),
                  pl.BlockSpec((1, nheads), const2),
                  pl.BlockSpec((1, nheads), const2),
                  pl.BlockSpec((1, nheads), const2),
                  pl.BlockSpec((1, gw), lambda bi, gi, ci: (0, gi)),
                  pl.BlockSpec((1, gw, nstate), lambda bi, gi, ci: (bi, gi, 0)),
                  pl.BlockSpec((lc, lc), const2),
                  pl.BlockSpec((lc, 3 * lc), const2),
                  pl.BlockSpec((lc, lc), const2)],
        out_specs=[pl.BlockSpec((lc, gw), lambda bi, gi, ci: (rows(bi, gi, ci), gi)),
                   pl.BlockSpec((1, gw, nstate), lambda bi, gi, ci: (bi, gi, 0))],
        out_shape=[jax.ShapeDtypeStruct((t, inner), BF16),
                   jax.ShapeDtypeStruct((bsz, inner, nstate), F32)],
        scratch_shapes=[pltpu.VMEM((gw, nstate), F32), pltpu.VMEM((nheads, lc), F32)],
        compiler_params=_params("parallel", "parallel", "arbitrary"),
        name="ssd_scan",
    )(xbc, xbc, xbc, pm, sm, dt_bias.reshape(1, nheads), a_log.reshape(1, nheads), d.reshape(1, nheads),
      gn.reshape(1, inner), s0, tril, tril3, eye)
    return y, s_new


def _gdn_heads(qs, ks, vs, gates, betas, cs_cols, cs_rows, states, tri, eye, gn):
    n = len(qs)
    hd = range(n)
    lc, dk = qs[0].shape
    dv = vs[0].shape[1]
    cs_last = [cs_rows[i][:, lc - 1:lc] for i in hd]
    decay = [jnp.exp(jnp.where(tri, cs_cols[i] - cs_rows[i], MASKED)) for i in hd]
    qn = [qs[i] * (lax.rsqrt(jnp.sum(qs[i] * qs[i], axis=1, keepdims=True) + 1e-6) * dk ** -0.5) for i in hd]
    kn = [ks[i] * lax.rsqrt(jnp.sum(ks[i] * ks[i], axis=1, keepdims=True) + 1e-6) for i in hd]
    kb = [kn[i] * betas[i] for i in hd]
    kn16 = [kn[i].astype(BF16) for i in hd]
    off_diag = 1.0 - eye
    a = [_dot_nt(kb[i].astype(BF16), kn16[i]) * decay[i] * off_diag for i in hd]
    qk = [_dot_nt(qn[i].astype(BF16), kn16[i]) * decay[i] for i in hd]
    s16 = [states[i].astype(BF16) for i in hd]
    o_state = [_dot((qn[i] * jnp.exp(cs_cols[i])).astype(BF16), s16[i]) for i in hd]
    tinv = [eye - a[i] for i in hd]
    p = [_dot_x3(a[i], a[i]) for i in hd]
    steps = max(1, int(np.ceil(np.log2(lc))) - 1)
    for st in range(steps):
        tinv = [tinv[i] + _dot_x3(tinv[i], p[i]) for i in hd]
        if st + 1 < steps:
            p = [_dot_x3(p[i], p[i]) for i in hd]
    rhs = [jnp.concatenate([vs[i] * betas[i], kb[i] * jnp.exp(cs_cols[i])], axis=1).astype(BF16) for i in hd]
    sol = [_dot(tinv[i].astype(BF16), rhs[i]) for i in hd]
    vn16 = [(sol[i][:, :dv] - _dot(sol[i][:, dv:].astype(BF16), s16[i])).astype(BF16) for i in hd]
    o = [o_state[i] + _dot(qk[i].astype(BF16), vn16[i]) for i in hd]
    k_end = [(kn[i] * jnp.exp(cs_last[i] - cs_cols[i])).astype(BF16) for i in hd]
    s_new = [states[i] * jnp.exp(cs_last[i]) + _dot_tn(k_end[i], vn16[i]) for i in hd]
    outs = [_rms_rows(o[i], gn) * _silu(gates[i]) for i in hd]
    return outs, s_new


def _gdn_body(q_ref, k_ref, v_ref, gate_ref, sm_ref, alog_ref, dtb_ref, gn_ref, s0_ref, tril_ref, tril3_ref, eye_ref,
              eyeh_ref, o_ref, so_ref, s_scr, cst_scr, *, nc, nheads, a_col0, hps):
    hb = pl.program_id(1)
    c = pl.program_id(2)
    d = q_ref.shape[1] // hps

    @pl.when(c == 0)
    def _():
        s_scr[...] = s0_ref[0]

    a_raw = sm_ref[:, a_col0:a_col0 + nheads]
    b_raw = sm_ref[:, a_col0 + nheads:a_col0 + 2 * nheads]
    gl = -jnp.exp(alog_ref[...]) * _softplus(a_raw + dtb_ref[...])
    beta_all = _sigmoid(b_raw)
    cs = _mask_dot(tril3_ref[...], gl)
    cst_scr[...] = _mask_dot_nt(eyeh_ref[...], cs)
    tri = tril_ref[...].astype(F32) > 0.0
    eye = eye_ref[...].astype(F32)
    gn = gn_ref[...]
    sls = [slice(j * d, (j + 1) * d) for j in range(hps)]
    hidx = [hb * hps + j for j in range(hps)]
    outs, s_new = _gdn_heads(
        [q_ref[:, sl] for sl in sls], [k_ref[:, sl] for sl in sls], [v_ref[:, sl] for sl in sls],
        [gate_ref[:, sl] for sl in sls], [_pick_lane(beta_all, h) for h in hidx], [_pick_lane(cs, h) for h in hidx],
        [cst_scr[pl.ds(h, 1), :] for h in hidx], [s_scr[j] for j in range(hps)], tri, eye, gn)
    for j in range(hps):
        s_scr[j] = s_new[j]
        o_ref[:, sls[j]] = outs[j].astype(o_ref.dtype)

    @pl.when(c == nc - 1)
    def _():
        so_ref[0] = s_scr[...]


GDN_HEADS_PER_STEP = 16


def _gdn(qkv, pm, gate_col0, sm, a_col0, a_log, dt_bias, gn, s0, bsz, l, nheads, dk, dv):
    hps = GDN_HEADS_PER_STEP
    w = hps * dv
    assert dk == dv and gate_col0 % w == 0 and nheads % hps == 0
    t = bsz * l
    lc = CHUNK
    nc = l // lc
    nhb = nheads // hps
    tril, tril3, eye = _tri_masks(lc)
    eyeh = jnp.asarray(np.eye(nheads, dtype=np.float32), dtype=BF16)
    rows = lambda bi, hi, ci: bi * nc + ci
    const2 = lambda bi, hi, ci: (0, 0)
    col = lambda k: pl.BlockSpec((lc, w), lambda bi, hi, ci: (rows(bi, hi, ci), k * nhb + hi))
    state = pl.BlockSpec((1, hps, dk, dv), lambda bi, hi, ci: (bi, hi, 0, 0))
    o, s_new = pl.pallas_call(
        functools.partial(_gdn_body, nc=nc, nheads=nheads, a_col0=a_col0, hps=hps),
        grid=(bsz, nhb, nc),
        in_specs=[col(0), col(1), col(2),
                  pl.BlockSpec((lc, w), lambda bi, hi, ci: (rows(bi, hi, ci), gate_col0 // w + hi)),
                  pl.BlockSpec((lc, sm.shape[1]), lambda bi, hi, ci: (rows(bi, hi, ci), 0)),
                  pl.BlockSpec((1, nheads), const2),
                  pl.BlockSpec((1, nheads), const2),
                  pl.BlockSpec((1, dv), const2),
                  state,
                  pl.BlockSpec((lc, lc), const2),
                  pl.BlockSpec((lc, 3 * lc), const2),
                  pl.BlockSpec((lc, lc), const2),
                  pl.BlockSpec((nheads, nheads), const2)],
        out_specs=[pl.BlockSpec((lc, w), lambda bi, hi, ci: (rows(bi, hi, ci), hi)), state],
        out_shape=[jax.ShapeDtypeStruct((t, nheads * dv), BF16),
                   jax.ShapeDtypeStruct((bsz, nheads, dk, dv), F32)],
        scratch_shapes=[pltpu.VMEM((hps, dk, dv), F32), pltpu.VMEM((nheads, lc), F32)],
        compiler_params=_params("parallel", "parallel", "arbitrary"),
        name="gdn_scan",
    )(qkv, qkv, qkv, pm, sm, a_log.reshape(1, nheads), dt_bias.reshape(1, nheads), gn.reshape(1, dv), s0,
      tril, tril3, eye, eyeh)
    return o, s_new


def _hgrn_levels(lc):
    hs = []
    h = lc // 2
    while h >= 1:
        hs.append(h)
        h //= 2
    return hs


def _hgrn_masks(lc):
    r = np.arange(lc)
    t, u = r[:, None], r[None, :]
    mats = [u <= t, u > t]
    for h in _hgrn_levels(lc):
        start = (t // h) * h
        mats.append((u >= start) & (u <= t))
        mats.append((u > t) & (u <= start + h - 1))
    return jnp.asarray(np.tile(np.concatenate(mats, axis=0).astype(np.float32), (1, 3)), dtype=BF16)


def _hgrn_heads(qs, hfs, vs, gates, lbs, gn, sts, mst16):
    n = len(qs)
    hd = range(n)
    lc, d = qs[0].shape
    logf = [jnp.log(lbs[i] + (1.0 - lbs[i]) * _sigmoid(hfs[i])) for i in hd]
    kk = [(1.0 - lbs[i]) * _sigmoid(-hfs[i]) for i in hd]
    cums = [_mask_dot(mst16, logf[i]) for i in hd]
    v16 = [vs[i].astype(BF16) for i in hd]
    st16 = [sts[i].astype(BF16) for i in hd]
    o_state = [_dot_nt((qs[i] * jnp.exp(cums[i][0:lc])).astype(BF16), st16[i]) for i in hd]

    row = lax.broadcasted_iota(jnp.int32, (lc, 1), 0)
    ti = lax.broadcasted_iota(jnp.int32, (lc, lc), 0)
    si = lax.broadcasted_iota(jnp.int32, (lc, lc), 1)
    att = [jnp.where(ti == si, jnp.sum(qs[i] * kk[i], axis=1, keepdims=True), 0.0) for i in hd]
    for li, hs in enumerate(_hgrn_levels(lc)):
        sh = int(np.log2(hs))
        upper = ((row >> sh) & 1) == 1
        same_block = (ti >> (sh + 1)) == (si >> (sh + 1))
        for i in hd:
            seg = cums[i][(2 + 2 * li) * lc:(3 + 2 * li) * lc]
            rest = cums[i][(3 + 2 * li) * lc:(4 + 2 * li) * lc]
            ql = jnp.where(upper, qs[i] * jnp.exp(seg), 0.0)
            kl = jnp.where(upper, 0.0, kk[i] * jnp.exp(rest))
            att[i] = att[i] + jnp.where(same_block, _dot_nt(ql.astype(BF16), kl.astype(BF16)), 0.0)

    outs, st_new = [], []
    for i in hd:
        o = o_state[i] + _dot(att[i].astype(BF16), v16[i])
        k_end = kk[i] * jnp.exp(cums[i][lc:2 * lc])
        st_new.append(sts[i] * jnp.exp(cums[i][lc - 1:lc, :]) + _dot_tn(v16[i], k_end.astype(BF16)))
        outs.append(_rms_rows(o, gn) * _silu(gates[i]))
    return outs, st_new


def _hgrn_body(q_ref, f_ref, i_ref, gate_ref, lb_ref, gn_ref, s0_ref, mst_ref, o_ref, so_ref, st_scr, *, nc, hps):
    c = pl.program_id(2)
    d = q_ref.shape[1] // hps

    @pl.when(c == 0)
    def _():
        for j in range(hps):
            st_scr[j] = s0_ref[0, j].T

    mst16 = mst_ref[...]
    gn = gn_ref[...]
    sls = [slice(j * d, (j + 1) * d) for j in range(hps)]
    outs, st_new = _hgrn_heads(
        [q_ref[:, sl] for sl in sls], [f_ref[:, sl] for sl in sls], [i_ref[:, sl] for sl in sls],
        [gate_ref[:, sl] for sl in sls], [lb_ref[:, sl] for sl in sls], gn, [st_scr[j] for j in range(hps)], mst16)
    for j in range(hps):
        st_scr[j] = st_new[j]
        o_ref[:, sls[j]] = outs[j].astype(o_ref.dtype)

    @pl.when(c == nc - 1)
    def _():
        for j in range(hps):
            so_ref[0, j] = st_scr[j].T


HGRN_HEADS_PER_STEP = 32


def _hgrn(pm, lb, gn, s0, bsz, l, nheads, dk):
    hps = HGRN_HEADS_PER_STEP
    assert nheads % hps == 0
    w = hps * dk
    nhb = nheads // hps
    t = bsz * l
    lc = CHUNK
    nc = l // lc
    mst = _hgrn_masks(lc)
    rows = lambda bi, hi, ci: bi * nc + ci
    const2 = lambda bi, hi, ci: (0, 0)
    col = lambda k: pl.BlockSpec((lc, w), lambda bi, hi, ci: (rows(bi, hi, ci), k * nhb + hi))
    state = pl.BlockSpec((1, hps, dk, dk), lambda bi, hi, ci: (bi, hi, 0, 0))
    o, s_new = pl.pallas_call(
        functools.partial(_hgrn_body, nc=nc, hps=hps),
        grid=(bsz, nhb, nc),
        in_specs=[col(0), col(1), col(2), col(3),
                  pl.BlockSpec((1, w), lambda bi, hi, ci: (0, hi)),
                  pl.BlockSpec((1, dk), const2),
                  state,
                  pl.BlockSpec(mst.shape, const2)],
        out_specs=[pl.BlockSpec((lc, w), lambda bi, hi, ci: (rows(bi, hi, ci), hi)), state],
        out_shape=[jax.ShapeDtypeStruct((t, nheads * dk), BF16),
                   jax.ShapeDtypeStruct((bsz, nheads, dk, dk), F32)],
        scratch_shapes=[pltpu.VMEM((hps, dk, dk), F32)],
        compiler_params=_params("parallel", "parallel", "arbitrary"),
        name="hgrn_scan",
    )(pm, pm, pm, pm, lb.reshape(1, nheads * dk), gn.reshape(1, dk), s0, mst)
    return o, s_new


MLA_SUB_ROWS = 1024


def _mla_body(qi_ref, ki_ref, last_ref, q_ref, k_ref, o_ref, m_scr, l_scr, acc_scr, *, tq, tk, t_past, hs):
    step = pl.program_id(1)
    iq = qi_ref[step]
    ikv = ki_ref[step]
    nh, _, dq = q_ref.shape
    r = o_ref.shape[2]
    rows = hs * tq
    first_limit = t_past + iq * tq + CHUNK

    @pl.when(ikv == 0)
    def _():
        m_scr[...] = jnp.full_like(m_scr, MASKED)
        l_scr[...] = jnp.zeros_like(l_scr)
        acc_scr[...] = jnp.zeros_like(acc_scr)

    def update(masked):
        kc = k_ref[0]
        vals = kc[:, 0:r]
        if masked:
            tok = lax.broadcasted_iota(jnp.int32, (rows, 1), 0) & (tq - 1)
            limit = t_past + iq * tq + ((tok >> CHUNK_SHIFT) + 1) * CHUNK
            visible = (ikv * tk + lax.broadcasted_iota(jnp.int32, (1, tk), 1)) < limit

        def scores(g):
            return _dot_nt(q_ref[g * hs:(g + 1) * hs].reshape(rows, dq), kc)

        ngroups = nh // hs
        s_next = scores(0)
        for g in range(ngroups):
            rsl = slice(g * rows, (g + 1) * rows)
            s = s_next
            if g + 1 < ngroups:
                s_next = scores(g + 1)
            if masked:
                s = jnp.where(visible, s, MASKED)
            m_old = m_scr[rsl]
            m_new = jnp.maximum(m_old, jnp.max(s, axis=1, keepdims=True))
            alpha = jnp.exp(m_old - m_new)
            p = jnp.exp(s - m_new)
            l_scr[rsl] = alpha * l_scr[rsl] + jnp.sum(p, axis=1, keepdims=True)
            acc_scr[rsl] = alpha * acc_scr[rsl] + _dot(p.astype(BF16), vals)
            m_scr[rsl] = m_new

    fully_visible = (ikv + 1) * tk <= first_limit
    pl.when(fully_visible)(lambda: update(False))
    pl.when(jnp.logical_not(fully_visible))(lambda: update(True))

    @pl.when(last_ref[step] == 1)
    def _():
        o = acc_scr[...] / l_scr[...]
        o_ref[...] = o.reshape(nh, tq, r).astype(o_ref.dtype)


def _mla_attention(q, k, r, bsz, l, t_past, tq, tk):
    nh, t, dq = q.shape
    tkeys = k.shape[1]
    assert l % tq == 0 and tq % CHUNK == 0 and t_past % CHUNK == 0 and tkeys % tk == 0 and tkeys >= t_past + l
    assert tq & (tq - 1) == 0 and k.shape[2] == dq and r % LANES == 0
    nq = l // tq
    hs = min(nh, max(1, MLA_SUB_ROWS // tq))
    assert nh % hs == 0
    qi, ki, last = [], [], []
    for qb in range(nq):
        need = -(-(t_past + (qb + 1) * tq) // tk)
        qi += [qb] * need
        ki += list(range(need))
        last += [0] * (need - 1) + [1]
    tables = [jnp.asarray(np.asarray(a, np.int32)) for a in (qi, ki, last)]
    qmap = lambda bi, si, qi_ref, ki_ref, last_ref: (0, bi * nq + qi_ref[si], 0)
    kmap = lambda bi, si, qi_ref, ki_ref, last_ref: (bi, ki_ref[si], 0)
    return pl.pallas_call(
        functools.partial(_mla_body, tq=tq, tk=tk, t_past=t_past, hs=hs),
        grid_spec=pltpu.PrefetchScalarGridSpec(
            num_scalar_prefetch=3,
            grid=(bsz, len(qi)),
            in_specs=[pl.BlockSpec((nh, tq, dq), qmap),
                      pl.BlockSpec((1, tk, dq), kmap)],
            out_specs=pl.BlockSpec((nh, tq, r), qmap),
            scratch_shapes=[pltpu.VMEM((nh * tq, 1), F32), pltpu.VMEM((nh * tq, 1), F32),
                            pltpu.VMEM((nh * tq, r), F32)]),
        out_shape=jax.ShapeDtypeStruct((nh, t, r), BF16),
        compiler_params=_params("parallel", "arbitrary"),
        name="mla_attention",
    )(*tables, q, k)


def _xattn_body(q_ref, k_ref, v_ref, o_ref, *, nheads, hd):
    q = q_ref[...]
    k = k_ref[0]
    v = v_ref[0]
    outs = []
    for h in range(nheads):
        sl = slice(h * hd, (h + 1) * hd)
        s = _dot_nt(q[:, sl], k[:, sl]) * hd ** -0.5
        s = s - jnp.max(s, axis=1, keepdims=True)
        p = jnp.exp(s)
        p = p / jnp.sum(p, axis=1, keepdims=True)
        outs.append(_dot(p.astype(BF16), v[:, sl]))
    o_ref[...] = jnp.concatenate(outs, axis=1).astype(o_ref.dtype)


def _cross_attention(q, mem_k, mem_v, bsz, l, nheads, hd, tl_cap=512):
    t, d = q.shape
    m = mem_k.shape[1]
    tl = _tile(l, tl_cap, 8)
    nl = l // tl
    return pl.pallas_call(
        functools.partial(_xattn_body, nheads=nheads, hd=hd),
        grid=(bsz, nl),
        in_specs=[pl.BlockSpec((tl, d), lambda bi, li: (bi * nl + li, 0)),
                  pl.BlockSpec((1, m, d), lambda bi, li: (bi, 0, 0)),
                  pl.BlockSpec((1, m, d), lambda bi, li: (bi, 0, 0))],
        out_specs=pl.BlockSpec((tl, d), lambda bi, li: (bi * nl + li, 0)),
        out_shape=jax.ShapeDtypeStruct((t, d), BF16),
        compiler_params=_params("parallel", "parallel"),
        name="cross_attention",
    )(q, mem_k, mem_v)


SSD_HEADS, SSD_HEAD_DIM, SSD_GROUPS, SSD_STATE = 64, 64, 8, 128
SSD_INNER = SSD_HEADS * SSD_HEAD_DIM
SSD_CONV_DIM = SSD_INNER + 2 * SSD_GROUPS * SSD_STATE
GDN_HEADS, GDN_DK, GDN_DV = 32, 128, 128
GDN_CONV_DIM = 2 * GDN_HEADS * GDN_DK + GDN_HEADS * GDN_DV
MLA_HEADS, MLA_Q_RANK, MLA_KV_RANK, MLA_NOPE, MLA_ROPE, MLA_V = 32, 1024, 512, 128, 64, 128
HG_HEADS, HG_DK = 32, 128
HG_W = HG_HEADS * HG_DK
XA_HEADS, XA_HEAD_DIM = 4, 128
MLA_TK = 1024
MLA_TQ = 128


def _rope_tables(pos, p):
    half = p // 2
    inv = ROPE_THETA ** (-jnp.arange(half, dtype=F32) / half)
    ang = pos[:, None] * inv[None, :]
    return jnp.cos(ang), jnp.sin(ang)


def _prep_weights(w):
    out = {}
    e_sizes = [SSD_INNER, SSD_CONV_DIM, SSD_HEADS, GDN_CONV_DIM, GDN_HEADS, GDN_HEADS, GDN_HEADS * GDN_DV]
    e_off = np.concatenate([[0], np.cumsum(e_sizes)])
    wi = w["w_in_even"]
    sec = lambda a, offs, i: a[:, :, offs[i]:offs[i + 1]]
    out["w_in_even_main"] = jnp.concatenate(
        [sec(wi, e_off, 0), sec(wi, e_off, 1), sec(wi, e_off, 3), sec(wi, e_off, 6)], axis=-1).astype(BF16)
    out["w_in_even_small"] = jnp.concatenate(
        [sec(wi, e_off, 2), sec(wi, e_off, 4), sec(wi, e_off, 5)], axis=-1).astype(BF16)
    o_sizes = [MLA_Q_RANK, MLA_KV_RANK, MLA_ROPE, HG_W, HG_W, HG_W, HG_W]
    o_off = np.concatenate([[0], np.cumsum(o_sizes)])
    wo = w["w_in_odd"]
    out["w_in_odd_main"] = jnp.concatenate(
        [sec(wo, o_off, 3), sec(wo, o_off, 4), sec(wo, o_off, 5), sec(wo, o_off, 6), sec(wo, o_off, 0)],
        axis=-1).astype(BF16)
    out["w_in_odd_small"] = jnp.concatenate([sec(wo, o_off, 1), sec(wo, o_off, 2)], axis=-1).astype(BF16)
    n_odd = wo.shape[0]
    uq = w["mla_w_uq"].reshape(n_odd, MLA_Q_RANK, MLA_HEADS, MLA_NOPE + MLA_ROPE)
    out["w_uq"] = jnp.concatenate(
        [uq[..., :MLA_NOPE].reshape(n_odd, MLA_Q_RANK, MLA_HEADS * MLA_NOPE),
         uq[..., MLA_NOPE:].reshape(n_odd, MLA_Q_RANK, MLA_HEADS * MLA_ROPE)], axis=-1).astype(BF16)
    out["w_uk"] = jnp.transpose(w["mla_w_uk"], (0, 2, 3, 1)).astype(BF16)
    out["w_uv"] = jnp.transpose(w["mla_w_uv"], (0, 2, 1, 3)).astype(BF16)
    for name in ("w_out_even", "w_out_odd", "xa_w_q", "xa_w_k", "xa_w_v", "xa_w_o", "ffn_w1", "ffn_w2"):
        out[name] = w[name].astype(BF16)
    return out


def _even_mixer(hn, w, wb, e, bsz, l, ssd_s, ssd_buf, gdn_s, gdn_buf):
    pm = _matmul([hn], [wb["w_in_even_main"][e]], F32)
    sm = _matmul([hn], [wb["w_in_even_small"][e]], F32)
    xbc_col0, qkv_col0 = SSD_INNER, SSD_INNER + SSD_CONV_DIM
    gate_col0 = qkv_col0 + GDN_CONV_DIM
    xbc = _conv_silu(pm, xbc_col0, SSD_CONV_DIM, ssd_buf, w["ssd_conv_w"][e], w["ssd_conv_b"][e], bsz, l)
    qkv = _conv_silu(pm, qkv_col0, GDN_CONV_DIM, gdn_buf, w["gdn_conv_w"][e],
                     jnp.zeros((GDN_CONV_DIM,), F32), bsz, l)
    y, ssd_new = _ssd(xbc, pm, 0, sm, w["ssd_dt_bias"][e], w["ssd_a_log"][e], w["ssd_d"][e], w["ssd_norm"][e],
                      ssd_s.reshape(bsz, SSD_INNER, SSD_STATE), bsz, l, SSD_HEADS, SSD_HEAD_DIM, SSD_GROUPS,
                      SSD_STATE)
    o, gdn_new = _gdn(qkv, pm, gate_col0, sm, SSD_HEADS, w["gdn_a_log"][e], w["gdn_dt_bias"][e], w["gdn_norm"][e],
                      gdn_s, bsz, l, GDN_HEADS, GDN_DK, GDN_DV)
    w_out = wb["w_out_even"][e]
    mix = _matmul([y, o], [w_out, w_out], BF16, b_rows=[0, SSD_INNER])
    pm3 = pm.reshape(bsz, l, pm.shape[1])
    tail = pm3[:, l - (CONV_W - 1):]
    states = (ssd_new.reshape(bsz, SSD_HEADS, SSD_HEAD_DIM, SSD_STATE), tail[:, :, xbc_col0:qkv_col0],
              gdn_new, tail[:, :, qkv_col0:gate_col0])
    return mix, states


def _odd_mixer(hn, w, wb, j, lb, bsz, l, ckv_past, kr_past, hg_s):
    t_past = ckv_past.shape[1]
    pm = _matmul([hn], [wb["w_in_odd_main"][j]], F32)
    sm = _matmul([hn], [wb["w_in_odd_small"][j]], F32)
    posf = (t_past + jnp.arange(l, dtype=jnp.int32)).astype(F32)
    scale = (MLA_NOPE + MLA_ROPE) ** -0.5
    cqn = _rms(pm, w["mla_q_norm"][j], BF16, col_block=4 * HG_W // MLA_Q_RANK, width=MLA_Q_RANK)
    q = _matmul([cqn], [wb["w_uq"][j]], F32)
    lane_pad = (-MLA_ROPE) % LANES
    cos, sin = _rope_tables(posf, MLA_ROPE)
    cos2 = jnp.tile(jnp.concatenate([cos, cos], axis=1), (bsz, 1))
    sin2 = jnp.tile(jnp.concatenate([-sin, sin], axis=1), (bsz, 1))
    q_att = _head_split_matmul(q, MLA_HEADS * MLA_NOPE, wb["w_uk"][j], scale, cos2, sin2)
    c_kv = _rms(sm, w["mla_kv_norm"][j], F32, col_block=0, width=MLA_KV_RANK).reshape(bsz, l, MLA_KV_RANK)
    kr = sm[:, MLA_KV_RANK:].reshape(bsz, l, MLA_ROPE)
    kr1, kr2 = kr[..., :MLA_ROPE // 2], kr[..., MLA_ROPE // 2:]
    k_rope = jnp.concatenate([kr1 * cos - kr2 * sin, kr2 * cos + kr1 * sin], axis=-1)
    tkeys = t_past + l
    tk = _tile(tkeys, MLA_TK, CHUNK)
    k_att = jnp.concatenate([jnp.concatenate([ckv_past, c_kv], axis=1).astype(BF16),
                             jnp.concatenate([kr_past, k_rope], axis=1).astype(BF16)], axis=2)
    k_att = jnp.pad(k_att, ((0, 0), (0, 0), (0, lane_pad)))
    tq = MLA_TQ if l % MLA_TQ == 0 else CHUNK
    o_lat = _mla_attention(q_att, k_att, MLA_KV_RANK, bsz, l, t_past, tq, tk)
    o_mla = _head_merge_matmul(o_lat, wb["w_uv"][j])
    o_hg, hg_new = _hgrn(pm, lb, w["hg_norm"][j], hg_s, bsz, l, HG_HEADS, HG_DK)
    w_out = wb["w_out_odd"][j]
    mix = _matmul([o_mla, o_hg], [w_out, w_out], BF16, b_rows=[0, MLA_HEADS * MLA_V])
    return mix, (c_kv, k_rope, hg_new)


def _forward(x, mem_k, mem_v, ssd_s, ssd_buf, gdn_s, gdn_buf, ckv_past, kr_past, hg_s, w, wb, lbs):
    bsz, l, d = x.shape
    depth = w["norm_g"].shape[0]
    x = x.reshape(bsz * l, d)
    hn = _rms(x, w["norm_g"][0, 0], BF16)
    ev, od = [], []
    for layer in range(depth):
        ng = w["norm_g"][layer]
        if layer % 2 == 0:
            e = layer // 2
            mix, st = _even_mixer(hn, w, wb, e, bsz, l, ssd_s[e], ssd_buf[e], gdn_s[e], gdn_buf[e])
            ev.append(st)
        else:
            j = layer // 2
            mix, st = _odd_mixer(hn, w, wb, j, lbs[layer], bsz, l, ckv_past[j], kr_past[j], hg_s[j])
            od.append(st)
        x, hn = _add_rms(x, mix, ng[1], ng[2])
        q = _matmul([hn], [wb["xa_w_q"][layer]], BF16)
        xo = _cross_attention(q, mem_k[layer], mem_v[layer], bsz, l, XA_HEADS, XA_HEAD_DIM)
        xa = _matmul([xo], [wb["xa_w_o"][layer]], BF16)
        x, hn = _add_rms(x, xa, ng[3], ng[4])
        h1 = _matmul([hn], [wb["ffn_w1"][layer]], BF16, act="relu2")
        f = _matmul([h1], [wb["ffn_w2"][layer]], BF16)
        x, hn = _add_rms(x, f, ng[5], w["norm_g"][layer + 1, 0] if layer + 1 < depth else None)
    stack = lambda items, i: jnp.stack([it[i] for it in items])
    return (x.reshape(bsz, l, d), stack(ev, 0), stack(ev, 1), stack(ev, 2), stack(ev, 3),
            stack(od, 0), stack(od, 1), stack(od, 2))


def kernel(x_prompt, x_sample, mem_prompt, state_ssd, state_ssd_conv, state_gdn, state_gdn_conv, cache_mla_ckv, cache_mla_krope, state_hgrn, cache_mem_k, cache_mem_v, norm_g, w_in_even, w_out_even, ssd_conv_w, ssd_conv_b, ssd_dt_bias, ssd_a_log, ssd_d, ssd_norm, gdn_conv_w, gdn_a_log, gdn_dt_bias, gdn_norm, w_in_odd, w_out_odd, mla_q_norm, mla_w_uq, mla_kv_norm, mla_w_uk, mla_w_uv, hg_lower_bound, hg_norm, xa_mem_norm, xa_w_q, xa_w_k, xa_w_v, xa_w_o, ffn_w1, ffn_w2):
    w = dict(norm_g=norm_g, w_in_even=w_in_even, w_out_even=w_out_even, ssd_conv_w=ssd_conv_w, ssd_conv_b=ssd_conv_b,
             ssd_dt_bias=ssd_dt_bias, ssd_a_log=ssd_a_log, ssd_d=ssd_d, ssd_norm=ssd_norm, gdn_conv_w=gdn_conv_w,
             gdn_a_log=gdn_a_log, gdn_dt_bias=gdn_dt_bias, gdn_norm=gdn_norm, w_in_odd=w_in_odd, w_out_odd=w_out_odd,
             mla_q_norm=mla_q_norm, mla_w_uq=mla_w_uq, mla_kv_norm=mla_kv_norm, mla_w_uk=mla_w_uk, mla_w_uv=mla_w_uv,
             hg_norm=hg_norm, xa_w_q=xa_w_q, xa_w_k=xa_w_k, xa_w_v=xa_w_v, xa_w_o=xa_w_o, ffn_w1=ffn_w1,
             ffn_w2=ffn_w2)
    wb = _prep_weights(w)
    depth = norm_g.shape[0]
    n_even, n_odd = (depth + 1) // 2, depth // 2
    lbs = jnp.cumsum(jax.nn.softmax(hg_lower_bound.astype(F32), axis=0), axis=0)
    lbs = lbs - lbs[0]

    b, m, d = mem_prompt.shape
    xa_dim = XA_HEADS * XA_HEAD_DIM
    mk, mv = [], []
    for layer in range(depth):
        mn = _rms(mem_prompt.reshape(b * m, d), xa_mem_norm[layer], BF16)
        mk.append(_matmul([mn], [wb["xa_w_k"][layer]], F32).reshape(b, m, xa_dim))
        mv.append(_matmul([mn], [wb["xa_w_v"][layer]], F32).reshape(b, m, xa_dim))
    p_mem_k = jnp.stack(mk)
    p_mem_v = jnp.stack(mv)
    dt = x_prompt.dtype
    prompt = _forward(
        x_prompt, p_mem_k.astype(BF16), p_mem_v.astype(BF16),
        jnp.zeros((n_even, b, SSD_HEADS, SSD_HEAD_DIM, SSD_STATE), dt),
        jnp.zeros((n_even, b, CONV_W - 1, SSD_CONV_DIM), dt),
        jnp.zeros((n_even, b, GDN_HEADS, GDN_DK, GDN_DV), dt),
        jnp.zeros((n_even, b, CONV_W - 1, GDN_CONV_DIM), dt),
        jnp.zeros((n_odd, b, 0, MLA_KV_RANK), dt),
        jnp.zeros((n_odd, b, 0, MLA_ROPE), dt),
        jnp.zeros((n_odd, b, HG_HEADS, HG_DK, HG_DK), dt),
        w, wb, lbs)
    db = x_sample.shape[0]
    sample = _forward(
        x_sample, cache_mem_k.reshape(depth, db, m, xa_dim).astype(BF16),
        cache_mem_v.reshape(depth, db, m, xa_dim).astype(BF16),
        state_ssd, state_ssd_conv, state_gdn, state_gdn_conv, cache_mla_ckv, cache_mla_krope, state_hgrn,
        w, wb, lbs)
    y_prompt, p_rest = prompt[0], prompt[1:]
    y_sample, s_rest = sample[0], sample[1:]
    return (y_prompt, y_sample, *p_rest,
            p_mem_k.reshape(depth, b, m, XA_HEADS, XA_HEAD_DIM), p_mem_v.reshape(depth, b, m, XA_HEADS, XA_HEAD_DIM),
            *s_rest)
```

```python
import functools

import numpy as np
import jax
import jax.numpy as jnp
from jax import lax
from jax.experimental import pallas as pl
from jax.experimental.pallas import tpu as pltpu

F32 = jnp.float32
BF16 = jnp.bfloat16

VMEM_LIMIT_BYTES = 56 * 1024 * 1024
LANES = 128
NORM_EPS = 1e-6
CHUNK = 64
CHUNK_SHIFT = 6
assert 1 << CHUNK_SHIFT == CHUNK
CONV_W = 4
ROPE_THETA = 10000.0
MASKED = -1e30


def _params(*sem):
    return pltpu.CompilerParams(dimension_semantics=sem, vmem_limit_bytes=VMEM_LIMIT_BYTES)


def _tile(n, cap, mult):
    if n <= cap:
        return n
    for d in range(cap - cap % mult, 0, -mult):
        if n % d == 0:
            return d
    return n


def _dot(a, b):
    return jnp.dot(a, b, preferred_element_type=F32)


def _dot_nt(a, b):
    return lax.dot_general(a, b, (((1,), (1,)), ((), ())), preferred_element_type=F32)


def _dot_tn(a, b):
    return lax.dot_general(a, b, (((0,), (0,)), ((), ())), preferred_element_type=F32)


def _split3(x):
    hi = x.astype(BF16)
    r = x - hi.astype(F32)
    mid = r.astype(BF16)
    lo = (r - mid.astype(F32)).astype(BF16)
    return hi, mid, lo


def _mask_dot(mask3, x):
    return _dot(mask3, jnp.concatenate(_split3(x), axis=0))


def _mask_dot_nt(mask16, x):
    hi, mid, lo = _split3(x)
    return _dot_nt(mask16, hi) + (_dot_nt(mask16, mid) + _dot_nt(mask16, lo))


def _dot_x3(a, b):
    a_hi = a.astype(BF16)
    a_lo = (a - a_hi.astype(F32)).astype(BF16)
    b_hi = b.astype(BF16)
    b_lo = (b - b_hi.astype(F32)).astype(BF16)
    return _dot(a_hi, b_hi) + (_dot(a_hi, b_lo) + _dot(a_lo, b_hi))


def _softplus(x):
    return jnp.maximum(x, 0.0) + jnp.log1p(jnp.exp(-jnp.abs(x)))


def _sigmoid(x):
    return 0.5 * jnp.tanh(0.5 * x) + 0.5


def _silu(x):
    return x * _sigmoid(x)


def _pick_lane(a, idx):
    lane = lax.broadcasted_iota(jnp.int32, a.shape, 1)
    return jnp.sum(jnp.where(lane == idx, a, 0.0), axis=1, keepdims=True)


def _mm_body(*refs, n_ops, nks, act):
    a_refs, b_refs = refs[:n_ops], refs[n_ops:2 * n_ops]
    o_ref, acc_ref = refs[2 * n_ops], refs[2 * n_ops + 1]
    k = pl.program_id(2)
    nk = sum(nks)

    @pl.when(k == 0)
    def _():
        acc_ref[...] = jnp.zeros_like(acc_ref)

    off = 0
    for a_ref, b_ref, n in zip(a_refs, b_refs, nks):
        def step(a_ref=a_ref, b_ref=b_ref):
            acc_ref[...] += _dot(a_ref[...], b_ref[...])
        if n_ops == 1:
            step()
        else:
            pl.when((k >= off) & (k < off + n))(step)
        off += n

    @pl.when(k == nk - 1)
    def _():
        r = acc_ref[...]
        if act == "relu2":
            r = jnp.square(jnp.maximum(r, 0.0))
        o_ref[...] = r.astype(o_ref.dtype)


def _mm_full_k_body(a_ref, b_ref, o_ref, *, act):
    r = _dot(a_ref[...], b_ref[...])
    if act == "relu2":
        r = jnp.square(jnp.maximum(r, 0.0))
    o_ref[...] = r.astype(o_ref.dtype)


FULL_K_MAX = 4096
FULL_K_B_BLOCK_BYTES = 4 * 1024 * 1024


def _matmul(a_list, b_list, out_dtype, act=None, b_rows=None, tm_cap=1024, tn_cap=None, tk_cap=1024):
    m = a_list[0].shape[0]
    n = b_list[0].shape[1]
    b_rows = [0] * len(a_list) if b_rows is None else b_rows
    tm = _tile(m, tm_cap, 8)
    k0 = a_list[0].shape[1]
    if len(a_list) == 1 and k0 <= FULL_K_MAX and b_rows[0] == 0 and b_list[0].shape[0] == k0:
        cap = tn_cap or min(2048, max(512, FULL_K_B_BLOCK_BYTES // (2 * k0)))
        tn = _tile(n, cap, LANES)
        return pl.pallas_call(
            functools.partial(_mm_full_k_body, act=act),
            grid=(m // tm, n // tn),
            in_specs=[pl.BlockSpec((tm, k0), lambda i, j: (i, 0)),
                      pl.BlockSpec((k0, tn), lambda i, j: (0, j))],
            out_specs=pl.BlockSpec((tm, tn), lambda i, j: (i, j)),
            out_shape=jax.ShapeDtypeStruct((m, n), out_dtype),
            compiler_params=_params("parallel", "arbitrary"),
            name="matmul_full_k",
        )(a_list[0], b_list[0])
    tn = _tile(n, tn_cap or 2048, LANES)
    tk = min(_tile(a.shape[1], tk_cap * (2 if len(a_list) == 1 else 1), LANES) for a in a_list)
    nks, offs = [], []
    for a, b, r0 in zip(a_list, b_list, b_rows):
        assert a.shape[0] == m and b.shape[1] == n and a.shape[1] % tk == 0 and r0 % tk == 0
        assert r0 + a.shape[1] <= b.shape[0]
        offs.append(sum(nks))
        nks.append(a.shape[1] // tk)
    in_specs = []
    for o, nki in zip(offs, nks):
        in_specs.append(pl.BlockSpec((tm, tk), lambda i, j, k, o=o, nki=nki: (i, jnp.clip(k - o, 0, nki - 1))))
    for o, nki, r0 in zip(offs, nks, b_rows):
        in_specs.append(pl.BlockSpec(
            (tk, tn), lambda i, j, k, o=o, nki=nki, rb=r0 // tk: (rb + jnp.clip(k - o, 0, nki - 1), j)))
    return pl.pallas_call(
        functools.partial(_mm_body, n_ops=len(a_list), nks=tuple(nks), act=act),
        grid=(m // tm, n // tn, sum(nks)),
        in_specs=in_specs,
        out_specs=pl.BlockSpec((tm, tn), lambda i, j, k: (i, j)),
        out_shape=jax.ShapeDtypeStruct((m, n), out_dtype),
        scratch_shapes=[pltpu.VMEM((tm, tn), F32)],
        compiler_params=_params("parallel", "parallel", "arbitrary"),
        name="matmul",
    )(*a_list, *b_list)


def _head_split_body(a_ref, b_ref, rot_ref, cos_ref, sin_ref, o_ref, *, scale):
    _, dk, n = b_ref.shape
    p = cos_ref.shape[1]
    tm = o_ref.shape[1]
    pair = rot_ref[...]
    for j in range(2):
        a = a_ref[:, j * dk:(j + 1) * dk].astype(BF16)
        o_ref[j, :, 0:n] = (_dot(a, b_ref[j]) * scale).astype(o_ref.dtype)
        x = pair[:, j * p:(j + 1) * p]
        swapped = jnp.concatenate([x[:, p // 2:], x[:, :p // 2]], axis=1)
        o_ref[j, :, n:n + p] = ((x * cos_ref[...] + swapped * sin_ref[...]) * scale).astype(o_ref.dtype)
        o_ref[j, :, n + p:] = jnp.zeros((tm, o_ref.shape[2] - n - p), o_ref.dtype)


def _head_split_matmul(a, rot_col0, w, scale, cos2, sin2, tm_cap=1024):
    t = a.shape[0]
    nh, dk, n = w.shape
    p = cos2.shape[1]
    assert n % LANES == 0 and 2 * p == LANES and nh % 2 == 0 and rot_col0 % LANES == 0
    tm = _tile(t, tm_cap, 8)
    return pl.pallas_call(
        functools.partial(_head_split_body, scale=scale),
        grid=(t // tm, nh // 2),
        in_specs=[pl.BlockSpec((tm, 2 * dk), lambda i, h: (i, h)),
                  pl.BlockSpec((2, dk, n), lambda i, h: (h, 0, 0)),
                  pl.BlockSpec((tm, LANES), lambda i, h: (i, rot_col0 // LANES + h)),
                  pl.BlockSpec((tm, p), lambda i, h: (i, 0)),
                  pl.BlockSpec((tm, p), lambda i, h: (i, 0))],
        out_specs=pl.BlockSpec((2, tm, n + LANES), lambda i, h: (h, i, 0)),
        out_shape=jax.ShapeDtypeStruct((nh, t, n + LANES), BF16),
        compiler_params=_params("parallel", "arbitrary"),
        name="head_split_matmul",
    )(a, w, a, cos2, sin2)


HEAD_MERGE_HEADS_PER_STEP = 2


def _head_merge_body(a_ref, b_ref, o_ref):
    dv = b_ref.shape[2]
    for j in range(b_ref.shape[0]):
        o_ref[:, j * dv:(j + 1) * dv] = _dot(a_ref[j], b_ref[j]).astype(o_ref.dtype)


def _head_merge_matmul(a, w, tm_cap=1024):
    nh, t, k = a.shape
    dv = w.shape[2]
    hps = HEAD_MERGE_HEADS_PER_STEP
    assert nh % hps == 0
    tm = _tile(t, tm_cap, 8)
    return pl.pallas_call(
        _head_merge_body,
        grid=(t // tm, nh // hps),
        in_specs=[pl.BlockSpec((hps, tm, k), lambda i, h: (h, i, 0)),
                  pl.BlockSpec((hps, k, dv), lambda i, h: (h, 0, 0))],
        out_specs=pl.BlockSpec((tm, hps * dv), lambda i, h: (i, h)),
        out_shape=jax.ShapeDtypeStruct((t, nh * dv), BF16),
        compiler_params=_params("parallel", "arbitrary"),
        name="head_merge_matmul",
    )(a, w)


def _rms_rows(x, g):
    return x * lax.rsqrt(jnp.mean(x * x, axis=-1, keepdims=True) + NORM_EPS) * g


def _rms_body(x_ref, g_ref, o_ref):
    o_ref[...] = _rms_rows(x_ref[...], g_ref[...]).astype(o_ref.dtype)


def _rms(x, g, out_dtype, col_block=0, width=None, tr_cap=256):
    t = x.shape[0]
    width = x.shape[1] if width is None else width
    tr = _tile(t, tr_cap, 8)
    return pl.pallas_call(
        _rms_body,
        grid=(t // tr,),
        in_specs=[pl.BlockSpec((tr, width), lambda i: (i, col_block)),
                  pl.BlockSpec((1, width), lambda i: (0, 0))],
        out_specs=pl.BlockSpec((tr, width), lambda i: (i, 0)),
        out_shape=jax.ShapeDtypeStruct((t, width), out_dtype),
        compiler_params=_params("parallel"),
        name="rms",
    )(x, g.reshape(1, width))


def _add_rms_body(x_ref, y_ref, g1_ref, g2_ref, xo_ref, ho_ref):
    xn = x_ref[...] + _rms_rows(y_ref[...].astype(F32), g1_ref[...])
    xo_ref[...] = xn
    ho_ref[...] = _rms_rows(xn, g2_ref[...]).astype(ho_ref.dtype)


def _add_rms_last_body(x_ref, y_ref, g1_ref, xo_ref):
    xo_ref[...] = x_ref[...] + _rms_rows(y_ref[...].astype(F32), g1_ref[...])


def _add_rms(x, y, g1, g2, tr_cap=256):
    t, d = x.shape
    tr = _tile(t, tr_cap, 8)
    row = pl.BlockSpec((tr, d), lambda i: (i, 0))
    gain = pl.BlockSpec((1, d), lambda i: (0, 0))
    if g2 is None:
        out = pl.pallas_call(
            _add_rms_last_body, grid=(t // tr,), in_specs=[row, row, gain], out_specs=row,
            out_shape=jax.ShapeDtypeStruct((t, d), F32), compiler_params=_params("parallel"),
            name="add_rms_last",
        )(x, y, g1.reshape(1, d))
        return out, None
    return pl.pallas_call(
        _add_rms_body, grid=(t // tr,), in_specs=[row, row, gain, gain], out_specs=[row, row],
        out_shape=[jax.ShapeDtypeStruct((t, d), F32), jax.ShapeDtypeStruct((t, d), BF16)],
        compiler_params=_params("parallel"), name="add_rms",
    )(x, y, g1.reshape(1, d), g2.reshape(1, d))


CONV_HALO = 8


def _conv_taps(x, w, b):
    acc = b + x * w[CONV_W - 1:CONV_W, :]
    for j in range(1, CONV_W):
        acc = acc + pltpu.roll(x, j, 0) * w[CONV_W - 1 - j:CONV_W - j, :]
    return acc


def _conv_body(u_ref, buf_ref, w_ref, b_ref, o_ref, hist_ref, *, tl):
    lt = pl.program_id(2)

    @pl.when(lt == 0)
    def _():
        hist_ref[...] = jnp.zeros_like(hist_ref)
        hist_ref[CONV_HALO - (CONV_W - 1):CONV_HALO, :] = buf_ref[0]

    u = u_ref[...]
    w = w_ref[...]
    b = b_ref[...]
    o_ref[...] = _silu(_conv_taps(u, w, b))
    head = jnp.concatenate([hist_ref[...], u[0:CONV_HALO, :]], axis=0)
    o_ref[0:CONV_HALO, :] = _silu(_conv_taps(head, w, b)[CONV_HALO:, :])
    hist_ref[...] = u[tl - CONV_HALO:tl, :]


def _conv_silu(u, col0, c, buf, w, b, bsz, l, tl_cap=512, tc=1024):
    assert l >= CONV_W - 1 and col0 % tc == 0 and c % tc == 0
    tl = _tile(l, tl_cap, 8)
    assert tl >= CONV_HALO
    nl = l // tl
    cb0 = col0 // tc
    return pl.pallas_call(
        functools.partial(_conv_body, tl=tl),
        grid=(bsz, c // tc, nl),
        in_specs=[pl.BlockSpec((tl, tc), lambda bi, ci, li: (bi * nl + li, cb0 + ci)),
                  pl.BlockSpec((1, CONV_W - 1, tc), lambda bi, ci, li: (bi, 0, ci)),
                  pl.BlockSpec((CONV_W, tc), lambda bi, ci, li: (0, ci)),
                  pl.BlockSpec((1, tc), lambda bi, ci, li: (0, ci))],
        out_specs=pl.BlockSpec((tl, tc), lambda bi, ci, li: (bi * nl + li, ci)),
        out_shape=jax.ShapeDtypeStruct((bsz * l, c), F32),
        scratch_shapes=[pltpu.VMEM((CONV_HALO, tc), F32)],
        compiler_params=_params("parallel", "parallel", "arbitrary"),
        name="conv_silu",
    )(u, buf, w, b.reshape(1, c))


def _tri_masks(n):
    r = np.arange(n)
    tril = (r[:, None] >= r[None, :]).astype(np.float32)
    return (jnp.asarray(tril, dtype=BF16), jnp.asarray(np.tile(tril, (1, 3)), dtype=BF16),
            jnp.asarray(np.eye(n, dtype=np.float32), dtype=BF16))


def _ssd_body(x_ref, bm_ref, cm_ref, z_ref, sm_ref, dtb_ref, alog_ref, d_ref, gn_ref, s0_ref, tril_ref, tril3_ref,
              eye_ref, y_ref, so_ref, s_scr, cst_scr, *, nc, hpg, hd, nheads, gps):
    g0 = pl.program_id(1) * gps
    c = pl.program_id(2)
    lc = x_ref.shape[0]
    gw = hpg * hd
    nstate = bm_ref.shape[1] // gps

    @pl.when(c == 0)
    def _():
        s_scr[...] = s0_ref[0]

    dt = _softplus(sm_ref[:, 0:nheads] + dtb_ref[...])
    da = dt * (-jnp.exp(alog_ref[...]))
    cs = _mask_dot(tril3_ref[...], da)
    cst_scr[...] = _mask_dot_nt(eye_ref[...], cs)
    tri = tril_ref[...].astype(F32) > 0.0

    x = x_ref[...]
    gs = range(gps)
    bm = [bm_ref[:, gi * nstate:(gi + 1) * nstate].astype(BF16) for gi in gs]
    cm = [cm_ref[:, gi * nstate:(gi + 1) * nstate].astype(BF16) for gi in gs]
    s = [s_scr[gi * gw:(gi + 1) * gw, :] for gi in gs]
    cb = [_dot_nt(cm[gi], bm[gi]) for gi in gs]
    y_state = [_dot_nt(cm[gi], s[gi].astype(BF16)) for gi in gs]
    js = range(gps * hpg)
    hd_ids = [g0 * hpg + j for j in js]
    cs_col = [_pick_lane(cs, h) for h in hd_ids]
    dt_col = [_pick_lane(dt, h) for h in hd_ids]
    d_h = [_pick_lane(d_ref[...], h) for h in hd_ids]
    cs_row = [cst_scr[pl.ds(h, 1), :] for h in hd_ids]
    cs_last = [cs_row[j][:, lc - 1:lc] for j in js]
    mix = [(cb[j // hpg] * jnp.exp(jnp.where(tri, cs_col[j] - cs_row[j], MASKED))).astype(BF16) for j in js]
    xh = [x[:, j * hd:(j + 1) * hd] for j in js]
    xdt = [xh[j] * dt_col[j] for j in js]
    y_in = [_dot(mix[j], xdt[j].astype(BF16)) for j in js]
    ys = [y_in[j] + y_state[j // hpg][:, (j % hpg) * hd:(j % hpg + 1) * hd] * jnp.exp(cs_col[j]) + d_h[j] * xh[j]
          for j in js]
    xws = [xdt[j] * jnp.exp(cs_last[j] - cs_col[j]) for j in js]
    decs = [jnp.broadcast_to(jnp.exp(cs_last[j]), (hd, nstate)) for j in js]
    for gi in gs:
        hsl = slice(gi * hpg, (gi + 1) * hpg)
        xw = jnp.concatenate(xws[hsl], axis=1)
        s_new = s[gi] * jnp.concatenate(decs[hsl], axis=0) + _dot_tn(xw.astype(BF16), bm[gi])
        s_scr[gi * gw:(gi + 1) * gw, :] = s_new
        y = jnp.concatenate(ys[hsl], axis=1) * _silu(z_ref[:, gi * gw:(gi + 1) * gw])
        y_ref[:, gi * gw:(gi + 1) * gw] = _rms_rows(y, gn_ref[:, gi * gw:(gi + 1) * gw]).astype(y_ref.dtype)

    @pl.when(c == nc - 1)
    def _():
        so_ref[0] = s_scr[...]


SSD_GROUPS_PER_STEP = 8


def _ssd(xbc, pm, z_col0, sm, dt_bias, a_log, d, gn, s0, bsz, l, nheads, hd, ngroups, nstate):
    t = bsz * l
    lc = CHUNK
    nc = l // lc
    hpg = nheads // ngroups
    gps = SSD_GROUPS_PER_STEP
    gw = gps * hpg * hd
    sw = gps * nstate
    inner = nheads * hd
    assert l % lc == 0 and nheads == lc and z_col0 % gw == 0 and ngroups % gps == 0 and inner % sw == 0
    tril, tril3, eye = _tri_masks(lc)
    rows = lambda bi, gi, ci: bi * nc + ci
    const2 = lambda bi, gi, ci: (0, 0)
    y, s_new = pl.pallas_call(
        functools.partial(_ssd_body, nc=nc, hpg=hpg, hd=hd, nheads=nheads, gps=gps),
        grid=(bsz, ngroups // gps, nc),
        in_specs=[pl.BlockSpec((lc, gw), lambda bi, gi, ci: (rows(bi, gi, ci), gi)),
                  pl.BlockSpec((lc, sw), lambda bi, gi, ci: (rows(bi, gi, ci), inner // sw + gi)),
                  pl.BlockSpec((lc, sw), lambda bi, gi, ci: (rows(bi, gi, ci), (inner + ngroups * nstate) // sw + gi)),
                  pl.BlockSpec((lc, gw), lambda bi, gi, ci: (rows(bi, gi, ci), z_col0 // gw + gi)),
                  pl.BlockSpec((lc, sm.shape[1]), lambda bi, gi, ci: (rows(bi, gi, ci), 0)),
                  pl.BlockSpec((1, nheads), const2),
                  pl.BlockSpec((1, nheads), const2),
                  pl.BlockSpec((1, nheads), const2),
                  pl.BlockSpec((1, gw), lambda bi, gi, ci: (0, gi)),
                  pl.BlockSpec((1, gw, nstate), lambda bi, gi, ci: (bi, gi, 0)),
                  pl.BlockSpec((lc, lc), const2),
                  pl.BlockSpec((lc, 3 * lc), const2),
                  pl.BlockSpec((lc, lc), const2)],
        out_specs=[pl.BlockSpec((lc, gw), lambda bi, gi, ci: (rows(bi, gi, ci), gi)),
                   pl.BlockSpec((1, gw, nstate), lambda bi, gi, ci: (bi, gi, 0))],
        out_shape=[jax.ShapeDtypeStruct((t, inner), BF16),
                   jax.ShapeDtypeStruct((bsz, inner, nstate), F32)],
        scratch_shapes=[pltpu.VMEM((gw, nstate), F32), pltpu.VMEM((nheads, lc), F32)],
        compiler_params=_params("parallel", "parallel", "arbitrary"),
        name="ssd_scan",
    )(xbc, xbc, xbc, pm, sm, dt_bias.reshape(1, nheads), a_log.reshape(1, nheads), d.reshape(1, nheads),
      gn.reshape(1, inner), s0, tril, tril3, eye)
    return y, s_new


def _gdn_heads(qs, ks, vs, gates, betas, cs_cols, cs_rows, states, tri, eye, gn):
    n = len(qs)
    hd = range(n)
    lc, dk = qs[0].shape
    dv = vs[0].shape[1]
    cs_last = [cs_rows[i][:, lc - 1:lc] for i in hd]
    decay = [jnp.exp(jnp.where(tri, cs_cols[i] - cs_rows[i], MASKED)) for i in hd]
    qn = [qs[i] * (lax.rsqrt(jnp.sum(qs[i] * qs[i], axis=1, keepdims=True) + 1e-6) * dk ** -0.5) for i in hd]
    kn = [ks[i] * lax.rsqrt(jnp.sum(ks[i] * ks[i], axis=1, keepdims=True) + 1e-6) for i in hd]
    kb = [kn[i] * betas[i] for i in hd]
    kn16 = [kn[i].astype(BF16) for i in hd]
    off_diag = 1.0 - eye
    a = [_dot_nt(kb[i].astype(BF16), kn16[i]) * decay[i] * off_diag for i in hd]
    qk = [_dot_nt(qn[i].astype(BF16), kn16[i]) * decay[i] for i in hd]
    s16 = [states[i].astype(BF16) for i in hd]
    o_state = [_dot((qn[i] * jnp.exp(cs_cols[i])).astype(BF16), s16[i]) for i in hd]
    tinv = [eye - a[i] for i in hd]
    p = [_dot_x3(a[i], a[i]) for i in hd]
    steps = max(1, int(np.ceil(np.log2(lc))) - 1)
    for st in range(steps):
        tinv = [tinv[i] + _dot_x3(tinv[i], p[i]) for i in hd]
        if st + 1 < steps:
            p = [_dot_x3(p[i], p[i]) for i in hd]
    rhs = [jnp.concatenate([vs[i] * betas[i], kb[i] * jnp.exp(cs_cols[i])], axis=1).astype(BF16) for i in hd]
    sol = [_dot(tinv[i].astype(BF16), rhs[i]) for i in hd]
    vn16 = [(sol[i][:, :dv] - _dot(sol[i][:, dv:].astype(BF16), s16[i])).astype(BF16) for i in hd]
    o = [o_state[i] + _dot(qk[i].astype(BF16), vn16[i]) for i in hd]
    k_end = [(kn[i] * jnp.exp(cs_last[i] - cs_cols[i])).astype(BF16) for i in hd]
    s_new = [states[i] * jnp.exp(cs_last[i]) + _dot_tn(k_end[i], vn16[i]) for i in hd]
    outs = [_rms_rows(o[i], gn) * _silu(gates[i]) for i in hd]
    return outs, s_new


def _gdn_body(q_ref, k_ref, v_ref, gate_ref, sm_ref, alog_ref, dtb_ref, gn_ref, s0_ref, tril_ref, tril3_ref, eye_ref,
              eyeh_ref, o_ref, so_ref, s_scr, cst_scr, *, nc, nheads, a_col0, hps):
    hb = pl.program_id(1)
    c = pl.program_id(2)
    d = q_ref.shape[1] // hps

    @pl.when(c == 0)
    def _():
        s_scr[...] = s0_ref[0]

    a_raw = sm_ref[:, a_col0:a_col0 + nheads]
    b_raw = sm_ref[:, a_col0 + nheads:a_col0 + 2 * nheads]
    gl = -jnp.exp(alog_ref[...]) * _softplus(a_raw + dtb_ref[...])
    beta_all = _sigmoid(b_raw)
    cs = _mask_dot(tril3_ref[...], gl)
    cst_scr[...] = _mask_dot_nt(eyeh_ref[...], cs)
    tri = tril_ref[...].astype(F32) > 0.0
    eye = eye_ref[...].astype(F32)
    gn = gn_ref[...]
    sls = [slice(j * d, (j + 1) * d) for j in range(hps)]
    hidx = [hb * hps + j for j in range(hps)]
    outs, s_new = _gdn_heads(
        [q_ref[:, sl] for sl in sls], [k_ref[:, sl] for sl in sls], [v_ref[:, sl] for sl in sls],
        [gate_ref[:, sl] for sl in sls], [_pick_lane(beta_all, h) for h in hidx], [_pick_lane(cs, h) for h in hidx],
        [cst_scr[pl.ds(h, 1), :] for h in hidx], [s_scr[j] for j in range(hps)], tri, eye, gn)
    for j in range(hps):
        s_scr[j] = s_new[j]
        o_ref[:, sls[j]] = outs[j].astype(o_ref.dtype)

    @pl.when(c == nc - 1)
    def _():
        so_ref[0] = s_scr[...]


GDN_HEADS_PER_STEP = 16


def _gdn(qkv, pm, gate_col0, sm, a_col0, a_log, dt_bias, gn, s0, bsz, l, nheads, dk, dv):
    hps = GDN_HEADS_PER_STEP
    w = hps * dv
    assert dk == dv and gate_col0 % w == 0 and nheads % hps == 0
    t = bsz * l
    lc = CHUNK
    nc = l // lc
    nhb = nheads // hps
    tril, tril3, eye = _tri_masks(lc)
    eyeh = jnp.asarray(np.eye(nheads, dtype=np.float32), dtype=BF16)
    rows = lambda bi, hi, ci: bi * nc + ci
    const2 = lambda bi, hi, ci: (0, 0)
    col = lambda k: pl.BlockSpec((lc, w), lambda bi, hi, ci: (rows(bi, hi, ci), k * nhb + hi))
    state = pl.BlockSpec((1, hps, dk, dv), lambda bi, hi, ci: (bi, hi, 0, 0))
    o, s_new = pl.pallas_call(
        functools.partial(_gdn_body, nc=nc, nheads=nheads, a_col0=a_col0, hps=hps),
        grid=(bsz, nhb, nc),
        in_specs=[col(0), col(1), col(2),
                  pl.BlockSpec((lc, w), lambda bi, hi, ci: (rows(bi, hi, ci), gate_col0 // w + hi)),
                  pl.BlockSpec((lc, sm.shape[1]), lambda bi, hi, ci: (rows(bi, hi, ci), 0)),
                  pl.BlockSpec((1, nheads), const2),
                  pl.BlockSpec((1, nheads), const2),
                  pl.BlockSpec((1, dv), const2),
                  state,
                  pl.BlockSpec((lc, lc), const2),
                  pl.BlockSpec((lc, 3 * lc), const2),
                  pl.BlockSpec((lc, lc), const2),
                  pl.BlockSpec((nheads, nheads), const2)],
        out_specs=[pl.BlockSpec((lc, w), lambda bi, hi, ci: (rows(bi, hi, ci), hi)), state],
        out_shape=[jax.ShapeDtypeStruct((t, nheads * dv), BF16),
                   jax.ShapeDtypeStruct((bsz, nheads, dk, dv), F32)],
        scratch_shapes=[pltpu.VMEM((hps, dk, dv), F32), pltpu.VMEM((nheads, lc), F32)],
        compiler_params=_params("parallel", "parallel", "arbitrary"),
        name="gdn_scan",
    )(qkv, qkv, qkv, pm, sm, a_log.reshape(1, nheads), dt_bias.reshape(1, nheads), gn.reshape(1, dv), s0,
      tril, tril3, eye, eyeh)
    return o, s_new


def _hgrn_levels(lc):
    hs = []
    h = lc // 2
    while h >= 1:
        hs.append(h)
        h //= 2
    return hs


def _hgrn_masks(lc):
    r = np.arange(lc)
    t, u = r[:, None], r[None, :]
    mats = [u <= t, u > t]
    for h in _hgrn_levels(lc):
        start = (t // h) * h
        mats.append((u >= start) & (u <= t))
        mats.append((u > t) & (u <= start + h - 1))
    return jnp.asarray(np.tile(np.concatenate(mats, axis=0).astype(np.float32), (1, 3)), dtype=BF16)


def _hgrn_heads(qs, hfs, vs, gates, lbs, gn, sts, mst16):
    n = len(qs)
    hd = range(n)
    lc, d = qs[0].shape
    logf = [jnp.log(lbs[i] + (1.0 - lbs[i]) * _sigmoid(hfs[i])) for i in hd]
    kk = [(1.0 - lbs[i]) * _sigmoid(-hfs[i]) for i in hd]
    cums = [_mask_dot(mst16, logf[i]) for i in hd]
    v16 = [vs[i].astype(BF16) for i in hd]
    st16 = [sts[i].astype(BF16) for i in hd]
    o_state = [_dot_nt((qs[i] * jnp.exp(cums[i][0:lc])).astype(BF16), st16[i]) for i in hd]

    row = lax.broadcasted_iota(jnp.int32, (lc, 1), 0)
    ti = lax.broadcasted_iota(jnp.int32, (lc, lc), 0)
    si = lax.broadcasted_iota(jnp.int32, (lc, lc), 1)
    att = [jnp.where(ti == si, jnp.sum(qs[i] * kk[i], axis=1, keepdims=True), 0.0) for i in hd]
    for li, hs in enumerate(_hgrn_levels(lc)):
        sh = int(np.log2(hs))
        upper = ((row >> sh) & 1) == 1
        same_block = (ti >> (sh + 1)) == (si >> (sh + 1))
        for i in hd:
            seg = cums[i][(2 + 2 * li) * lc:(3 + 2 * li) * lc]
            rest = cums[i][(3 + 2 * li) * lc:(4 + 2 * li) * lc]
            ql = jnp.where(upper, qs[i] * jnp.exp(seg), 0.0)
            kl = jnp.where(upper, 0.0, kk[i] * jnp.exp(rest))
            att[i] = att[i] + jnp.where(same_block, _dot_nt(ql.astype(BF16), kl.astype(BF16)), 0.0)

    outs, st_new = [], []
    for i in hd:
        o = o_state[i] + _dot(att[i].astype(BF16), v16[i])
        k_end = kk[i] * jnp.exp(cums[i][lc:2 * lc])
        st_new.append(sts[i] * jnp.exp(cums[i][lc - 1:lc, :]) + _dot_tn(v16[i], k_end.astype(BF16)))
        outs.append(_rms_rows(o, gn) * _silu(gates[i]))
    return outs, st_new


def _hgrn_body(q_ref, f_ref, i_ref, gate_ref, lb_ref, gn_ref, s0_ref, mst_ref, o_ref, so_ref, st_scr, *, nc, hps):
    c = pl.program_id(2)
    d = q_ref.shape[1] // hps

    @pl.when(c == 0)
    def _():
        for j in range(hps):
            st_scr[j] = s0_ref[0, j].T

    mst16 = mst_ref[...]
    gn = gn_ref[...]
    sls = [slice(j * d, (j + 1) * d) for j in range(hps)]
    outs, st_new = _hgrn_heads(
        [q_ref[:, sl] for sl in sls], [f_ref[:, sl] for sl in sls], [i_ref[:, sl] for sl in sls],
        [gate_ref[:, sl] for sl in sls], [lb_ref[:, sl] for sl in sls], gn, [st_scr[j] for j in range(hps)], mst16)
    for j in range(hps):
        st_scr[j] = st_new[j]
        o_ref[:, sls[j]] = outs[j].astype(o_ref.dtype)

    @pl.when(c == nc - 1)
    def _():
        for j in range(hps):
            so_ref[0, j] = st_scr[j].T


HGRN_HEADS_PER_STEP = 32


def _hgrn(pm, lb, gn, s0, bsz, l, nheads, dk):
    hps = HGRN_HEADS_PER_STEP
    assert nheads % hps == 0
    w = hps * dk
    nhb = nheads // hps
    t = bsz * l
    lc = CHUNK
    nc = l // lc
    mst = _hgrn_masks(lc)
    rows = lambda bi, hi, ci: bi * nc + ci
    const2 = lambda bi, hi, ci: (0, 0)
    col = lambda k: pl.BlockSpec((lc, w), lambda bi, hi, ci: (rows(bi, hi, ci), k * nhb + hi))
    state = pl.BlockSpec((1, hps, dk, dk), lambda bi, hi, ci: (bi, hi, 0, 0))
    o, s_new = pl.pallas_call(
        functools.partial(_hgrn_body, nc=nc, hps=hps),
        grid=(bsz, nhb, nc),
        in_specs=[col(0), col(1), col(2), col(3),
                  pl.BlockSpec((1, w), lambda bi, hi, ci: (0, hi)),
                  pl.BlockSpec((1, dk), const2),
                  state,
                  pl.BlockSpec(mst.shape, const2)],
        out_specs=[pl.BlockSpec((lc, w), lambda bi, hi, ci: (rows(bi, hi, ci), hi)), state],
        out_shape=[jax.ShapeDtypeStruct((t, nheads * dk), BF16),
                   jax.ShapeDtypeStruct((bsz, nheads, dk, dk), F32)],
        scratch_shapes=[pltpu.VMEM((hps, dk, dk), F32)],
        compiler_params=_params("parallel", "parallel", "arbitrary"),
        name="hgrn_scan",
    )(pm, pm, pm, pm, lb.reshape(1, nheads * dk), gn.reshape(1, dk), s0, mst)
    return o, s_new


MLA_SUB_ROWS = 1024


def _mla_body(qi_ref, ki_ref, last_ref, q_ref, k_ref, o_ref, m_scr, l_scr, acc_scr, *, tq, tk, t_past, hs):
    step = pl.program_id(1)
    iq = qi_ref[step]
    ikv = ki_ref[step]
    nh, _, dq = q_ref.shape
    r = o_ref.shape[2]
    rows = hs * tq
    first_limit = t_past + iq * tq + CHUNK

    @pl.when(ikv == 0)
    def _():
        m_scr[...] = jnp.full_like(m_scr, MASKED)
        l_scr[...] = jnp.zeros_like(l_scr)
        acc_scr[...] = jnp.zeros_like(acc_scr)

    def update(masked):
        kc = k_ref[0]
        vals = kc[:, 0:r]
        if masked:
            tok = lax.broadcasted_iota(jnp.int32, (rows, 1), 0) & (tq - 1)
            limit = t_past + iq * tq + ((tok >> CHUNK_SHIFT) + 1) * CHUNK
            visible = (ikv * tk + lax.broadcasted_iota(jnp.int32, (1, tk), 1)) < limit

        def scores(g):
            return _dot_nt(q_ref[g * hs:(g + 1) * hs].reshape(rows, dq), kc)

        ngroups = nh // hs
        s_next = scores(0)
        for g in range(ngroups):
            rsl = slice(g * rows, (g + 1) * rows)
            s = s_next
            if g + 1 < ngroups:
                s_next = scores(g + 1)
            if masked:
                s = jnp.where(visible, s, MASKED)
            m_old = m_scr[rsl]
            m_new = jnp.maximum(m_old, jnp.max(s, axis=1, keepdims=True))
            alpha = jnp.exp(m_old - m_new)
            p = jnp.exp(s - m_new)
            l_scr[rsl] = alpha * l_scr[rsl] + jnp.sum(p, axis=1, keepdims=True)
            acc_scr[rsl] = alpha * acc_scr[rsl] + _dot(p.astype(BF16), vals)
            m_scr[rsl] = m_new

    fully_visible = (ikv + 1) * tk <= first_limit
    pl.when(fully_visible)(lambda: update(False))
    pl.when(jnp.logical_not(fully_visible))(lambda: update(True))

    @pl.when(last_ref[step] == 1)
    def _():
        o = acc_scr[...] / l_scr[...]
        o_ref[...] = o.reshape(nh, tq, r).astype(o_ref.dtype)


def _mla_attention(q, k, r, bsz, l, t_past, tq, tk):
    nh, t, dq = q.shape
    tkeys = k.shape[1]
    assert l % tq == 0 and tq % CHUNK == 0 and t_past % CHUNK == 0 and tkeys % tk == 0 and tkeys >= t_past + l
    assert tq & (tq - 1) == 0 and k.shape[2] == dq and r % LANES == 0
    nq = l // tq
    hs = min(nh, max(1, MLA_SUB_ROWS // tq))
    assert nh % hs == 0
    qi, ki, last = [], [], []
    for qb in range(nq):
        need = -(-(t_past + (qb + 1) * tq) // tk)
        qi += [qb] * need
        ki += list(range(need))
        last += [0] * (need - 1) + [1]
    tables = [jnp.asarray(np.asarray(a, np.int32)) for a in (qi, ki, last)]
    qmap = lambda bi, si, qi_ref, ki_ref, last_ref: (0, bi * nq + qi_ref[si], 0)
    kmap = lambda bi, si, qi_ref, ki_ref, last_ref: (bi, ki_ref[si], 0)
    return pl.pallas_call(
        functools.partial(_mla_body, tq=tq, tk=tk, t_past=t_past, hs=hs),
        grid_spec=pltpu.PrefetchScalarGridSpec(
            num_scalar_prefetch=3,
            grid=(bsz, len(qi)),
            in_specs=[pl.BlockSpec((nh, tq, dq), qmap),
                      pl.BlockSpec((1, tk, dq), kmap)],
            out_specs=pl.BlockSpec((nh, tq, r), qmap),
            scratch_shapes=[pltpu.VMEM((nh * tq, 1), F32), pltpu.VMEM((nh * tq, 1), F32),
                            pltpu.VMEM((nh * tq, r), F32)]),
        out_shape=jax.ShapeDtypeStruct((nh, t, r), BF16),
        compiler_params=_params("parallel", "arbitrary"),
        name="mla_attention",
    )(*tables, q, k)


def _xattn_body(q_ref, k_ref, v_ref, o_ref, *, nheads, hd):
    q = q_ref[...]
    k = k_ref[0]
    v = v_ref[0]
    outs = []
    for h in range(nheads):
        sl = slice(h * hd, (h + 1) * hd)
        s = _dot_nt(q[:, sl], k[:, sl]) * hd ** -0.5
        s = s - jnp.max(s, axis=1, keepdims=True)
        p = jnp.exp(s)
        p = p / jnp.sum(p, axis=1, keepdims=True)
        outs.append(_dot(p.astype(BF16), v[:, sl]))
    o_ref[...] = jnp.concatenate(outs, axis=1).astype(o_ref.dtype)


def _cross_attention(q, mem_k, mem_v, bsz, l, nheads, hd, tl_cap=512):
    t, d = q.shape
    m = mem_k.shape[1]
    tl = _tile(l, tl_cap, 8)
    nl = l // tl
    return pl.pallas_call(
        functools.partial(_xattn_body, nheads=nheads, hd=hd),
        grid=(bsz, nl),
        in_specs=[pl.BlockSpec((tl, d), lambda bi, li: (bi * nl + li, 0)),
                  pl.BlockSpec((1, m, d), lambda bi, li: (bi, 0, 0)),
                  pl.BlockSpec((1, m, d), lambda bi, li: (bi, 0, 0))],
        out_specs=pl.BlockSpec((tl, d), lambda bi, li: (bi * nl + li, 0)),
        out_shape=jax.ShapeDtypeStruct((t, d), BF16),
        compiler_params=_params("parallel", "parallel"),
        name="cross_attention",
    )(q, mem_k, mem_v)


SSD_HEADS, SSD_HEAD_DIM, SSD_GROUPS, SSD_STATE = 64, 64, 8, 128
SSD_INNER = SSD_HEADS * SSD_HEAD_DIM
SSD_CONV_DIM = SSD_INNER + 2 * SSD_GROUPS * SSD_STATE
GDN_HEADS, GDN_DK, GDN_DV = 32, 128, 128
GDN_CONV_DIM = 2 * GDN_HEADS * GDN_DK + GDN_HEADS * GDN_DV
MLA_HEADS, MLA_Q_RANK, MLA_KV_RANK, MLA_NOPE, MLA_ROPE, MLA_V = 32, 1024, 512, 128, 64, 128
HG_HEADS, HG_DK = 32, 128
HG_W = HG_HEADS * HG_DK
XA_HEADS, XA_HEAD_DIM = 4, 128
MLA_TK = 1024
MLA_TQ = 128


def _rope_tables(pos, p):
    half = p // 2
    inv = ROPE_THETA ** (-jnp.arange(half, dtype=F32) / half)
    ang = pos[:, None] * inv[None, :]
    return jnp.cos(ang), jnp.sin(ang)


def _prep_weights(w):
    out = {}
    e_sizes = [SSD_INNER, SSD_CONV_DIM, SSD_HEADS, GDN_CONV_DIM, GDN_HEADS, GDN_HEADS, GDN_HEADS * GDN_DV]
    e_off = np.concatenate([[0], np.cumsum(e_sizes)])
    wi = w["w_in_even"]
    sec = lambda a, offs, i: a[:, :, offs[i]:offs[i + 1]]
    out["w_in_even_main"] = jnp.concatenate(
        [sec(wi, e_off, 0), sec(wi, e_off, 1), sec(wi, e_off, 3), sec(wi, e_off, 6)], axis=-1).astype(BF16)
    out["w_in_even_small"] = jnp.concatenate(
        [sec(wi, e_off, 2), sec(wi, e_off, 4), sec(wi, e_off, 5)], axis=-1).astype(BF16)
    o_sizes = [MLA_Q_RANK, MLA_KV_RANK, MLA_ROPE, HG_W, HG_W, HG_W, HG_W]
    o_off = np.concatenate([[0], np.cumsum(o_sizes)])
    wo = w["w_in_odd"]
    out["w_in_odd_main"] = jnp.concatenate(
        [sec(wo, o_off, 3), sec(wo, o_off, 4), sec(wo, o_off, 5), sec(wo, o_off, 6), sec(wo, o_off, 0)],
        axis=-1).astype(BF16)
    out["w_in_odd_small"] = jnp.concatenate([sec(wo, o_off, 1), sec(wo, o_off, 2)], axis=-1).astype(BF16)
    n_odd = wo.shape[0]
    uq = w["mla_w_uq"].reshape(n_odd, MLA_Q_RANK, MLA_HEADS, MLA_NOPE + MLA_ROPE)
    out["w_uq"] = jnp.concatenate(
        [uq[..., :MLA_NOPE].reshape(n_odd, MLA_Q_RANK, MLA_HEADS * MLA_NOPE),
         uq[..., MLA_NOPE:].reshape(n_odd, MLA_Q_RANK, MLA_HEADS * MLA_ROPE)], axis=-1).astype(BF16)
    out["w_uk"] = jnp.transpose(w["mla_w_uk"], (0, 2, 3, 1)).astype(BF16)
    out["w_uv"] = jnp.transpose(w["mla_w_uv"], (0, 2, 1, 3)).astype(BF16)
    for name in ("w_out_even", "w_out_odd", "xa_w_q", "xa_w_k", "xa_w_v", "xa_w_o", "ffn_w1", "ffn_w2"):
        out[name] = w[name].astype(BF16)
    return out


def _even_mixer(hn, w, wb, e, bsz, l, ssd_s, ssd_buf, gdn_s, gdn_buf):
    pm = _matmul([hn], [wb["w_in_even_main"][e]], F32)
    sm = _matmul([hn], [wb["w_in_even_small"][e]], F32)
    xbc_col0, qkv_col0 = SSD_INNER, SSD_INNER + SSD_CONV_DIM
    gate_col0 = qkv_col0 + GDN_CONV_DIM
    xbc = _conv_silu(pm, xbc_col0, SSD_CONV_DIM, ssd_buf, w["ssd_conv_w"][e], w["ssd_conv_b"][e], bsz, l)
    qkv = _conv_silu(pm, qkv_col0, GDN_CONV_DIM, gdn_buf, w["gdn_conv_w"][e],
                     jnp.zeros((GDN_CONV_DIM,), F32), bsz, l)
    y, ssd_new = _ssd(xbc, pm, 0, sm, w["ssd_dt_bias"][e], w["ssd_a_log"][e], w["ssd_d"][e], w["ssd_norm"][e],
                      ssd_s.reshape(bsz, SSD_INNER, SSD_STATE), bsz, l, SSD_HEADS, SSD_HEAD_DIM, SSD_GROUPS,
                      SSD_STATE)
    o, gdn_new = _gdn(qkv, pm, gate_col0, sm, SSD_HEADS, w["gdn_a_log"][e], w["gdn_dt_bias"][e], w["gdn_norm"][e],
                      gdn_s, bsz, l, GDN_HEADS, GDN_DK, GDN_DV)
    w_out = wb["w_out_even"][e]
    mix = _matmul([y, o], [w_out, w_out], BF16, b_rows=[0, SSD_INNER])
    pm3 = pm.reshape(bsz, l, pm.shape[1])
    tail = pm3[:, l - (CONV_W - 1):]
    states = (ssd_new.reshape(bsz, SSD_HEADS, SSD_HEAD_DIM, SSD_STATE), tail[:, :, xbc_col0:qkv_col0],
              gdn_new, tail[:, :, qkv_col0:gate_col0])
    return mix, states


def _odd_mixer(hn, w, wb, j, lb, bsz, l, ckv_past, kr_past, hg_s):
    t_past = ckv_past.shape[1]
    pm = _matmul([hn], [wb["w_in_odd_main"][j]], F32)
    sm = _matmul([hn], [wb["w_in_odd_small"][j]], F32)
    posf = (t_past + jnp.arange(l, dtype=jnp.int32)).astype(F32)
    scale = (MLA_NOPE + MLA_ROPE) ** -0.5
    cqn = _rms(pm, w["mla_q_norm"][j], BF16, col_block=4 * HG_W // MLA_Q_RANK, width=MLA_Q_RANK)
    q = _matmul([cqn], [wb["w_uq"][j]], F32)
    lane_pad = (-MLA_ROPE) % LANES
    cos, sin = _rope_tables(posf, MLA_ROPE)
    cos2 = jnp.tile(jnp.concatenate([cos, cos], axis=1), (bsz, 1))
    sin2 = jnp.tile(jnp.concatenate([-sin, sin], axis=1), (bsz, 1))
    q_att = _head_split_matmul(q, MLA_HEADS * MLA_NOPE, wb["w_uk"][j], scale, cos2, sin2)
    c_kv = _rms(sm, w["mla_kv_norm"][j], F32, col_block=0, width=MLA_KV_RANK).reshape(bsz, l, MLA_KV_RANK)
    kr = sm[:, MLA_KV_RANK:].reshape(bsz, l, MLA_ROPE)
    kr1, kr2 = kr[..., :MLA_ROPE // 2], kr[..., MLA_ROPE // 2:]
    k_rope = jnp.concatenate([kr1 * cos - kr2 * sin, kr2 * cos + kr1 * sin], axis=-1)
    tkeys = t_past + l
    tk = _tile(tkeys, MLA_TK, CHUNK)
    k_att = jnp.concatenate([jnp.concatenate([ckv_past, c_kv], axis=1).astype(BF16),
                             jnp.concatenate([kr_past, k_rope], axis=1).astype(BF16)], axis=2)
    k_att = jnp.pad(k_att, ((0, 0), (0, 0), (0, lane_pad)))
    tq = MLA_TQ if l % MLA_TQ == 0 else CHUNK
    o_lat = _mla_attention(q_att, k_att, MLA_KV_RANK, bsz, l, t_past, tq, tk)
    o_mla = _head_merge_matmul(o_lat, wb["w_uv"][j])
    o_hg, hg_new = _hgrn(pm, lb, w["hg_norm"][j], hg_s, bsz, l, HG_HEADS, HG_DK)
    w_out = wb["w_out_odd"][j]
    mix = _matmul([o_mla, o_hg], [w_out, w_out], BF16, b_rows=[0, MLA_HEADS * MLA_V])
    return mix, (c_kv, k_rope, hg_new)


def _forward(x, mem_k, mem_v, ssd_s, ssd_buf, gdn_s, gdn_buf, ckv_past, kr_past, hg_s, w, wb, lbs):
    bsz, l, d = x.shape
    depth = w["norm_g"].shape[0]
    x = x.reshape(bsz * l, d)
    hn = _rms(x, w["norm_g"][0, 0], BF16)
    ev, od = [], []
    for layer in range(depth):
        ng = w["norm_g"][layer]
        if layer % 2 == 0:
            e = layer // 2
            mix, st = _even_mixer(hn, w, wb, e, bsz, l, ssd_s[e], ssd_buf[e], gdn_s[e], gdn_buf[e])
            ev.append(st)
        else:
            j = layer // 2
            mix, st = _odd_mixer(hn, w, wb, j, lbs[layer], bsz, l, ckv_past[j], kr_past[j], hg_s[j])
            od.append(st)
        x, hn = _add_rms(x, mix, ng[1], ng[2])
        q = _matmul([hn], [wb["xa_w_q"][layer]], BF16)
        xo = _cross_attention(q, mem_k[layer], mem_v[layer], bsz, l, XA_HEADS, XA_HEAD_DIM)
        xa = _matmul([xo], [wb["xa_w_o"][layer]], BF16)
        x, hn = _add_rms(x, xa, ng[3], ng[4])
        h1 = _matmul([hn], [wb["ffn_w1"][layer]], BF16, act="relu2")
        f = _matmul([h1], [wb["ffn_w2"][layer]], BF16)
        x, hn = _add_rms(x, f, ng[5], w["norm_g"][layer + 1, 0] if layer + 1 < depth else None)
    stack = lambda items, i: jnp.stack([it[i] for it in items])
    return (x.reshape(bsz, l, d), stack(ev, 0), stack(ev, 1), stack(ev, 2), stack(ev, 3),
            stack(od, 0), stack(od, 1), stack(od, 2))


def kernel(x_prompt, x_sample, mem_prompt, state_ssd, state_ssd_conv, state_gdn, state_gdn_conv, cache_mla_ckv, cache_mla_krope, state_hgrn, cache_mem_k, cache_mem_v, norm_g, w_in_even, w_out_even, ssd_conv_w, ssd_conv_b, ssd_dt_bias, ssd_a_log, ssd_d, ssd_norm, gdn_conv_w, gdn_a_log, gdn_dt_bias, gdn_norm, w_in_odd, w_out_odd, mla_q_norm, mla_w_uq, mla_kv_norm, mla_w_uk, mla_w_uv, hg_lower_bound, hg_norm, xa_mem_norm, xa_w_q, xa_w_k, xa_w_v, xa_w_o, ffn_w1, ffn_w2):
    w = dict(norm_g=norm_g, w_in_even=w_in_even, w_out_even=w_out_even, ssd_conv_w=ssd_conv_w, ssd_conv_b=ssd_conv_b,
             ssd_dt_bias=ssd_dt_bias, ssd_a_log=ssd_a_log, ssd_d=ssd_d, ssd_norm=ssd_norm, gdn_conv_w=gdn_conv_w,
             gdn_a_log=gdn_a_log, gdn_dt_bias=gdn_dt_bias, gdn_norm=gdn_norm, w_in_odd=w_in_odd, w_out_odd=w_out_odd,
             mla_q_norm=mla_q_norm, mla_w_uq=mla_w_uq, mla_kv_norm=mla_kv_norm, mla_w_uk=mla_w_uk, mla_w_uv=mla_w_uv,
             hg_norm=hg_norm, xa_w_q=xa_w_q, xa_w_k=xa_w_k, xa_w_v=xa_w_v, xa_w_o=xa_w_o, ffn_w1=ffn_w1,
             ffn_w2=ffn_w2)
    wb = _prep_weights(w)
    depth = norm_g.shape[0]
    n_even, n_odd = (depth + 1) // 2, depth // 2
    lbs = jnp.cumsum(jax.nn.softmax(hg_lower_bound.astype(F32), axis=0), axis=0)
    lbs = lbs - lbs[0]

    b, m, d = mem_prompt.shape
    xa_dim = XA_HEADS * XA_HEAD_DIM
    mk, mv = [], []
    for layer in range(depth):
        mn = _rms(mem_prompt.reshape(b * m, d), xa_mem_norm[layer], BF16)
        mk.append(_matmul([mn], [wb["xa_w_k"][layer]], F32).reshape(b, m, xa_dim))
        mv.append(_matmul([mn], [wb["xa_w_v"][layer]], F32).reshape(b, m, xa_dim))
    p_mem_k = jnp.stack(mk)
    p_mem_v = jnp.stack(mv)
    dt = x_prompt.dtype
    prompt = _forward(
        x_prompt, p_mem_k.astype(BF16), p_mem_v.astype(BF16),
        jnp.zeros((n_even, b, SSD_HEADS, SSD_HEAD_DIM, SSD_STATE), dt),
        jnp.zeros((n_even, b, CONV_W - 1, SSD_CONV_DIM), dt),
        jnp.zeros((n_even, b, GDN_HEADS, GDN_DK, GDN_DV), dt),
        jnp.zeros((n_even, b, CONV_W - 1, GDN_CONV_DIM), dt),
        jnp.zeros((n_odd, b, 0, MLA_KV_RANK), dt),
        jnp.zeros((n_odd, b, 0, MLA_ROPE), dt),
        jnp.zeros((n_odd, b, HG_HEADS, HG_DK, HG_DK), dt),
        w, wb, lbs)
    db = x_sample.shape[0]
    sample = _forward(
        x_sample, cache_mem_k.reshape(depth, db, m, xa_dim).astype(BF16),
        cache_mem_v.reshape(depth, db, m, xa_dim).astype(BF16),
        state_ssd, state_ssd_conv, state_gdn, state_gdn_conv, cache_mla_ckv, cache_mla_krope, state_hgrn,
        w, wb, lbs)
    y_prompt, p_rest = prompt[0], prompt[1:]
    y_sample, s_rest = sample[0], sample[1:]
    return (y_prompt, y_sample, *p_rest,
            p_mem_k.reshape(depth, b, m, XA_HEADS, XA_HEAD_DIM), p_mem_v.reshape(depth, b, m, XA_HEADS, XA_HEAD_DIM),
            *s_rest)
```
